```python
import jax, jax.numpy as jnp
from jax import lax
import numpy as np

D_MODEL = 1024
BATCH = 2
SEQ = 8192
DEPTH = 2
DEC_BATCH = 128
DEC_SEQ = 8
PAST_LEN = 8192
PAGE_SIZE = 128

N_MIXERS = 2
N_RET = (DEPTH + 1) // 2
N_MLA = DEPTH // 2
RET_HEADS = 4
RET_DK = D_MODEL // RET_HEADS
RET_DV = 2 * RET_DK
RET_CHUNK = 128
ROPE_BASE = 10000.0
MLA_HEADS = 16
QK_NOPE = 64
QK_ROPE = 32
V_HEAD = 64
KV_RANK = 256
Q_RANK = 768
Q_BLOCK = 128
MLA_SCALE = (QK_NOPE + QK_ROPE) ** -0.5
N_EXPERTS = 32
TOP_K = 4
D_EXPERT = D_MODEL
SWIGLU_LIMIT = 7.0
SWIGLU_ALPHA = 1.702
MOE_BLOCK = 128
NORM_EPS = 1e-6
GN_EPS = 1e-6
POOL_NUM, POOL_DEN = 5, 4

kernel_name = "retnet_mla_hybrid_moe_step"


def rms_norm(x, g):
    xf = x.astype(jnp.float32)
    y = xf * lax.rsqrt(jnp.mean(xf * xf, axis=-1, keepdims=True) + NORM_EPS)
    return (y * g.astype(jnp.float32)).astype(x.dtype)


def rope(x, pos):
    half = x.shape[-1] // 2
    inv = ROPE_BASE ** (-jnp.arange(half, dtype=jnp.float32) / half)
    ang = pos.astype(jnp.float32)[:, None] * inv[None, :]
    cos = jnp.cos(ang)[None, :, None, :]
    sin = jnp.sin(ang)[None, :, None, :]
    xf = x.astype(jnp.float32)
    x1, x2 = xf[..., :half], xf[..., half:]
    return jnp.concatenate([x1 * cos - x2 * sin, x1 * sin + x2 * cos], axis=-1).astype(x.dtype)


def ret_log_gamma():
    return jnp.log(1.0 - 2.0 ** (-5.0 - jnp.arange(RET_HEADS, dtype=jnp.float32)))


def retention_project(h, pos, wq, wk, wv, wg):
    B, T, _ = h.shape
    q = rope((h @ wq).reshape(B, T, RET_HEADS, RET_DK), pos).astype(jnp.float32)
    k = rope((h @ wk).reshape(B, T, RET_HEADS, RET_DK), pos).astype(jnp.float32) * (RET_DK ** -0.5)
    v = (h @ wv).reshape(B, T, RET_HEADS, RET_DV).astype(jnp.float32)
    g = h @ wg
    return q, k, v, g


def retention_chunk(state, q, k, v, log_gamma):
    L = q.shape[1]
    idx = jnp.arange(L, dtype=jnp.float32)
    diff = idx[:, None] - idx[None, :]
    decay = jnp.where(diff[None] >= 0,
                      jnp.exp(jnp.maximum(diff, 0.0)[None] * log_gamma[:, None, None]), 0.0)
    scores = jnp.einsum('bihd,bjhd->bhij', q, k) * decay[None]
    inner = jnp.einsum('bhij,bjhv->bihv', scores, v)
    q_decay = jnp.exp((idx[:, None] + 1.0) * log_gamma[None, :])
    cross = jnp.einsum('bihd,bhdv->bihv', q, state) * q_decay[None, :, :, None]
    k_decay = jnp.exp((L - 1.0 - idx)[:, None] * log_gamma[None, :])
    new_state = (jnp.exp(L * log_gamma)[None, :, None, None] * state
                 + jnp.einsum('bjhd,bjhv->bhdv', k * k_decay[None, :, :, None], v))
    return new_state, inner + cross


def retention_output(o, g, gn, wo, dtype):
    B, T = o.shape[0], o.shape[1]
    mu = jnp.mean(o, axis=-1, keepdims=True)
    var = jnp.mean(jnp.square(o - mu), axis=-1, keepdims=True)
    o = ((o - mu) * lax.rsqrt(var + GN_EPS)).reshape(B, T, RET_HEADS * RET_DV) * gn.astype(jnp.float32)
    return (jax.nn.silu(g.astype(jnp.float32)) * o).astype(dtype) @ wo


def retention_prompt(h, wq, wk, wv, wg, gn, wo):
    B, S, _ = h.shape
    pos = jnp.arange(S)
    q, k, v, g = retention_project(h, pos, wq, wk, wv, wg)
    C = min(RET_CHUNK, S)
    nc = S // C
    lg = ret_log_gamma()

    def to_chunks(t):
        return t.reshape(B, nc, C, *t.shape[2:]).swapaxes(0, 1)

    def step(st, inp):
        qc, kc, vc = inp
        return retention_chunk(st, qc, kc, vc, lg)

    s0 = jnp.zeros((B, RET_HEADS, RET_DK, RET_DV), jnp.float32)
    s_fin, oc = lax.scan(step, s0, (to_chunks(q), to_chunks(k), to_chunks(v)))
    o = oc.swapaxes(0, 1).reshape(B, S, RET_HEADS, RET_DV)
    return retention_output(o, g, gn, wo, h.dtype), s_fin


def retention_sample(h, state, wq, wk, wv, wg, gn, wo):
    T = h.shape[1]
    pos = PAST_LEN + jnp.arange(T)
    q, k, v, g = retention_project(h, pos, wq, wk, wv, wg)
    s_new, o = retention_chunk(state.astype(jnp.float32), q, k, v, ret_log_gamma())
    return retention_output(o, g, gn, wo, h.dtype), s_new


def mla_project(h, pos, wdq, gq, wuq, wdkv, gkv):
    B, T, _ = h.shape
    cq = rms_norm(h @ wdq, gq)
    q = (cq @ wuq).reshape(B, T, MLA_HEADS, QK_NOPE + QK_ROPE)
    q_nope = q[..., :QK_NOPE]
    q_pe = rope(q[..., QK_NOPE:], pos)
    kv = h @ wdkv
    ckv = rms_norm(kv[..., :KV_RANK], gkv)
    kpe = rope(kv[..., None, KV_RANK:], pos)[:, :, 0]
    return q_nope, q_pe, ckv, kpe


def mla_prompt(h, wdq, gq, wuq, wdkv, gkv, wuk, wuv, wo):
    B, S, _ = h.shape
    pos = jnp.arange(S)
    q_nope, q_pe, ckv, kpe = mla_project(h, pos, wdq, gq, wuq, wdkv, gkv)
    k_nope = jnp.einsum('bsc,chd->bshd', ckv, wuk)
    v = jnp.einsum('bsc,chd->bshd', ckv, wuv)
    QB = min(Q_BLOCK, S)
    nb = S // QB
    qn_b = q_nope.reshape(B, nb, QB, MLA_HEADS, QK_NOPE).swapaxes(0, 1)
    qp_b = q_pe.reshape(B, nb, QB, MLA_HEADS, QK_ROPE).swapaxes(0, 1)
    kpos = jnp.arange(S)

    def attend(args):
        qn, qp, i = args
        s = (jnp.einsum('bqhd,bkhd->bhqk', qn, k_nope)
             + jnp.einsum('bqhd,bkd->bhqk', qp, kpe)).astype(jnp.float32) * MLA_SCALE
        qpos = i * QB + jnp.arange(QB)
        mask = kpos[None, :] <= qpos[:, None]
        s = jnp.where(mask[None, None], s, jnp.finfo(jnp.float32).min)
        p = jax.nn.softmax(s, axis=-1)
        return jnp.einsum('bhqk,bkhd->bqhd', p.astype(v.dtype), v)

    o = lax.map(attend, (qn_b, qp_b, jnp.arange(nb)))
    o = o.swapaxes(0, 1).reshape(B, S, MLA_HEADS * V_HEAD)
    return o @ wo, ckv, kpe


def mla_sample(h, cache_ckv, cache_kpe, page_table, wdq, gq, wuq, wdkv, gkv, wuk, wuv, wo):
    B, T, _ = h.shape
    pos = PAST_LEN + jnp.arange(T)
    q_nope, q_pe, ckv, kpe = mla_project(h, pos, wdq, gq, wuq, wdkv, gkv)
    n_pages = page_table.shape[1]
    ckv_past = cache_ckv[page_table].reshape(B, n_pages * PAGE_SIZE, KV_RANK)
    kpe_past = cache_kpe[page_table].reshape(B, n_pages * PAGE_SIZE, QK_ROPE)
    q_lat = jnp.einsum('bthd,chd->bthc', q_nope, wuk)
    s_past = (jnp.einsum('bthc,bkc->bhtk', q_lat, ckv_past)
              + jnp.einsum('bthd,bkd->bhtk', q_pe, kpe_past)).astype(jnp.float32)
    s_new = (jnp.einsum('bthc,bkc->bhtk', q_lat, ckv)
             + jnp.einsum('bthd,bkd->bhtk', q_pe, kpe)).astype(jnp.float32)
    causal = jnp.arange(T)[None, :] <= jnp.arange(T)[:, None]
    s_new = jnp.where(causal[None, None], s_new, jnp.finfo(jnp.float32).min)
    s = jnp.concatenate([s_past, s_new], axis=-1) * MLA_SCALE
    p = jax.nn.softmax(s, axis=-1).astype(ckv.dtype)
    P = ckv_past.shape[1]
    o_lat = (jnp.einsum('bhtk,bkc->bthc', p[..., :P], ckv_past)
             + jnp.einsum('bhtk,bkc->bthc', p[..., P:], ckv))
    o = jnp.einsum('bthc,chd->bthd', o_lat, wuv).reshape(B, T, MLA_HEADS * V_HEAD)
    return o @ wo, ckv, kpe


def moe(x, wr, br, wgu, bgu, wd, bd):
    shp = x.shape
    x2 = x.reshape(-1, shp[-1])
    N = x2.shape[0]
    logits = (x2 @ wr + br).astype(jnp.float32)
    top_vals, top_idx = lax.top_k(logits, TOP_K)
    gates = jax.nn.softmax(top_vals, axis=-1)
    A = N * TOP_K
    flat_e = top_idx.reshape(A)
    flat_tok = jnp.arange(A, dtype=jnp.int32) // TOP_K
    flat_g = gates.reshape(A)
    order = jnp.argsort(flat_e)
    se, stok, sg = flat_e[order], flat_tok[order], flat_g[order]
    counts = jnp.bincount(flat_e, length=N_EXPERTS)
    padded = ((counts + MOE_BLOCK - 1) // MOE_BLOCK) * MOE_BLOCK
    ends = jnp.cumsum(padded)
    pad_start = ends - padded
    start = jnp.cumsum(counts) - counts
    dest = pad_start[se] + (jnp.arange(A) - start[se])
    n_blocks = -(-A // MOE_BLOCK) + N_EXPERTS
    R = n_blocks * MOE_BLOCK
    rows_tok = jnp.zeros((R,), jnp.int32).at[dest].set(stok)
    rows_gate = jnp.zeros((R,), jnp.float32).at[dest].set(sg)
    block_expert = jnp.minimum(
        jnp.searchsorted(ends, jnp.arange(n_blocks) * MOE_BLOCK, side='right'), N_EXPERTS - 1)
    xb = x2[rows_tok].reshape(n_blocks, MOE_BLOCK, shp[-1])

    def expert_block(args):
        xblk, e = args
        hgu = xblk @ wgu[e] + bgu[e]
        gate = jnp.minimum(hgu[:, ::2], SWIGLU_LIMIT)
        up = jnp.clip(hgu[:, 1::2], -SWIGLU_LIMIT, SWIGLU_LIMIT)
        act = (up + 1.0) * gate * jax.nn.sigmoid(gate * SWIGLU_ALPHA)
        return act @ wd[e] + bd[e]

    yb = lax.map(expert_block, (xb, block_expert)).reshape(R, shp[-1])
    y = jnp.zeros_like(x2).at[rows_tok].add(yb * rows_gate[:, None].astype(yb.dtype))
    return y.reshape(shp)


def setup_inputs(seed: int = 0) -> dict:
    key = jax.random.key(seed)
    ks = jax.random.split(key, 40)
    f32 = jnp.float32
    n_pages = PAST_LEN // PAGE_SIZE
    n_used = DEC_BATCH * n_pages
    n_pool = (n_used * POOL_NUM) // POOL_DEN

    def w(k, shape, fan_in, scale=1.0):
        return jax.random.normal(k, shape, f32) * (scale * fan_in ** -0.5)

    def gain(k, shape):
        return 1.0 + 0.02 * jax.random.normal(k, shape, f32)

    def bias(k, shape, s=0.02):
        return s * jax.random.normal(k, shape, f32)

    page_table = jax.random.permutation(ks[5], n_pool)[:n_used].reshape(DEC_BATCH, n_pages).astype(jnp.int32)
    return {
        "x_prompt": jax.random.normal(ks[0], (BATCH, SEQ, D_MODEL), f32),
        "x_sample": jax.random.normal(ks[1], (DEC_BATCH, DEC_SEQ, D_MODEL), f32),
        "state_ret": 0.3 * jax.random.normal(ks[2], (N_RET, DEC_BATCH, RET_HEADS, RET_DK, RET_DV), f32),
        "cache_ckv": jax.random.normal(ks[3], (N_MLA, n_pool, PAGE_SIZE, KV_RANK), f32),
        "cache_kpe": jax.random.normal(ks[4], (N_MLA, n_pool, PAGE_SIZE, QK_ROPE), f32),
        "page_table": page_table,
        "norm_mix": gain(ks[6], (DEPTH, D_MODEL)),
        "norm_ffn": gain(ks[7], (DEPTH, D_MODEL)),
        "norm_final": gain(ks[8], (D_MODEL,)),
        "ret_wq": w(ks[9], (N_RET, D_MODEL, RET_HEADS * RET_DK), D_MODEL),
        "ret_wk": w(ks[10], (N_RET, D_MODEL, RET_HEADS * RET_DK), D_MODEL),
        "ret_wv": w(ks[11], (N_RET, D_MODEL, RET_HEADS * RET_DV), D_MODEL),
        "ret_wg": w(ks[12], (N_RET, D_MODEL, RET_HEADS * RET_DV), D_MODEL),
        "ret_gn": gain(ks[13], (N_RET, RET_HEADS * RET_DV)),
        "ret_wo": w(ks[14], (N_RET, RET_HEADS * RET_DV, D_MODEL), RET_HEADS * RET_DV, 0.5),
        "mla_wdq": w(ks[15], (N_MLA, D_MODEL, Q_RANK), D_MODEL),
        "mla_gq": gain(ks[16], (N_MLA, Q_RANK)),
        "mla_wuq": w(ks[17], (N_MLA, Q_RANK, MLA_HEADS * (QK_NOPE + QK_ROPE)), Q_RANK),
        "mla_wdkv": w(ks[18], (N_MLA, D_MODEL, KV_RANK + QK_ROPE), D_MODEL),
        "mla_gkv": gain(ks[19], (N_MLA, KV_RANK)),
        "mla_wuk": w(ks[20], (N_MLA, KV_RANK, MLA_HEADS, QK_NOPE), KV_RANK),
        "mla_wuv": w(ks[21], (N_MLA, KV_RANK, MLA_HEADS, V_HEAD), KV_RANK),
        "mla_wo": w(ks[22], (N_MLA, MLA_HEADS * V_HEAD, D_MODEL), MLA_HEADS * V_HEAD, 0.5),
        "moe_wr": w(ks[23], (DEPTH, D_MODEL, N_EXPERTS), D_MODEL),
        "moe_br": bias(ks[24], (DEPTH, N_EXPERTS), 0.01),
        "moe_wgu": w(ks[25], (DEPTH, N_EXPERTS, D_MODEL, 2 * D_EXPERT), D_MODEL),
        "moe_bgu": bias(ks[26], (DEPTH, N_EXPERTS, 2 * D_EXPERT)),
        "moe_wd": w(ks[27], (DEPTH, N_EXPERTS, D_EXPERT, D_MODEL), D_EXPERT, 0.5),
        "moe_bd": bias(ks[28], (DEPTH, N_EXPERTS, D_MODEL)),
    }


def reference(x_prompt, x_sample, state_ret, cache_ckv, cache_kpe, page_table,
              norm_mix, norm_ffn, norm_final,
              ret_wq, ret_wk, ret_wv, ret_wg, ret_gn, ret_wo,
              mla_wdq, mla_gq, mla_wuq, mla_wdkv, mla_gkv, mla_wuk, mla_wuv, mla_wo,
              moe_wr, moe_br, moe_wgu, moe_bgu, moe_wd, moe_bd):
    yp, ys = x_prompt, x_sample
    ret_p, ret_s, ckv_p, kpe_p, ckv_s, kpe_s = [], [], [], [], [], []
    for layer in range(DEPTH):
        hp = rms_norm(yp, norm_mix[layer])
        hs = rms_norm(ys, norm_mix[layer])
        if layer % N_MIXERS == 0:
            r = layer // N_MIXERS
            op, sp = retention_prompt(hp, ret_wq[r], ret_wk[r], ret_wv[r], ret_wg[r], ret_gn[r], ret_wo[r])
            os_, ss = retention_sample(hs, state_ret[r], ret_wq[r], ret_wk[r], ret_wv[r], ret_wg[r],
                                       ret_gn[r], ret_wo[r])
            ret_p.append(sp)
            ret_s.append(ss)
        else:
            m = layer // N_MIXERS
            op, cp, kp = mla_prompt(hp, mla_wdq[m], mla_gq[m], mla_wuq[m], mla_wdkv[m], mla_gkv[m],
                                    mla_wuk[m], mla_wuv[m], mla_wo[m])
            os_, cs, ks_ = mla_sample(hs, cache_ckv[m], cache_kpe[m], page_table, mla_wdq[m], mla_gq[m],
                                      mla_wuq[m], mla_wdkv[m], mla_gkv[m], mla_wuk[m], mla_wuv[m], mla_wo[m])
            ckv_p.append(cp)
            kpe_p.append(kp)
            ckv_s.append(cs)
            kpe_s.append(ks_)
        yp = yp + op
        ys = ys + os_
        yp = yp + moe(rms_norm(yp, norm_ffn[layer]), moe_wr[layer], moe_br[layer], moe_wgu[layer],
                      moe_bgu[layer], moe_wd[layer], moe_bd[layer])
        ys = ys + moe(rms_norm(ys, norm_ffn[layer]), moe_wr[layer], moe_br[layer], moe_wgu[layer],
                      moe_bgu[layer], moe_wd[layer], moe_bd[layer])
    y_prompt = rms_norm(yp, norm_final)
    y_sample = rms_norm(ys, norm_final)
    return (y_prompt, y_sample, jnp.stack(ret_p), jnp.stack(ret_s), jnp.stack(ckv_p), jnp.stack(kpe_p),
            jnp.stack(ckv_s), jnp.stack(kpe_s))
```

```python
import functools
import math

import jax
import jax.numpy as jnp
from jax import lax
from jax.experimental import pallas as pl
from jax.experimental.pallas import tpu as pltpu

F32 = jnp.float32
BF16 = jnp.bfloat16

TOP_K = 4
SWIGLU_LIMIT = 7.0
SWIGLU_ALPHA = 1.702
NORM_EPS = 1e-6
GN_EPS = 1e-6
ROPE_BASE = 10000.0
LANES = 128
BF16_ROWS = 16

ROW_TILE = 512
RET_CHUNK = 256
MOE_BLOCK = 256
ATTN_TQ = 256
ATTN_TK = 256
VMEM_LIMIT = 56 * 1024 * 1024

_NT = (((1,), (1,)), ((), ()))


def _cparams(n_axes, vmem=VMEM_LIMIT):
    return pltpu.CompilerParams(dimension_semantics=("arbitrary",) * n_axes, vmem_limit_bytes=vmem)


def _rms(x, g):
    return x * lax.rsqrt(jnp.mean(x * x, axis=-1, keepdims=True) + NORM_EPS) * g


def _norm_matmul_kernel(x_ref, g_ref, w_ref, cos_ref, sin_ref, o_ref, h_ref, *, n_rope_tiles, head_dim):
    j = pl.program_id(1)

    @pl.when(j == 0)
    def _():
        h_ref[...] = _rms(x_ref[...], g_ref[...]).astype(BF16)

    acc = jnp.dot(h_ref[...], w_ref[...], preferred_element_type=F32)
    tn = acc.shape[1]

    if n_rope_tiles:
        @pl.when(j < n_rope_tiles)
        def _():
            cos = cos_ref[...]
            sin = sin_ref[...]
            half = head_dim // 2
            for h in range(tn // head_dim):
                x1 = acc[:, h * head_dim:h * head_dim + half]
                x2 = acc[:, h * head_dim + half:(h + 1) * head_dim]
                o_ref[:, h * head_dim:h * head_dim + half] = (x1 * cos - x2 * sin).astype(o_ref.dtype)
                o_ref[:, h * head_dim + half:(h + 1) * head_dim] = (x1 * sin + x2 * cos).astype(o_ref.dtype)

        @pl.when(j >= n_rope_tiles)
        def _():
            o_ref[...] = acc.astype(o_ref.dtype)
    else:
        o_ref[...] = acc.astype(o_ref.dtype)


def _norm_matmul(x, g, w, cos, sin, *, tn, n_rope_tiles, head_dim, out_dtype):
    n, d = x.shape
    nout = w.shape[1]
    tm = ROW_TILE
    return pl.pallas_call(
        functools.partial(_norm_matmul_kernel, n_rope_tiles=n_rope_tiles, head_dim=head_dim),
        grid=(n // tm, nout // tn),
        in_specs=[
            pl.BlockSpec((tm, d), lambda i, j: (i, 0)),
            pl.BlockSpec((1, d), lambda i, j: (0, 0)),
            pl.BlockSpec((d, tn), lambda i, j: (0, j)),
            pl.BlockSpec((tm, cos.shape[1]), lambda i, j: (i, 0)),
            pl.BlockSpec((tm, sin.shape[1]), lambda i, j: (i, 0)),
        ],
        out_specs=pl.BlockSpec((tm, tn), lambda i, j: (i, j)),
        out_shape=jax.ShapeDtypeStruct((n, nout), out_dtype),
        scratch_shapes=[pltpu.VMEM((tm, d), BF16)],
        compiler_params=_cparams(2),
    )(x, g.reshape(1, d), w, cos, sin)


def _group_norm_gate(o, g, gn):
    mu = jnp.mean(o, axis=-1, keepdims=True)
    oc = o - mu
    var = jnp.mean(oc * oc, axis=-1, keepdims=True)
    on = oc * lax.rsqrt(var + GN_EPS) * gn
    gf = g.astype(F32)
    return (gf / (1.0 + jnp.exp(-gf))) * on


def _retention_step(q, k, v, state, lg):
    L = q.shape[0]
    row = lax.broadcasted_iota(jnp.int32, (L, L), 0)
    col = lax.broadcasted_iota(jnp.int32, (L, L), 1)
    diff = (row - col).astype(F32)
    decay = jnp.where(diff >= 0.0, jnp.exp(jnp.maximum(diff, 0.0) * lg), 0.0)
    scores = lax.dot_general(q, k, _NT, preferred_element_type=F32) * decay
    inner = jnp.dot(scores.astype(BF16), v, preferred_element_type=F32)
    idx = lax.broadcasted_iota(jnp.int32, (L, 1), 0).astype(F32)
    q_decay = jnp.exp((idx + 1.0) * lg)
    cross = jnp.dot(q, state.astype(BF16), preferred_element_type=F32) * q_decay
    k_decay = jnp.exp((L - 1.0 - idx) * lg)
    kd = (k.astype(F32) * k_decay).T.astype(BF16)
    chunk_decay = jnp.exp(jnp.full((1, 1), L, F32) * lg)
    new_state = chunk_decay * state + jnp.dot(kd, v, preferred_element_type=F32)
    return inner + cross, new_state


def _ret_prompt_kernel(lg_ref, q_ref, k_ref, v_ref, g_ref, gn_ref, o_ref, s_ref):
    h = pl.program_id(1)
    c = pl.program_id(2)

    @pl.when(c == 0)
    def _():
        s_ref[...] = jnp.zeros_like(s_ref)

    o, new_state = _retention_step(q_ref[...], k_ref[...], v_ref[...], s_ref[0, 0], lg_ref[h])
    s_ref[0, 0] = new_state
    o_ref[...] = _group_norm_gate(o, g_ref[...], gn_ref[...]).astype(o_ref.dtype)


def _ret_prompt(qkvg, gn, lg, *, batch, seq, heads, dk, dv):
    L = min(RET_CHUNK, seq)
    nc = seq // L
    kq = heads * dk // dk
    v0 = 2 * heads * dk // dv
    g0 = v0 + heads
    grid_spec = pltpu.PrefetchScalarGridSpec(
        num_scalar_prefetch=1,
        grid=(batch, heads, nc),
        in_specs=[
            pl.BlockSpec((L, dk), lambda b, h, c, lg: (b * nc + c, h)),
            pl.BlockSpec((L, dk), lambda b, h, c, lg: (b * nc + c, kq + h)),
            pl.BlockSpec((L, dv), lambda b, h, c, lg: (b * nc + c, v0 + h)),
            pl.BlockSpec((L, dv), lambda b, h, c, lg: (b * nc + c, g0 + h)),
            pl.BlockSpec((1, dv), lambda b, h, c, lg: (0, h)),
        ],
        out_specs=[
            pl.BlockSpec((L, dv), lambda b, h, c, lg: (b * nc + c, h)),
            pl.BlockSpec((1, 1, dk, dv), lambda b, h, c, lg: (b, h, 0, 0)),
        ],
    )
    return pl.pallas_call(
        _ret_prompt_kernel,
        grid_spec=grid_spec,
        out_shape=[
            jax.ShapeDtypeStruct((batch * seq, heads * dv), BF16),
            jax.ShapeDtypeStruct((batch, heads, dk, dv), F32),
        ],
        compiler_params=_cparams(3),
    )(lg, qkvg, qkvg, qkvg, qkvg, gn.reshape(1, -1))


def _ret_sample_kernel(lg_ref, q_ref, k_ref, v_ref, g_ref, gn_ref, s_in_ref, o_ref, s_out_ref, *, t_len):
    h = pl.program_id(1)
    lg = lg_ref[h]
    nb = s_in_ref.shape[0]
    n = nb * t_len
    q = q_ref[...]
    k = k_ref[...]
    v = v_ref[...]
    g = g_ref[...]
    gn = gn_ref[...]
    row = lax.broadcasted_iota(jnp.int32, (n, n), 0)
    col = lax.broadcasted_iota(jnp.int32, (n, n), 1)
    diff = (row - col).astype(F32)
    keep = (row // t_len == col // t_len) & (row >= col)
    decay = jnp.where(keep, jnp.exp(jnp.maximum(diff, 0.0) * lg), 0.0)
    scores = lax.dot_general(q, k, _NT, preferred_element_type=F32) * decay
    inner = jnp.dot(scores.astype(BF16), v, preferred_element_type=F32)
    idx = (lax.broadcasted_iota(jnp.int32, (n, 1), 0) % t_len).astype(F32)
    q_decay = jnp.exp((idx + 1.0) * lg)
    kd = k.astype(F32) * jnp.exp((t_len - 1.0 - idx) * lg)
    chunk_decay = jnp.exp(jnp.full((1, 1), t_len, F32) * lg)
    group = BF16_ROWS // t_len
    seq_of_row = lax.broadcasted_iota(jnp.int32, (BF16_ROWS, 1), 0) // t_len
    for p in range(nb // group):
        rows = slice(p * BF16_ROWS, (p + 1) * BF16_ROWS)
        q16, kd16, v16 = q[rows], kd[rows], v[rows]
        cross = jnp.zeros((BF16_ROWS, v.shape[1]), F32)
        for j in range(group):
            b = p * group + j
            state = s_in_ref[b, 0]
            mine = seq_of_row == j
            cross = jnp.where(mine, jnp.dot(q16, state.astype(BF16), preferred_element_type=F32), cross)
            kdb = jnp.where(mine, kd16, 0.0).T.astype(BF16)
            s_out_ref[b, 0] = chunk_decay * state + jnp.dot(kdb, v16, preferred_element_type=F32)
        o = inner[rows] + cross * q_decay[rows]
        o_ref[rows, :] = _group_norm_gate(o, g[rows], gn).astype(o_ref.dtype)


def _ret_sample(qkvg, gn, lg, state, *, row0, t_len, heads, dk, dv):
    nbatch = state.shape[0]
    nb = 8
    rows = nb * t_len
    r0 = row0 // rows
    kq = heads
    v0 = 2 * heads * dk // dv
    g0 = v0 + heads
    grid_spec = pltpu.PrefetchScalarGridSpec(
        num_scalar_prefetch=1,
        grid=(nbatch // nb, heads),
        in_specs=[
            pl.BlockSpec((rows, dk), lambda i, h, lg: (r0 + i, h)),
            pl.BlockSpec((rows, dk), lambda i, h, lg: (r0 + i, kq + h)),
            pl.BlockSpec((rows, dv), lambda i, h, lg: (r0 + i, v0 + h)),
            pl.BlockSpec((rows, dv), lambda i, h, lg: (r0 + i, g0 + h)),
            pl.BlockSpec((1, dv), lambda i, h, lg: (0, h)),
            pl.BlockSpec((nb, 1, dk, dv), lambda i, h, lg: (i, h, 0, 0)),
        ],
        out_specs=[
            pl.BlockSpec((rows, dv), lambda i, h, lg: (i, h)),
            pl.BlockSpec((nb, 1, dk, dv), lambda i, h, lg: (i, h, 0, 0)),
        ],
    )
    return pl.pallas_call(
        functools.partial(_ret_sample_kernel, t_len=t_len),
        grid_spec=grid_spec,
        out_shape=[
            jax.ShapeDtypeStruct((nbatch * t_len, heads * dv), BF16),
            jax.ShapeDtypeStruct(state.shape, F32),
        ],
        compiler_params=_cparams(2),
    )(lg, qkvg, qkvg, qkvg, qkvg, gn.reshape(1, -1), state)


def _matmul_res_kernel(x_ref, w_ref, r_ref, o_ref):
    o_ref[...] = r_ref[...] + jnp.dot(x_ref[...], w_ref[...], preferred_element_type=F32)


def _matmul_res(x, w, res):
    n, k = x.shape
    d = w.shape[1]
    tm = ROW_TILE
    return pl.pallas_call(
        _matmul_res_kernel,
        grid=(n // tm,),
        in_specs=[
            pl.BlockSpec((tm, k), lambda i: (i, 0)),
            pl.BlockSpec((k, d), lambda i: (0, 0)),
            pl.BlockSpec((tm, d), lambda i: (i, 0)),
        ],
        out_specs=pl.BlockSpec((tm, d), lambda i: (i, 0)),
        out_shape=jax.ShapeDtypeStruct((n, d), F32),
        compiler_params=_cparams(1),
    )(x, w, res)


def _router_kernel(x_ref, g_ref, wh_ref, wl_ref, b_ref, xn_ref, idx_ref, gate_ref, *, n_experts):
    xn = _rms(x_ref[...], g_ref[...])
    hi = xn.astype(BF16)
    lo = (xn - hi.astype(F32)).astype(BF16)
    xn_ref[...] = hi
    logits = (jnp.dot(hi, wh_ref[...], preferred_element_type=F32)
              + jnp.dot(lo, wh_ref[...], preferred_element_type=F32)
              + jnp.dot(hi, wl_ref[...], preferred_element_type=F32)) + b_ref[...]
    lane = lax.broadcasted_iota(jnp.int32, logits.shape, 1).astype(F32)
    neg = jnp.float32(-jnp.inf)
    work = jnp.where(lane < n_experts, logits, neg)
    vals, idxs = [], []
    for _ in range(TOP_K):
        m = jnp.max(work, axis=-1, keepdims=True)
        sel = jnp.min(jnp.where(work == m, lane, float(LANES)), axis=-1, keepdims=True)
        vals.append(m)
        idxs.append(sel)
        work = jnp.where(lane == sel, neg, work)
    es = [jnp.exp(v - vals[0]) for v in vals]
    denom = es[0]
    for e in es[1:]:
        denom = denom + e
    idx_out = jnp.zeros(logits.shape, F32)
    gate_out = jnp.zeros(logits.shape, F32)
    for kk in range(TOP_K):
        idx_out = jnp.where(lane == kk, idxs[kk], idx_out)
        gate_out = jnp.where(lane == kk, es[kk] / denom, gate_out)
    idx_ref[...] = idx_out.astype(jnp.int32)
    gate_ref[...] = gate_out


def _router(y, g, wr, br):
    n, d = y.shape
    e = wr.shape[1]
    tm = ROW_TILE
    wr_pad = jnp.zeros((d, LANES), F32).at[:, :e].set(wr)
    wh = wr_pad.astype(BF16)
    wl = (wr_pad - wh.astype(F32)).astype(BF16)
    b_pad = jnp.zeros((1, LANES), F32).at[0, :e].set(br)
    return pl.pallas_call(
        functools.partial(_router_kernel, n_experts=e),
        grid=(n // tm,),
        in_specs=[
            pl.BlockSpec((tm, d), lambda i: (i, 0)),
            pl.BlockSpec((1, d), lambda i: (0, 0)),
            pl.BlockSpec((d, LANES), lambda i: (0, 0)),
            pl.BlockSpec((d, LANES), lambda i: (0, 0)),
            pl.BlockSpec((1, LANES), lambda i: (0, 0)),
        ],
        out_specs=[
            pl.BlockSpec((tm, d), lambda i: (i, 0)),
            pl.BlockSpec((tm, LANES), lambda i: (i, 0)),
            pl.BlockSpec((tm, LANES), lambda i: (i, 0)),
        ],
        out_shape=[
            jax.ShapeDtypeStruct((n, d), BF16),
            jax.ShapeDtypeStruct((n, LANES), jnp.int32),
            jax.ShapeDtypeStruct((n, LANES), F32),
        ],
        compiler_params=_cparams(1),
    )(y, g.reshape(1, d), wh, wl, b_pad)


def _expert_kernel(be_ref, nv_ref, x_ref, wgu_ref, bgu_ref, wd_ref, bd_ref, gate_ref, o_ref, *, d_expert):
    i = pl.program_id(0)

    @pl.when(i < nv_ref[0])
    def _():
        hgu = jnp.dot(x_ref[...], wgu_ref[0], preferred_element_type=F32) + bgu_ref[0]
        gate = jnp.minimum(hgu[:, :d_expert], SWIGLU_LIMIT)
        up = jnp.clip(hgu[:, d_expert:], -SWIGLU_LIMIT, SWIGLU_LIMIT)
        act = (up + 1.0) * gate * (1.0 / (1.0 + jnp.exp(-(gate * SWIGLU_ALPHA))))
        y = jnp.dot(act.astype(BF16), wd_ref[0], preferred_element_type=F32) + bd_ref[0]
        o_ref[...] = y * gate_ref[...]

    @pl.when(i >= nv_ref[0])
    def _():
        o_ref[...] = jnp.zeros_like(o_ref)


def _experts(xb, rows_gate, block_expert, n_valid, wgu, bgu, wd, bd):
    r, d = xb.shape
    de = wd.shape[1]
    blk = MOE_BLOCK
    grid_spec = pltpu.PrefetchScalarGridSpec(
        num_scalar_prefetch=2,
        grid=(r // blk,),
        in_specs=[
            pl.BlockSpec((blk, d), lambda i, be, nv: (i, 0)),
            pl.BlockSpec((1, d, 2 * de), lambda i, be, nv: (be[i], 0, 0)),
            pl.BlockSpec((1, 1, 2 * de), lambda i, be, nv: (be[i], 0, 0)),
            pl.BlockSpec((1, de, d), lambda i, be, nv: (be[i], 0, 0)),
            pl.BlockSpec((1, 1, d), lambda i, be, nv: (be[i], 0, 0)),
            pl.BlockSpec((blk, 1), lambda i, be, nv: (i, 0)),
        ],
        out_specs=pl.BlockSpec((blk, d), lambda i, be, nv: (i, 0)),
    )
    return pl.pallas_call(
        functools.partial(_expert_kernel, d_expert=de),
        grid_spec=grid_spec,
        out_shape=jax.ShapeDtypeStruct((r, d), F32),
        compiler_params=_cparams(1),
    )(block_expert, n_valid, xb, wgu, bgu, wd, bd, rows_gate)


def _moe(y, g, wr, br, wgu, bgu, wd, bd):
    n, d = y.shape
    e = wr.shape[1]
    de = wd.shape[1]
    xn, idx_pad, gate_pad = _router(y, g, wr, br)
    top_idx = idx_pad[:, :TOP_K]
    gates = gate_pad[:, :TOP_K]

    a = n * TOP_K
    blk = MOE_BLOCK
    flat_e = top_idx.reshape(a)
    onehot = (flat_e[:, None] == jnp.arange(e, dtype=jnp.int32)[None, :]).astype(jnp.int32)
    csum = jnp.cumsum(onehot, axis=0)
    counts = csum[-1]
    rank = jnp.take_along_axis(csum, flat_e[:, None], axis=1)[:, 0] - 1
    padded = ((counts + blk - 1) // blk) * blk
    ends = jnp.cumsum(padded)
    dest = (ends - padded)[flat_e] + rank
    n_blocks = -(-a // blk) + e
    r = n_blocks * blk
    flat_tok = jnp.arange(a, dtype=jnp.int32) // TOP_K
    rows_tok = jnp.zeros((r,), jnp.int32).at[dest].set(flat_tok)
    rows_gate = jnp.zeros((r,), F32).at[dest].set(gates.reshape(a))
    block_expert = jnp.minimum(
        jnp.searchsorted(ends, jnp.arange(n_blocks, dtype=jnp.int32) * blk, side='right'), e - 1).astype(jnp.int32)
    n_valid = (ends[-1:] // blk).astype(jnp.int32)

    wgu_b = jnp.concatenate([wgu[:, :, 0::2], wgu[:, :, 1::2]], axis=-1).astype(BF16)
    bgu_b = jnp.concatenate([bgu[:, 0::2], bgu[:, 1::2]], axis=-1).reshape(e, 1, 2 * de)
    wd_b = wd.astype(BF16)
    bd_b = bd.reshape(e, 1, d)

    xb = jnp.take(xn, rows_tok, axis=0)
    yb = _experts(xb, rows_gate.reshape(r, 1), block_expert, n_valid, wgu_b, bgu_b, wd_b, bd_b)
    contrib = jnp.take(yb, dest, axis=0).reshape(n, TOP_K, d)
    return y + jnp.sum(contrib, axis=1)


def _rope_lanes(x, c, sa, sb, half):
    n = x.shape[1]
    return x * c + pltpu.roll(x, n - half, 1) * sa + pltpu.roll(x, half, 1) * sb


def _mla_proj_kernel(dqkv_ref, gq_ref, gkv_ref, wuq_ref, wuk_ref, wuv_ref, c_ref, sa_ref, sb_ref,
                     q_ref, k_ref, v_ref, ckv_ref, kpe_ref, *, q_rank, kv_rank, heads, rope_half, scale):
    x = dqkv_ref[...]
    c = c_ref[...]
    sa = sa_ref[...]
    sb = sb_ref[...]
    cq = _rms(x[:, :q_rank], gq_ref[...]).astype(BF16)
    ckv = _rms(x[:, q_rank:q_rank + kv_rank], gkv_ref[...])
    ckv_ref[...] = ckv
    kpe = _rope_lanes(x[:, q_rank + kv_rank:], c, sa, sb, rope_half)
    kpe_ref[...] = kpe
    ckv_b = ckv.astype(BF16)
    q = jnp.dot(cq, wuq_ref[...], preferred_element_type=F32)
    k = jnp.dot(ckv_b, wuk_ref[...], preferred_element_type=F32)
    for h in range(heads):
        cols = slice(h * LANES, (h + 1) * LANES)
        q_ref[:, cols] = (_rope_lanes(q[:, cols], c, sa, sb, rope_half) * scale).astype(BF16)
        k_ref[:, cols] = (k[:, cols] + kpe).astype(BF16)
    v_ref[...] = jnp.dot(ckv_b, wuv_ref[...], preferred_element_type=F32).astype(BF16)


def _mla_proj(dqkv, gq, gkv, wuq_p, wuk_p, wuv_p, c, sa, sb, *, q_rank, kv_rank, heads, rope_half, scale):
    n, w = dqkv.shape
    tm = ROW_TILE
    hv = wuv_p.shape[1]
    full = lambda shape: pl.BlockSpec(shape, lambda i: (0,) * len(shape))
    rows = lambda width: pl.BlockSpec((tm, width), lambda i: (i, 0))
    return pl.pallas_call(
        functools.partial(_mla_proj_kernel, q_rank=q_rank, kv_rank=kv_rank, heads=heads, rope_half=rope_half,
                          scale=scale),
        grid=(n // tm,),
        in_specs=[rows(w), full((1, q_rank)), full((1, kv_rank)), full(wuq_p.shape), full(wuk_p.shape),
                  full(wuv_p.shape), rows(LANES), rows(LANES), rows(LANES)],
        out_specs=[rows(heads * LANES), rows(heads * LANES), rows(hv), rows(kv_rank), rows(LANES)],
        out_shape=[
            jax.ShapeDtypeStruct((n, heads * LANES), BF16),
            jax.ShapeDtypeStruct((n, heads * LANES), BF16),
            jax.ShapeDtypeStruct((n, hv), BF16),
            jax.ShapeDtypeStruct((n, kv_rank), F32),
            jax.ShapeDtypeStruct((n, LANES), F32),
        ],
        compiler_params=_cparams(1),
    )(dqkv, gq.reshape(1, -1), gkv.reshape(1, -1), wuq_p, wuk_p, wuv_p, c, sa, sb)


def _flash_kernel(q_ref, k_ref, v_ref, o_ref, *, tq, tk, v_head):
    qi = pl.program_id(2)
    qs = [q_ref[:, :LANES], q_ref[:, LANES:]]

    def tile(ki, carry, masked):
        k = k_ref[pl.ds(pl.multiple_of(ki * tk, tk), tk), :]
        v = v_ref[pl.ds(pl.multiple_of(ki * tk, tk), tk), :]
        out = []
        for h in range(2):
            m, l, acc = carry[h]
            s = lax.dot_general(qs[h], k[:, h * LANES:(h + 1) * LANES], _NT, preferred_element_type=F32)
            if masked:
                row = lax.broadcasted_iota(jnp.int32, s.shape, 0)
                col = lax.broadcasted_iota(jnp.int32, s.shape, 1)
                s = jnp.where(col <= row, s, jnp.finfo(F32).min)
            m_new = jnp.maximum(m, jnp.max(s, axis=-1, keepdims=True))
            alpha = jnp.exp(m - m_new)
            p = jnp.exp(s - m_new)
            l = alpha * l + jnp.sum(p, axis=-1, keepdims=True)
            acc = alpha * acc + jnp.dot(p.astype(BF16), v, preferred_element_type=F32)
            out.append((m_new, l, acc))
        return tuple(out)

    init = tuple((jnp.full((tq, 1), -jnp.inf, F32), jnp.zeros((tq, 1), F32), jnp.zeros((tq, LANES), F32))
                 for _ in range(2))
    carry = lax.fori_loop(0, qi, lambda ki, cr: tile(ki, cr, False), init)
    carry = tile(qi, carry, True)
    lane = lax.broadcasted_iota(jnp.int32, (tq, LANES), 1)
    o0 = carry[0][2] / carry[0][1]
    o1 = carry[1][2] / carry[1][1]
    o_ref[...] = jnp.where(lane < v_head, o0, o1).astype(o_ref.dtype)


def _flash(q_cat, k_cat, v, *, batch, seq, heads, v_head):
    tq = min(ATTN_TQ, seq)
    tk = tq
    nq = seq // tq
    return pl.pallas_call(
        functools.partial(_flash_kernel, tq=tq, tk=tk, v_head=v_head),
        grid=(batch, heads // 2, nq),
        in_specs=[
            pl.BlockSpec((tq, 2 * LANES), lambda b, hp, qi: (b * nq + qi, hp)),
            pl.BlockSpec((seq, 2 * LANES), lambda b, hp, qi: (b, hp)),
            pl.BlockSpec((seq, 2 * v_head), lambda b, hp, qi: (b, hp)),
        ],
        out_specs=pl.BlockSpec((tq, 2 * v_head), lambda b, hp, qi: (b * nq + qi, hp)),
        out_shape=jax.ShapeDtypeStruct((batch * seq, heads * v_head), BF16),
        compiler_params=_cparams(3),
    )(q_cat, k_cat, v)


def _qlat_kernel(q_ref, w_ref, o_ref, *, heads, kv_rank):
    for h in range(heads):
        o_ref[:, h * kv_rank:(h + 1) * kv_rank] = jnp.dot(
            q_ref[:, h * LANES:(h + 1) * LANES], w_ref[h], preferred_element_type=F32).astype(o_ref.dtype)


def _qlat(q_cat, wukt_p, *, row0, nrows, heads, kv_rank):
    tm = min(ROW_TILE, nrows)
    r0 = row0 // tm
    return pl.pallas_call(
        functools.partial(_qlat_kernel, heads=heads, kv_rank=kv_rank),
        grid=(nrows // tm,),
        in_specs=[
            pl.BlockSpec((tm, heads * LANES), lambda i: (r0 + i, 0)),
            pl.BlockSpec(wukt_p.shape, lambda i: (0, 0, 0)),
        ],
        out_specs=pl.BlockSpec((tm, heads * kv_rank), lambda i: (i, 0)),
        out_shape=jax.ShapeDtypeStruct((nrows, heads * kv_rank), BF16),
        compiler_params=_cparams(1),
    )(q_cat, wukt_p)


def _olat_kernel(x_ref, w_ref, o_ref, *, pairs, kv_rank):
    for p in range(pairs):
        o_ref[:, p * LANES:(p + 1) * LANES] = jnp.dot(
            x_ref[:, p * 2 * kv_rank:(p + 1) * 2 * kv_rank], w_ref[p], preferred_element_type=F32).astype(o_ref.dtype)


def _olat(o_lat, wuv_pair, *, heads, kv_rank):
    n = o_lat.shape[0]
    tm = min(ROW_TILE, n)
    return pl.pallas_call(
        functools.partial(_olat_kernel, pairs=heads // 2, kv_rank=kv_rank),
        grid=(n // tm,),
        in_specs=[
            pl.BlockSpec((tm, heads * kv_rank), lambda i: (i, 0)),
            pl.BlockSpec(wuv_pair.shape, lambda i: (0, 0, 0)),
        ],
        out_specs=pl.BlockSpec((tm, (heads // 2) * LANES), lambda i: (i, 0)),
        out_shape=jax.ShapeDtypeStruct((n, (heads // 2) * LANES), BF16),
        compiler_params=_cparams(1),
    )(o_lat, wuv_pair)


def _paged_copies(pt_ref, cache_ckv, cache_kpe, ckv_buf, kpe_buf, sem, b, slot, p, page):
    pg = pt_ref[b, p]
    return (
        pltpu.make_async_copy(cache_ckv.at[pg], ckv_buf.at[slot, pl.ds(p * page, page)], sem.at[0, slot]),
        pltpu.make_async_copy(cache_kpe.at[pg], kpe_buf.at[slot, pl.ds(p * page, page)], sem.at[1, slot]),
    )


def _paged_attn_kernel(pt_ref, qlat_ref, qpe_ref, cnew_ref, pnew_ref, cache_ckv, cache_kpe, o_ref,
                       ckv_buf, kpe_buf, sem, *, n_pages, page, heads):
    b = pl.program_id(0)
    nb = pl.num_programs(0)
    slot = b % 2

    def start_all(bb, sl):
        def body(p, _):
            for cp in _paged_copies(pt_ref, cache_ckv, cache_kpe, ckv_buf, kpe_buf, sem, bb, sl, p, page):
                cp.start()
            return 0
        lax.fori_loop(0, n_pages, body, 0)

    @pl.when(b == 0)
    def _():
        start_all(0, 0)

    @pl.when(b + 1 < nb)
    def _():
        start_all(b + 1, 1 - slot)

    def wait_body(p, _):
        for cp in _paged_copies(pt_ref, cache_ckv, cache_kpe, ckv_buf, kpe_buf, sem, b, slot, p, page):
            cp.wait()
        return 0
    lax.fori_loop(0, n_pages, wait_body, 0)

    qlat = qlat_ref[0]
    qpe = qpe_ref[0]
    ck = ckv_buf[slot].astype(BF16)
    kp = kpe_buf[slot].astype(BF16)
    s_past = (lax.dot_general(qlat, ck, _NT, preferred_element_type=F32)
              + lax.dot_general(qpe, kp, _NT, preferred_element_type=F32))
    cn = cnew_ref[0].astype(BF16)
    pn = pnew_ref[0].astype(BF16)
    s_new = (lax.dot_general(qlat, cn, _NT, preferred_element_type=F32)
             + lax.dot_general(qpe, pn, _NT, preferred_element_type=F32))
    t_of_row = lax.broadcasted_iota(jnp.int32, s_new.shape, 0) // heads
    j = lax.broadcasted_iota(jnp.int32, s_new.shape, 1)
    s_new = jnp.where(j <= t_of_row, s_new, jnp.finfo(F32).min)
    m = jnp.maximum(jnp.max(s_past, axis=-1, keepdims=True), jnp.max(s_new, axis=-1, keepdims=True))
    p_past = jnp.exp(s_past - m)
    p_new = jnp.exp(s_new - m)
    denom = jnp.sum(p_past, axis=-1, keepdims=True) + jnp.sum(p_new, axis=-1, keepdims=True)
    o = (jnp.dot(p_past.astype(BF16), ck, preferred_element_type=F32)
         + jnp.dot(p_new.astype(BF16), cn, preferred_element_type=F32))
    o_ref[0] = (o / denom).astype(o_ref.dtype)


def _paged_attn(page_table, qlat, qpe, ckv_new, kpe_new, cache_ckv, cache_kpe, *, heads):
    nbatch, n_pages = page_table.shape
    page, kv_rank = cache_ckv.shape[1], cache_ckv.shape[2]
    rope = cache_kpe.shape[2]
    rows = qlat.shape[1]
    tpad = ckv_new.shape[1]
    grid_spec = pltpu.PrefetchScalarGridSpec(
        num_scalar_prefetch=1,
        grid=(nbatch,),
        in_specs=[
            pl.BlockSpec((1, rows, kv_rank), lambda b, pt: (b, 0, 0)),
            pl.BlockSpec((1, rows, rope), lambda b, pt: (b, 0, 0)),
            pl.BlockSpec((1, tpad, kv_rank), lambda b, pt: (b, 0, 0)),
            pl.BlockSpec((1, tpad, rope), lambda b, pt: (b, 0, 0)),
            pl.BlockSpec(memory_space=pl.ANY),
            pl.BlockSpec(memory_space=pl.ANY),
        ],
        out_specs=pl.BlockSpec((1, rows, kv_rank), lambda b, pt: (b, 0, 0)),
        scratch_shapes=[
            pltpu.VMEM((2, n_pages * page, kv_rank), F32),
            pltpu.VMEM((2, n_pages * page, rope), F32),
            pltpu.SemaphoreType.DMA((2, 2)),
        ],
    )
    return pl.pallas_call(
        functools.partial(_paged_attn_kernel, n_pages=n_pages, page=page, heads=heads),
        grid_spec=grid_spec,
        out_shape=jax.ShapeDtypeStruct((nbatch, rows, kv_rank), BF16),
        compiler_params=_cparams(1),
    )(page_table, qlat, qpe, ckv_new, kpe_new, cache_ckv, cache_kpe)


def _rmsnorm_kernel(x_ref, g_ref, o_ref):
    o_ref[...] = _rms(x_ref[...], g_ref[...])


def _rmsnorm(x, g):
    n, d = x.shape
    tm = ROW_TILE
    return pl.pallas_call(
        _rmsnorm_kernel,
        grid=(n // tm,),
        in_specs=[pl.BlockSpec((tm, d), lambda i: (i, 0)), pl.BlockSpec((1, d), lambda i: (0, 0))],
        out_specs=pl.BlockSpec((tm, d), lambda i: (i, 0)),
        out_shape=jax.ShapeDtypeStruct((n, d), F32),
        compiler_params=_cparams(1),
    )(x, g.reshape(1, d))


def _rope_angles(pos, half):
    inv = ROPE_BASE ** (-jnp.arange(half, dtype=F32) / half)
    return pos.astype(F32)[:, None] * inv[None, :]


def _token_positions(batch, seq, dec_batch, dec_seq, past_len):
    return jnp.concatenate([jnp.tile(jnp.arange(seq), batch), jnp.tile(past_len + jnp.arange(dec_seq), dec_batch)])


def kernel(x_prompt, x_sample, state_ret, cache_ckv, cache_kpe, page_table, norm_mix, norm_ffn, norm_final, ret_wq, ret_wk, ret_wv, ret_wg, ret_gn, ret_wo, mla_wdq, mla_gq, mla_wuq, mla_wdkv, mla_gkv, mla_wuk, mla_wuv, mla_wo, moe_wr, moe_br, moe_wgu, moe_bgu, moe_wd, moe_bd):
    batch, seq, d = x_prompt.shape
    dec_batch, dec_seq, _ = x_sample.shape
    ret_heads, dk, dv = state_ret.shape[2], state_ret.shape[3], state_ret.shape[4]
    page = cache_ckv.shape[2]
    kv_rank = cache_ckv.shape[3]
    rope_dim = cache_kpe.shape[3]
    past_len = page_table.shape[1] * page
    mla_heads, nope = mla_wuk.shape[2], mla_wuk.shape[3]
    v_head = mla_wuv.shape[3]
    q_rank = mla_wdq.shape[2]
    n_prompt = batch * seq
    n_sample = dec_batch * dec_seq
    assert nope + rope_dim <= LANES and 2 * v_head == LANES and dk == 2 * LANES

    y = jnp.concatenate([x_prompt.reshape(n_prompt, d), x_sample.reshape(n_sample, d)], axis=0)
    pos = _token_positions(batch, seq, dec_batch, dec_seq, past_len)

    ang = _rope_angles(pos, dk // 2)
    w_all = jnp.concatenate([ret_wq[0], ret_wk[0] * (dk ** -0.5), ret_wv[0], ret_wg[0]], axis=1).astype(BF16)
    qkvg = _norm_matmul(y, norm_mix[0], w_all, jnp.cos(ang), jnp.sin(ang), tn=ret_heads * dk,
                        n_rope_tiles=2, head_dim=dk, out_dtype=BF16)
    lg = jnp.log(1.0 - 2.0 ** (-5.0 - jnp.arange(ret_heads, dtype=F32)))
    gated_p, state_p = _ret_prompt(qkvg, ret_gn[0], lg, batch=batch, seq=seq, heads=ret_heads, dk=dk, dv=dv)
    gated_s, state_s = _ret_sample(qkvg, ret_gn[0], lg, state_ret[0], row0=n_prompt, t_len=dec_seq,
                                   heads=ret_heads, dk=dk, dv=dv)
    gated = jnp.concatenate([gated_p, gated_s], axis=0)
    y = _matmul_res(gated, ret_wo[0].astype(BF16), y)
    y = _moe(y, norm_ffn[0], moe_wr[0], moe_br[0], moe_wgu[0], moe_bgu[0], moe_wd[0], moe_bd[0])

    half = rope_dim // 2
    ang = _rope_angles(pos, half)
    cos2 = jnp.concatenate([jnp.cos(ang), jnp.cos(ang)], axis=1)
    sin = jnp.sin(ang)
    n_all = y.shape[0]
    ones = jnp.ones((n_all, LANES), F32)
    zeros = jnp.zeros((n_all, LANES), F32)
    c_tab = lax.dynamic_update_slice(ones, cos2, (0, nope))
    sa_tab = lax.dynamic_update_slice(zeros, -sin, (0, nope))
    sb_tab = lax.dynamic_update_slice(zeros, sin, (0, nope + half))

    wdkv = mla_wdkv[0]
    kpe_cols = jnp.zeros((d, LANES), F32).at[:, nope:nope + rope_dim].set(wdkv[:, kv_rank:])
    w_down = jnp.concatenate([mla_wdq[0], wdkv[:, :kv_rank], kpe_cols], axis=1).astype(BF16)
    dqkv = _norm_matmul(y, norm_mix[1], w_down, zeros[:, :LANES], zeros[:, :LANES], tn=w_down.shape[1],
                        n_rope_tiles=0, head_dim=0, out_dtype=F32)

    qk_dim = nope + rope_dim
    wuq_p = jnp.zeros((q_rank, mla_heads, LANES), F32).at[:, :, :qk_dim].set(
        mla_wuq[0].reshape(q_rank, mla_heads, qk_dim)).reshape(q_rank, mla_heads * LANES).astype(BF16)
    wuk_p = jnp.zeros((kv_rank, mla_heads, LANES), F32).at[:, :, :nope].set(mla_wuk[0]).reshape(
        kv_rank, mla_heads * LANES).astype(BF16)
    wuv_p = mla_wuv[0].reshape(kv_rank, mla_heads * v_head).astype(BF16)
    scale = float(qk_dim) ** -0.5
    q_cat, k_cat, v_all, ckv_all, kpe_all = _mla_proj(
        dqkv, mla_gq[0], mla_gkv[0], wuq_p, wuk_p, wuv_p, c_tab, sa_tab, sb_tab,
        q_rank=q_rank, kv_rank=kv_rank, heads=mla_heads, rope_half=half, scale=scale)
    kpe_rows = kpe_all[:, nope:nope + rope_dim]

    o_p = _flash(q_cat, k_cat, v_all, batch=batch, seq=seq, heads=mla_heads, v_head=v_head)

    wukt_p = jnp.zeros((mla_heads, LANES, kv_rank), F32).at[:, :nope, :].set(
        jnp.transpose(mla_wuk[0], (1, 2, 0))).astype(BF16)
    q_lat = _qlat(q_cat, wukt_p, row0=n_prompt, nrows=n_sample, heads=mla_heads, kv_rank=kv_rank)
    rows = dec_seq * mla_heads
    q_pe = q_cat[n_prompt:].reshape(n_sample, mla_heads, LANES)[:, :, nope:qk_dim].reshape(dec_batch, rows, rope_dim)
    tpad = 16
    ckv_new = jnp.zeros((dec_batch, tpad, kv_rank), F32).at[:, :dec_seq].set(
        ckv_all[n_prompt:].reshape(dec_batch, dec_seq, kv_rank))
    kpe_new = jnp.zeros((dec_batch, tpad, rope_dim), F32).at[:, :dec_seq].set(
        kpe_rows[n_prompt:].reshape(dec_batch, dec_seq, rope_dim))
    o_lat = _paged_attn(page_table, q_lat.reshape(dec_batch, rows, kv_rank), q_pe, ckv_new, kpe_new,
                        cache_ckv[0], cache_kpe[0], heads=mla_heads)
    wuv_h = jnp.transpose(mla_wuv[0], (1, 0, 2)).reshape(mla_heads // 2, 2, kv_rank, v_head)
    wuv_pair = jnp.zeros((mla_heads // 2, 2, kv_rank, 2, v_head), F32)
    wuv_pair = wuv_pair.at[:, 0, :, 0, :].set(wuv_h[:, 0]).at[:, 1, :, 1, :].set(wuv_h[:, 1])
    wuv_pair = wuv_pair.reshape(mla_heads // 2, 2 * kv_rank, LANES).astype(BF16)
    o_s = _olat(o_lat.reshape(n_sample, mla_heads * kv_rank), wuv_pair, heads=mla_heads, kv_rank=kv_rank)

    o_all = jnp.concatenate([o_p, o_s], axis=0)
    y = _matmul_res(o_all, mla_wo[0].astype(BF16), y)
    y = _moe(y, norm_ffn[1], moe_wr[1], moe_br[1], moe_wgu[1], moe_bgu[1], moe_wd[1], moe_bd[1])

    y = _rmsnorm(y, norm_final)
    return (
        y[:n_prompt].reshape(batch, seq, d),
        y[n_prompt:].reshape(dec_batch, dec_seq, d),
        state_p[None],
        state_s[None],
        ckv_all[:n_prompt].reshape(1, batch, seq, kv_rank),
        kpe_rows[:n_prompt].reshape(1, batch, seq, rope_dim),
        ckv_all[n_prompt:].reshape(1, dec_batch, dec_seq, kv_rank),
        kpe_rows[n_prompt:].reshape(1, dec_batch, dec_seq, rope_dim),
    )
```

```python
import functools
import math

import jax
import jax.numpy as jnp
from jax import lax
from jax.experimental import pallas as pl
from jax.experimental.pallas import tpu as pltpu

F32 = jnp.float32
BF16 = jnp.bfloat16

TOP_K = 4
SWIGLU_LIMIT = 7.0
SWIGLU_ALPHA = 1.702
NORM_EPS = 1e-6
GN_EPS = 1e-6
ROPE_BASE = 10000.0
LANES = 128
BF16_ROWS = 16
MXU_DIM = 256

ROW_TILE = 512
RET_CHUNK = 256
MOE_BLOCK = 256
COMBINE_TILE = 128
ATTN_TILE = 512
ATTN_HEADS = 4
DMA_UNROLL = 8
VMEM_LIMIT = 56 * 1024 * 1024

_NT = (((1,), (1,)), ((), ()))


def _cparams(n_axes, vmem=VMEM_LIMIT):
    return pltpu.CompilerParams(dimension_semantics=("arbitrary",) * n_axes, vmem_limit_bytes=vmem)


def _rms(x, g):
    return x * lax.rsqrt(jnp.mean(x * x, axis=-1, keepdims=True) + NORM_EPS) * g


def _norm_matmul_kernel(x_ref, g_ref, w_ref, cos_ref, sin_ref, o_ref, h_ref, *, n_rope_tiles, head_dim):
    j = pl.program_id(1)

    @pl.when(j == 0)
    def _():
        h_ref[...] = _rms(x_ref[...], g_ref[...]).astype(BF16)

    acc = jnp.dot(h_ref[...], w_ref[...], preferred_element_type=F32)
    tn = acc.shape[1]

    if n_rope_tiles:
        @pl.when(j < n_rope_tiles)
        def _():
            cos = cos_ref[...]
            sin = sin_ref[...]
            half = head_dim // 2
            for h in range(tn // head_dim):
                x1 = acc[:, h * head_dim:h * head_dim + half]
                x2 = acc[:, h * head_dim + half:(h + 1) * head_dim]
                o_ref[:, h * head_dim:h * head_dim + half] = (x1 * cos - x2 * sin).astype(o_ref.dtype)
                o_ref[:, h * head_dim + half:(h + 1) * head_dim] = (x1 * sin + x2 * cos).astype(o_ref.dtype)

        @pl.when(j >= n_rope_tiles)
        def _():
            o_ref[...] = acc.astype(o_ref.dtype)
    else:
        o_ref[...] = acc.astype(o_ref.dtype)


def _norm_matmul(x, g, w, cos, sin, *, tn, n_rope_tiles, head_dim, out_dtype, name):
    n, d = x.shape
    nout = w.shape[1]
    tm = ROW_TILE
    return pl.pallas_call(
        functools.partial(_norm_matmul_kernel, n_rope_tiles=n_rope_tiles, head_dim=head_dim),
        grid=(n // tm, nout // tn),
        in_specs=[
            pl.BlockSpec((tm, d), lambda i, j: (i, 0)),
            pl.BlockSpec((1, d), lambda i, j: (0, 0)),
            pl.BlockSpec((d, tn), lambda i, j: (0, j)),
            pl.BlockSpec((tm, cos.shape[1]), lambda i, j: (i, 0)),
            pl.BlockSpec((tm, sin.shape[1]), lambda i, j: (i, 0)),
        ],
        out_specs=pl.BlockSpec((tm, tn), lambda i, j: (i, j)),
        out_shape=jax.ShapeDtypeStruct((n, nout), out_dtype),
        scratch_shapes=[pltpu.VMEM((tm, d), BF16)],
        compiler_params=_cparams(2),
        name=name,
    )(x, g.reshape(1, d), w, cos, sin)


def _group_norm_gate(o, g, gn):
    mu = jnp.mean(o, axis=-1, keepdims=True)
    oc = o - mu
    var = jnp.mean(oc * oc, axis=-1, keepdims=True)
    on = oc * lax.rsqrt(var + GN_EPS) * gn
    gf = g.astype(F32)
    return (gf / (1.0 + jnp.exp(-gf))) * on


def _retention_step(q, k, v, state, lg):
    L = q.shape[0]
    row = lax.broadcasted_iota(jnp.int32, (L, L), 0)
    col = lax.broadcasted_iota(jnp.int32, (L, L), 1)
    diff = (row - col).astype(F32)
    decay = jnp.where(diff >= 0.0, jnp.exp(jnp.maximum(diff, 0.0) * lg), 0.0)
    scores = lax.dot_general(q, k, _NT, preferred_element_type=F32) * decay
    inner = jnp.dot(scores.astype(BF16), v, preferred_element_type=F32)
    idx = lax.broadcasted_iota(jnp.int32, (L, 1), 0).astype(F32)
    q_decay = jnp.exp((idx + 1.0) * lg)
    cross = jnp.dot(q, state.astype(BF16), preferred_element_type=F32) * q_decay
    k_decay = jnp.exp((L - 1.0 - idx) * lg)
    kd = (k.astype(F32) * k_decay).T.astype(BF16)
    chunk_decay = jnp.exp(jnp.full((1, 1), L, F32) * lg)
    new_state = chunk_decay * state + jnp.dot(kd, v, preferred_element_type=F32)
    return inner + cross, new_state


def _ret_prompt_kernel(lg_ref, q_ref, k_ref, v_ref, g_ref, gn_ref, o_ref, s_ref):
    h = pl.program_id(1)
    c = pl.program_id(2)

    @pl.when(c == 0)
    def _():
        s_ref[...] = jnp.zeros_like(s_ref)

    o, new_state = _retention_step(q_ref[...], k_ref[...], v_ref[...], s_ref[0, 0], lg_ref[h])
    s_ref[0, 0] = new_state
    o_ref[...] = _group_norm_gate(o, g_ref[...], gn_ref[...]).astype(o_ref.dtype)


def _ret_prompt(qkvg, gn, lg, *, batch, seq, heads, dk, dv):
    L = min(RET_CHUNK, seq)
    nc = seq // L
    kq = heads
    v0 = 2 * heads * dk // dv
    g0 = v0 + heads
    grid_spec = pltpu.PrefetchScalarGridSpec(
        num_scalar_prefetch=1,
        grid=(batch, heads, nc),
        in_specs=[
            pl.BlockSpec((L, dk), lambda b, h, c, lg: (b * nc + c, h)),
            pl.BlockSpec((L, dk), lambda b, h, c, lg: (b * nc + c, kq + h)),
            pl.BlockSpec((L, dv), lambda b, h, c, lg: (b * nc + c, v0 + h)),
            pl.BlockSpec((L, dv), lambda b, h, c, lg: (b * nc + c, g0 + h)),
            pl.BlockSpec((1, dv), lambda b, h, c, lg: (0, h)),
        ],
        out_specs=[
            pl.BlockSpec((L, dv), lambda b, h, c, lg: (b * nc + c, h)),
            pl.BlockSpec((1, 1, dk, dv), lambda b, h, c, lg: (b, h, 0, 0)),
        ],
    )
    return pl.pallas_call(
        _ret_prompt_kernel,
        grid_spec=grid_spec,
        out_shape=[
            jax.ShapeDtypeStruct((batch * seq, heads * dv), BF16),
            jax.ShapeDtypeStruct((batch, heads, dk, dv), F32),
        ],
        compiler_params=_cparams(3),
        name="ret_prompt",
    )(lg, qkvg, qkvg, qkvg, qkvg, gn.reshape(1, -1))


def _ret_sample_kernel(lg_ref, q_ref, k_ref, v_ref, g_ref, gn_ref, s_in_ref, o_ref, s_out_ref, *, t_len):
    h = pl.program_id(1)
    lg = lg_ref[h]
    nb = s_in_ref.shape[0]
    n = nb * t_len
    q = q_ref[...]
    k = k_ref[...]
    v = v_ref[...]
    g = g_ref[...]
    gn = gn_ref[...]
    row = lax.broadcasted_iota(jnp.int32, (n, n), 0)
    col = lax.broadcasted_iota(jnp.int32, (n, n), 1)
    diff = (row - col).astype(F32)
    keep = (row // t_len == col // t_len) & (row >= col)
    decay = jnp.where(keep, jnp.exp(jnp.maximum(diff, 0.0) * lg), 0.0)
    scores = lax.dot_general(q, k, _NT, preferred_element_type=F32) * decay
    inner = jnp.dot(scores.astype(BF16), v, preferred_element_type=F32)
    idx = (lax.broadcasted_iota(jnp.int32, (n, 1), 0) % t_len).astype(F32)
    q_decay = jnp.exp((idx + 1.0) * lg)
    kd = k.astype(F32) * jnp.exp((t_len - 1.0 - idx) * lg)
    chunk_decay = jnp.exp(jnp.full((1, 1), t_len, F32) * lg)
    group = BF16_ROWS // t_len
    seq_of_row = lax.broadcasted_iota(jnp.int32, (BF16_ROWS, 1), 0) // t_len
    for p in range(nb // group):
        rows = slice(p * BF16_ROWS, (p + 1) * BF16_ROWS)
        q16, kd16, v16 = q[rows], kd[rows], v[rows]
        cross = jnp.zeros((BF16_ROWS, v.shape[1]), F32)
        for j in range(group):
            b = p * group + j
            state = s_in_ref[b, 0]
            mine = seq_of_row == j
            cross = jnp.where(mine, jnp.dot(q16, state.astype(BF16), preferred_element_type=F32), cross)
            kdb = jnp.where(mine, kd16, 0.0).T.astype(BF16)
            s_out_ref[b, 0] = chunk_decay * state + jnp.dot(kdb, v16, preferred_element_type=F32)
        o = inner[rows] + cross * q_decay[rows]
        o_ref[rows, :] = _group_norm_gate(o, g[rows], gn).astype(o_ref.dtype)


def _ret_sample(qkvg, gn, lg, state, *, row0, t_len, heads, dk, dv):
    nbatch = state.shape[0]
    nb = 8
    rows = nb * t_len
    r0 = row0 // rows
    kq = heads
    v0 = 2 * heads * dk // dv
    g0 = v0 + heads
    grid_spec = pltpu.PrefetchScalarGridSpec(
        num_scalar_prefetch=1,
        grid=(nbatch // nb, heads),
        in_specs=[
            pl.BlockSpec((rows, dk), lambda i, h, lg: (r0 + i, h)),
            pl.BlockSpec((rows, dk), lambda i, h, lg: (r0 + i, kq + h)),
            pl.BlockSpec((rows, dv), lambda i, h, lg: (r0 + i, v0 + h)),
            pl.BlockSpec((rows, dv), lambda i, h, lg: (r0 + i, g0 + h)),
            pl.BlockSpec((1, dv), lambda i, h, lg: (0, h)),
            pl.BlockSpec((nb, 1, dk, dv), lambda i, h, lg: (i, h, 0, 0)),
        ],
        out_specs=[
            pl.BlockSpec((rows, dv), lambda i, h, lg: (i, h)),
            pl.BlockSpec((nb, 1, dk, dv), lambda i, h, lg: (i, h, 0, 0)),
        ],
    )
    return pl.pallas_call(
        functools.partial(_ret_sample_kernel, t_len=t_len),
        grid_spec=grid_spec,
        out_shape=[
            jax.ShapeDtypeStruct((nbatch * t_len, heads * dv), BF16),
            jax.ShapeDtypeStruct(state.shape, F32),
        ],
        compiler_params=_cparams(2),
        name="ret_sample",
    )(lg, qkvg, qkvg, qkvg, qkvg, gn.reshape(1, -1), state)


def _matmul_res_kernel(x_ref, w_ref, r_ref, o_ref):
    o_ref[...] = r_ref[...] + jnp.dot(x_ref[...], w_ref[...], preferred_element_type=F32)


def _matmul_res(x, w, res, *, name):
    n, k = x.shape
    d = w.shape[1]
    tm = ROW_TILE
    return pl.pallas_call(
        _matmul_res_kernel,
        grid=(n // tm,),
        in_specs=[
            pl.BlockSpec((tm, k), lambda i: (i, 0)),
            pl.BlockSpec((k, d), lambda i: (0, 0)),
            pl.BlockSpec((tm, d), lambda i: (i, 0)),
        ],
        out_specs=pl.BlockSpec((tm, d), lambda i: (i, 0)),
        out_shape=jax.ShapeDtypeStruct((n, d), F32),
        compiler_params=_cparams(1),
        name=name,
    )(x, w, res)


def _router_kernel(x_ref, g_ref, wh_ref, wl_ref, b_ref, idx_ref, gate_ref, *, n_experts):
    xn = _rms(x_ref[...], g_ref[...])
    hi = xn.astype(BF16)
    lo = (xn - hi.astype(F32)).astype(BF16)
    logits = (jnp.dot(hi, wh_ref[...], preferred_element_type=F32)
              + jnp.dot(lo, wh_ref[...], preferred_element_type=F32)
              + jnp.dot(hi, wl_ref[...], preferred_element_type=F32)) + b_ref[...]
    lane = lax.broadcasted_iota(jnp.int32, logits.shape, 1).astype(F32)
    neg = jnp.float32(-jnp.inf)
    work = jnp.where(lane < n_experts, logits, neg)
    vals, idxs = [], []
    for _ in range(TOP_K):
        m = jnp.max(work, axis=-1, keepdims=True)
        sel = jnp.min(jnp.where(work == m, lane, float(LANES)), axis=-1, keepdims=True)
        vals.append(m)
        idxs.append(sel)
        work = jnp.where(lane == sel, neg, work)
    es = [jnp.exp(v - vals[0]) for v in vals]
    denom = es[0]
    for e in es[1:]:
        denom = denom + e
    idx_out = jnp.zeros(logits.shape, F32)
    gate_out = jnp.zeros(logits.shape, F32)
    for kk in range(TOP_K):
        idx_out = jnp.where(lane == kk, idxs[kk], idx_out)
        gate_out = jnp.where(lane == kk, es[kk] / denom, gate_out)
    idx_ref[...] = idx_out.astype(jnp.int32)
    gate_ref[...] = gate_out


def _router(y, g, wr, br):
    n, d = y.shape
    e = wr.shape[1]
    tm = ROW_TILE
    wr_pad = jnp.zeros((d, LANES), F32).at[:, :e].set(wr)
    wh = wr_pad.astype(BF16)
    wl = (wr_pad - wh.astype(F32)).astype(BF16)
    b_pad = jnp.zeros((1, LANES), F32).at[0, :e].set(br)
    return pl.pallas_call(
        functools.partial(_router_kernel, n_experts=e),
        grid=(n // tm,),
        in_specs=[
            pl.BlockSpec((tm, d), lambda i: (i, 0)),
            pl.BlockSpec((1, d), lambda i: (0, 0)),
            pl.BlockSpec((d, LANES), lambda i: (0, 0)),
            pl.BlockSpec((d, LANES), lambda i: (0, 0)),
            pl.BlockSpec((1, LANES), lambda i: (0, 0)),
        ],
        out_specs=[
            pl.BlockSpec((tm, LANES), lambda i: (i, 0)),
            pl.BlockSpec((tm, LANES), lambda i: (i, 0)),
        ],
        out_shape=[
            jax.ShapeDtypeStruct((n, LANES), jnp.int32),
            jax.ShapeDtypeStruct((n, LANES), F32),
        ],
        compiler_params=_cparams(1),
        name="moe_router",
    )(y, g.reshape(1, d), wh, wl, b_pad)


def _split_gate_up_kernel(w_ref, p_ref, o_ref):
    half = o_ref.shape[1] // 2
    hp = MXU_DIM // 2
    for c in range(o_ref.shape[1] // MXU_DIM):
        y = jnp.dot(w_ref[:, c * MXU_DIM:(c + 1) * MXU_DIM].astype(BF16), p_ref[...], preferred_element_type=F32)
        o_ref[:, c * hp:(c + 1) * hp] = y[:, :hp].astype(BF16)
        o_ref[:, half + c * hp:half + (c + 1) * hp] = y[:, hp:].astype(BF16)


def _split_gate_up(wgu):
    e, d, w2 = wgu.shape
    rows = e * d
    tm = ROW_TILE
    src = lax.broadcasted_iota(jnp.int32, (MXU_DIM, MXU_DIM), 0)
    dst = lax.broadcasted_iota(jnp.int32, (MXU_DIM, MXU_DIM), 1)
    perm = (dst == (src % 2) * (MXU_DIM // 2) + src // 2).astype(BF16)
    out = pl.pallas_call(
        _split_gate_up_kernel,
        grid=(rows // tm,),
        in_specs=[pl.BlockSpec((tm, w2), lambda i: (i, 0)), pl.BlockSpec((MXU_DIM, MXU_DIM), lambda i: (0, 0))],
        out_specs=pl.BlockSpec((tm, w2), lambda i: (i, 0)),
        out_shape=jax.ShapeDtypeStruct((rows, w2), BF16),
        compiler_params=_cparams(1),
        name="moe_split_gate_up",
    )(wgu.reshape(rows, w2), perm)
    return out.reshape(e, d, w2)


def _row_copy(src_hbm, dst, sem, src_row, dst_row):
    return pltpu.make_async_copy(src_hbm.at[pl.ds(src_row, 1)], dst.at[pl.ds(dst_row, 1)], sem)


def _expert_kernel(be_ref, nv_ref, tok_ref, y_hbm, g_ref, wgu_ref, bgu_ref, wd_ref, bd_ref, o_ref, xbuf, sem,
                   *, d_expert):
    i = pl.program_id(0)
    nv = nv_ref[0]
    blk = o_ref.shape[0]
    slot = i % 2

    def gather(block, sl, wait):
        def body(r, _):
            cp = _row_copy(y_hbm, xbuf.at[sl], sem.at[sl], tok_ref[block * blk + r], r)
            cp.wait() if wait else cp.start()
            return 0
        lax.fori_loop(0, blk, body, 0, unroll=DMA_UNROLL)

    @pl.when(i == 0)
    def _():
        gather(0, 0, False)

    @pl.when(i + 1 < nv)
    def _():
        gather(i + 1, 1 - slot, False)

    @pl.when(i < nv)
    def _():
        gather(i, slot, True)
        x = _rms(xbuf[slot], g_ref[...]).astype(BF16)
        hgu = jnp.dot(x, wgu_ref[0], preferred_element_type=F32) + bgu_ref[0]
        gate = jnp.minimum(hgu[:, :d_expert], SWIGLU_LIMIT)
        up = jnp.clip(hgu[:, d_expert:], -SWIGLU_LIMIT, SWIGLU_LIMIT)
        act = (up + 1.0) * gate * (1.0 / (1.0 + jnp.exp(-(gate * SWIGLU_ALPHA))))
        o_ref[...] = jnp.dot(act.astype(BF16), wd_ref[0], preferred_element_type=F32) + bd_ref[0]

    @pl.when(i >= nv)
    def _():
        o_ref[...] = jnp.zeros_like(o_ref)


def _experts(y, g, rows_tok, block_expert, n_valid, wgu, bgu, wd, bd):
    n, d = y.shape
    r = rows_tok.shape[0]
    de = wd.shape[1]
    blk = MOE_BLOCK
    grid_spec = pltpu.PrefetchScalarGridSpec(
        num_scalar_prefetch=3,
        grid=(r // blk,),
        in_specs=[
            pl.BlockSpec(memory_space=pl.ANY),
            pl.BlockSpec((1, d), lambda i, be, nv, tok: (0, 0)),
            pl.BlockSpec((1, d, 2 * de), lambda i, be, nv, tok: (be[i], 0, 0)),
            pl.BlockSpec((1, 1, 2 * de), lambda i, be, nv, tok: (be[i], 0, 0)),
            pl.BlockSpec((1, de, d), lambda i, be, nv, tok: (be[i], 0, 0)),
            pl.BlockSpec((1, 1, d), lambda i, be, nv, tok: (be[i], 0, 0)),
        ],
        out_specs=pl.BlockSpec((blk, d), lambda i, be, nv, tok: (i, 0)),
        scratch_shapes=[pltpu.VMEM((2, blk, d), F32), pltpu.SemaphoreType.DMA((2,))],
    )
    return pl.pallas_call(
        functools.partial(_expert_kernel, d_expert=de),
        grid_spec=grid_spec,
        out_shape=jax.ShapeDtypeStruct((r, d), F32),
        compiler_params=_cparams(1),
        name="moe_experts",
    )(block_expert, n_valid, rows_tok, y, g.reshape(1, d), wgu, bgu, wd, bd)


def _combine_kernel(dest_ref, res_ref, gate_ref, yb_hbm, o_ref, buf, sem):
    i = pl.program_id(0)
    nsteps = pl.num_programs(0)
    tm = o_ref.shape[0]
    slot = i % 2

    def gather(tile, sl, wait):
        def body(t, _):
            for kk in range(TOP_K):
                cp = _row_copy(yb_hbm, buf.at[sl, kk], sem.at[sl], dest_ref[(tile * tm + t) * TOP_K + kk], t)
                cp.wait() if wait else cp.start()
            return 0
        lax.fori_loop(0, tm, body, 0, unroll=DMA_UNROLL // TOP_K)

    @pl.when(i == 0)
    def _():
        gather(0, 0, False)

    @pl.when(i + 1 < nsteps)
    def _():
        gather(i + 1, 1 - slot, False)

    gather(i, slot, True)
    acc = res_ref[...]
    gates = gate_ref[...]
    for kk in range(TOP_K):
        acc = acc + gates[:, kk:kk + 1] * buf[slot, kk]
    o_ref[...] = acc


def _combine(res, gate_pad, dest, yb):
    n, d = res.shape
    tm = COMBINE_TILE
    grid_spec = pltpu.PrefetchScalarGridSpec(
        num_scalar_prefetch=1,
        grid=(n // tm,),
        in_specs=[
            pl.BlockSpec((tm, d), lambda i, dest: (i, 0)),
            pl.BlockSpec((tm, LANES), lambda i, dest: (i, 0)),
            pl.BlockSpec(memory_space=pl.ANY),
        ],
        out_specs=pl.BlockSpec((tm, d), lambda i, dest: (i, 0)),
        scratch_shapes=[pltpu.VMEM((2, TOP_K, tm, d), F32), pltpu.SemaphoreType.DMA((2,))],
    )
    return pl.pallas_call(
        _combine_kernel,
        grid_spec=grid_spec,
        out_shape=jax.ShapeDtypeStruct((n, d), F32),
        compiler_params=_cparams(1),
        name="moe_combine",
    )(dest, res, gate_pad, yb)


def _moe(y, g, wr, br, wgu, bgu, wd, bd):
    n, d = y.shape
    e = wr.shape[1]
    de = wd.shape[1]
    idx_pad, gate_pad = _router(y, g, wr, br)

    a = n * TOP_K
    blk = MOE_BLOCK
    flat_e = idx_pad[:, :TOP_K].reshape(a)
    onehot = (flat_e[:, None] == jnp.arange(e, dtype=jnp.int32)[None, :]).astype(jnp.int32)
    csum = jnp.cumsum(onehot, axis=0)
    counts = csum[-1]
    rank = jnp.take_along_axis(csum, flat_e[:, None], axis=1)[:, 0] - 1
    padded = ((counts + blk - 1) // blk) * blk
    ends = jnp.cumsum(padded)
    dest = ((ends - padded)[flat_e] + rank).astype(jnp.int32)
    n_blocks = -(-a // blk) + e
    r = n_blocks * blk
    flat_tok = jnp.arange(a, dtype=jnp.int32) // TOP_K
    rows_tok = jnp.zeros((r,), jnp.int32).at[dest].set(flat_tok)
    block_start = jnp.arange(n_blocks, dtype=jnp.int32) * blk
    block_expert = jnp.minimum(jnp.sum((ends[None, :] <= block_start[:, None]).astype(jnp.int32), axis=1), e - 1)
    n_valid = (ends[-1:] // blk).astype(jnp.int32)

    wgu_b = _split_gate_up(wgu)
    bgu_b = jnp.concatenate([bgu[:, 0::2], bgu[:, 1::2]], axis=-1).reshape(e, 1, 2 * de)
    yb = _experts(y, g, rows_tok, block_expert, n_valid, wgu_b, bgu_b, wd.astype(BF16), bd.reshape(e, 1, d))
    return _combine(y, gate_pad, dest, yb)


def _rope_lanes(x, c, sa, sb, half):
    n = x.shape[1]
    return x * c + pltpu.roll(x, n - half, 1) * sa + pltpu.roll(x, half, 1) * sb


def _mla_proj_kernel(dqkv_ref, gq_ref, gkv_ref, wuq_ref, wuk_ref, wuv_ref, ones_ref, c_ref, sa_ref, sb_ref,
                     q_ref, k_ref, v_ref, ckv_ref, kpe_ref, *, q_rank, kv_rank, heads, rope_half, scale):
    x = dqkv_ref[...]
    c = c_ref[...]
    sa = sa_ref[...]
    sb = sb_ref[...]
    cq = _rms(x[:, :q_rank], gq_ref[...]).astype(BF16)
    ckv = _rms(x[:, q_rank:q_rank + kv_rank], gkv_ref[...])
    ckv_ref[...] = ckv
    kpe = _rope_lanes(x[:, q_rank + kv_rank:], c, sa, sb, rope_half)
    kpe_ref[...] = kpe
    ckv_b = ckv.astype(BF16)
    q = jnp.dot(cq, wuq_ref[...], preferred_element_type=F32)
    k = jnp.dot(ckv_b, wuk_ref[...], preferred_element_type=F32)
    for h in range(heads):
        cols = slice(h * LANES, (h + 1) * LANES)
        q_ref[:, cols] = (_rope_lanes(q[:, cols], c, sa, sb, rope_half) * scale).astype(BF16)
        k_ref[:, cols] = (k[:, cols] + kpe).astype(BF16)
    v_ref[...] = (jnp.dot(ckv_b, wuv_ref[...], preferred_element_type=F32) + ones_ref[...]).astype(BF16)


def _mla_proj(dqkv, gq, gkv, wuq_p, wuk_p, wuv_p, ones_p, c, sa, sb, *, q_rank, kv_rank, heads, rope_half, scale):
    n, w = dqkv.shape
    tm = ROW_TILE
    hv = wuv_p.shape[1]
    full = lambda shape: pl.BlockSpec(shape, lambda i: (0,) * len(shape))
    rows = lambda width: pl.BlockSpec((tm, width), lambda i: (i, 0))
    return pl.pallas_call(
        functools.partial(_mla_proj_kernel, q_rank=q_rank, kv_rank=kv_rank, heads=heads, rope_half=rope_half,
                          scale=scale),
        grid=(n // tm,),
        in_specs=[rows(w), full((1, q_rank)), full((1, kv_rank)), full(wuq_p.shape), full(wuk_p.shape),
                  full(wuv_p.shape), full((1, hv)), rows(LANES), rows(LANES), rows(LANES)],
        out_specs=[rows(heads * LANES), rows(heads * LANES), rows(hv), rows(kv_rank), rows(LANES)],
        out_shape=[
            jax.ShapeDtypeStruct((n, heads * LANES), BF16),
            jax.ShapeDtypeStruct((n, heads * LANES), BF16),
            jax.ShapeDtypeStruct((n, hv), BF16),
            jax.ShapeDtypeStruct((n, kv_rank), F32),
            jax.ShapeDtypeStruct((n, LANES), F32),
        ],
        compiler_params=_cparams(1),
        name="mla_proj",
    )(dqkv, gq.reshape(1, -1), gkv.reshape(1, -1), wuq_p, wuk_p, wuv_p, ones_p, c, sa, sb)


def _flash_kernel(q_ref, k_ref, v_ref, o_ref, *scratch, tile, heads, v_head):
    m_refs, acc_refs = scratch[:heads], scratch[heads:]
    qi = pl.program_id(2)
    for h in range(heads):
        m_refs[h][...] = jnp.full((tile, LANES), -jnp.inf, F32)
        acc_refs[h][...] = jnp.zeros((tile, LANES), F32)
    nchunk = tile // LANES

    def step(ki, masked):
        start = pl.multiple_of(ki * tile, tile)
        scores = []
        for h in range(heads):
            cols = slice(h * LANES, (h + 1) * LANES)
            s = lax.dot_general(q_ref[:, cols], k_ref[pl.ds(start, tile), cols], _NT, preferred_element_type=F32)
            if masked:
                row = lax.broadcasted_iota(jnp.int32, s.shape, 0)
                col = lax.broadcasted_iota(jnp.int32, s.shape, 1)
                s = jnp.where(col <= row, s, jnp.finfo(F32).min)
            scores.append(s)
        for h in range(heads):
            cols = slice(h * LANES, (h + 1) * LANES)
            chunks = [scores[h][:, c * LANES:(c + 1) * LANES] for c in range(nchunk)]
            part = chunks[0]
            for ch in chunks[1:]:
                part = jnp.maximum(part, ch)
            m_old = m_refs[h][...]
            m_new = jnp.maximum(m_old, jnp.max(part, axis=-1, keepdims=True))
            m_refs[h][...] = m_new
            p = jnp.concatenate([jnp.exp2(ch - m_new) for ch in chunks], axis=1).astype(BF16)
            acc_refs[h][...] = (jnp.exp2(m_old - m_new) * acc_refs[h][...]
                                + jnp.dot(p, v_ref[pl.ds(start, tile), cols], preferred_element_type=F32))

    def body(ki, carry):
        step(ki, False)
        return carry

    lax.fori_loop(0, qi, body, 0)
    step(qi, True)
    lane = lax.broadcasted_iota(jnp.int32, (tile, LANES), 1)
    for j in range(heads // 2):
        even = acc_refs[2 * j][...]
        odd = acc_refs[2 * j + 1][...]
        num = jnp.where(lane < v_head, even, odd)
        den = jnp.where(lane < v_head, pltpu.roll(even, v_head, 1), pltpu.roll(odd, v_head, 1))
        o_ref[:, j * LANES:(j + 1) * LANES] = (num / den).astype(o_ref.dtype)


def _flash(q_cat, k_cat, v_ext, *, batch, seq, heads, v_head):
    tile = min(ATTN_TILE, seq)
    nq = seq // tile
    hg = ATTN_HEADS
    return pl.pallas_call(
        functools.partial(_flash_kernel, tile=tile, heads=hg, v_head=v_head),
        grid=(batch, heads // hg, nq),
        in_specs=[
            pl.BlockSpec((tile, hg * LANES), lambda b, g, qi: (b * nq + qi, g)),
            pl.BlockSpec((seq, hg * LANES), lambda b, g, qi: (b, g)),
            pl.BlockSpec((seq, hg * LANES), lambda b, g, qi: (b, g)),
        ],
        out_specs=pl.BlockSpec((tile, hg * v_head), lambda b, g, qi: (b * nq + qi, g)),
        out_shape=jax.ShapeDtypeStruct((batch * seq, heads * v_head), BF16),
        scratch_shapes=[pltpu.VMEM((tile, LANES), F32)] * (2 * hg),
        compiler_params=_cparams(3),
        name="mla_flash",
    )(q_cat, k_cat, v_ext)


def _qlat_kernel(q_ref, w_ref, o_ref, *, heads, kv_rank):
    for h in range(heads):
        o_ref[:, h * kv_rank:(h + 1) * kv_rank] = jnp.dot(
            q_ref[:, h * LANES:(h + 1) * LANES], w_ref[h], preferred_element_type=F32).astype(o_ref.dtype)


def _qlat(q_cat, wukt_p, *, row0, nrows, heads, kv_rank):
    tm = min(ROW_TILE, nrows)
    r0 = row0 // tm
    return pl.pallas_call(
        functools.partial(_qlat_kernel, heads=heads, kv_rank=kv_rank),
        grid=(nrows // tm,),
        in_specs=[
            pl.BlockSpec((tm, heads * LANES), lambda i: (r0 + i, 0)),
            pl.BlockSpec(wukt_p.shape, lambda i: (0, 0, 0)),
        ],
        out_specs=pl.BlockSpec((tm, heads * kv_rank), lambda i: (i, 0)),
        out_shape=jax.ShapeDtypeStruct((nrows, heads * kv_rank), BF16),
        compiler_params=_cparams(1),
        name="mla_qlat",
    )(q_cat, wukt_p)


def _olat_kernel(x_ref, w_ref, o_ref, *, pairs, kv_rank):
    for p in range(pairs):
        o_ref[:, p * LANES:(p + 1) * LANES] = jnp.dot(
            x_ref[:, p * 2 * kv_rank:(p + 1) * 2 * kv_rank], w_ref[p], preferred_element_type=F32).astype(o_ref.dtype)


def _olat(o_lat, wuv_pair, *, heads, kv_rank):
    n = o_lat.shape[0]
    tm = min(ROW_TILE, n)
    return pl.pallas_call(
        functools.partial(_olat_kernel, pairs=heads // 2, kv_rank=kv_rank),
        grid=(n // tm,),
        in_specs=[
            pl.BlockSpec((tm, heads * kv_rank), lambda i: (i, 0)),
            pl.BlockSpec(wuv_pair.shape, lambda i: (0, 0, 0)),
        ],
        out_specs=pl.BlockSpec((tm, (heads // 2) * LANES), lambda i: (i, 0)),
        out_shape=jax.ShapeDtypeStruct((n, (heads // 2) * LANES), BF16),
        compiler_params=_cparams(1),
        name="mla_olat",
    )(o_lat, wuv_pair)


def _paged_copies(pt_ref, cache_ckv, cache_kpe, ckv_buf, kpe_buf, sem, b, slot, p, page):
    pg = pt_ref[b, p]
    return (
        pltpu.make_async_copy(cache_ckv.at[pg], ckv_buf.at[slot, pl.ds(p * page, page)], sem.at[0, slot]),
        pltpu.make_async_copy(cache_kpe.at[pg], kpe_buf.at[slot, pl.ds(p * page, page)], sem.at[1, slot]),
    )


def _paged_attn_kernel(pt_ref, qlat_ref, qpe_ref, cnew_ref, pnew_ref, cache_ckv, cache_kpe, o_ref,
                       ckv_buf, kpe_buf, sem, *, n_pages, page, heads):
    b = pl.program_id(0)
    nb = pl.num_programs(0)
    slot = b % 2

    def start_all(bb, sl):
        def body(p, _):
            for cp in _paged_copies(pt_ref, cache_ckv, cache_kpe, ckv_buf, kpe_buf, sem, bb, sl, p, page):
                cp.start()
            return 0
        lax.fori_loop(0, n_pages, body, 0)

    @pl.when(b == 0)
    def _():
        start_all(0, 0)

    @pl.when(b + 1 < nb)
    def _():
        start_all(b + 1, 1 - slot)

    def wait_body(p, _):
        for cp in _paged_copies(pt_ref, cache_ckv, cache_kpe, ckv_buf, kpe_buf, sem, b, slot, p, page):
            cp.wait()
        return 0
    lax.fori_loop(0, n_pages, wait_body, 0)

    qlat = qlat_ref[0]
    qpe = qpe_ref[0]
    ck = ckv_buf[slot].astype(BF16)
    kp = kpe_buf[slot].astype(BF16)
    s_past = (lax.dot_general(qlat, ck, _NT, preferred_element_type=F32)
              + lax.dot_general(qpe, kp, _NT, preferred_element_type=F32))
    cn = cnew_ref[0].astype(BF16)
    pn = pnew_ref[0].astype(BF16)
    s_new = (lax.dot_general(qlat, cn, _NT, preferred_element_type=F32)
             + lax.dot_general(qpe, pn, _NT, preferred_element_type=F32))
    t_of_row = lax.broadcasted_iota(jnp.int32, s_new.shape, 0) // heads
    j = lax.broadcasted_iota(jnp.int32, s_new.shape, 1)
    s_new = jnp.where(j <= t_of_row, s_new, jnp.finfo(F32).min)
    m = jnp.maximum(jnp.max(s_past, axis=-1, keepdims=True), jnp.max(s_new, axis=-1, keepdims=True))
    p_past = jnp.exp2(s_past - m)
    p_new = jnp.exp2(s_new - m)
    denom = jnp.sum(p_past, axis=-1, keepdims=True) + jnp.sum(p_new, axis=-1, keepdims=True)
    o = (jnp.dot(p_past.astype(BF16), ck, preferred_element_type=F32)
         + jnp.dot(p_new.astype(BF16), cn, preferred_element_type=F32))
    o_ref[0] = (o / denom).astype(o_ref.dtype)


def _paged_attn(page_table, qlat, qpe, ckv_new, kpe_new, cache_ckv, cache_kpe, *, heads):
    nbatch, n_pages = page_table.shape
    page, kv_rank = cache_ckv.shape[1], cache_ckv.shape[2]
    rope = cache_kpe.shape[2]
    rows = qlat.shape[1]
    tpad = ckv_new.shape[1]
    grid_spec = pltpu.PrefetchScalarGridSpec(
        num_scalar_prefetch=1,
        grid=(nbatch,),
        in_specs=[
            pl.BlockSpec((1, rows, kv_rank), lambda b, pt: (b, 0, 0)),
            pl.BlockSpec((1, rows, rope), lambda b, pt: (b, 0, 0)),
            pl.BlockSpec((1, tpad, kv_rank), lambda b, pt: (b, 0, 0)),
            pl.BlockSpec((1, tpad, rope), lambda b, pt: (b, 0, 0)),
            pl.BlockSpec(memory_space=pl.ANY),
            pl.BlockSpec(memory_space=pl.ANY),
        ],
        out_specs=pl.BlockSpec((1, rows, kv_rank), lambda b, pt: (b, 0, 0)),
        scratch_shapes=[
            pltpu.VMEM((2, n_pages * page, kv_rank), F32),
            pltpu.VMEM((2, n_pages * page, rope), F32),
            pltpu.SemaphoreType.DMA((2, 2)),
        ],
    )
    return pl.pallas_call(
        functools.partial(_paged_attn_kernel, n_pages=n_pages, page=page, heads=heads),
        grid_spec=grid_spec,
        out_shape=jax.ShapeDtypeStruct((nbatch, rows, kv_rank), BF16),
        compiler_params=_cparams(1),
        name="mla_paged_attn",
    )(page_table, qlat, qpe, ckv_new, kpe_new, cache_ckv, cache_kpe)


def _rmsnorm_kernel(x_ref, g_ref, o_ref):
    o_ref[...] = _rms(x_ref[...], g_ref[...])


def _rmsnorm(x, g):
    n, d = x.shape
    tm = ROW_TILE
    return pl.pallas_call(
        _rmsnorm_kernel,
        grid=(n // tm,),
        in_specs=[pl.BlockSpec((tm, d), lambda i: (i, 0)), pl.BlockSpec((1, d), lambda i: (0, 0))],
        out_specs=pl.BlockSpec((tm, d), lambda i: (i, 0)),
        out_shape=jax.ShapeDtypeStruct((n, d), F32),
        compiler_params=_cparams(1),
        name="final_norm",
    )(x, g.reshape(1, d))


def _rope_angles(pos, half):
    inv = ROPE_BASE ** (-jnp.arange(half, dtype=F32) / half)
    return pos.astype(F32)[:, None] * inv[None, :]


def _token_positions(batch, seq, dec_batch, dec_seq, past_len):
    return jnp.concatenate([jnp.tile(jnp.arange(seq), batch), jnp.tile(past_len + jnp.arange(dec_seq), dec_batch)])


def kernel(x_prompt, x_sample, state_ret, cache_ckv, cache_kpe, page_table, norm_mix, norm_ffn, norm_final, ret_wq, ret_wk, ret_wv, ret_wg, ret_gn, ret_wo, mla_wdq, mla_gq, mla_wuq, mla_wdkv, mla_gkv, mla_wuk, mla_wuv, mla_wo, moe_wr, moe_br, moe_wgu, moe_bgu, moe_wd, moe_bd):
    batch, seq, d = x_prompt.shape
    dec_batch, dec_seq, _ = x_sample.shape
    ret_heads, dk, dv = state_ret.shape[2], state_ret.shape[3], state_ret.shape[4]
    page = cache_ckv.shape[2]
    kv_rank = cache_ckv.shape[3]
    rope_dim = cache_kpe.shape[3]
    past_len = page_table.shape[1] * page
    mla_heads, nope = mla_wuk.shape[2], mla_wuk.shape[3]
    v_head = mla_wuv.shape[3]
    q_rank = mla_wdq.shape[2]
    n_prompt = batch * seq
    n_sample = dec_batch * dec_seq
    assert nope + rope_dim <= LANES and 2 * v_head == LANES and dk == 2 * LANES

    y = jnp.concatenate([x_prompt.reshape(n_prompt, d), x_sample.reshape(n_sample, d)], axis=0)
    pos = _token_positions(batch, seq, dec_batch, dec_seq, past_len)

    ang = _rope_angles(pos, dk // 2)
    w_all = jnp.concatenate([ret_wq[0], ret_wk[0] * (dk ** -0.5), ret_wv[0], ret_wg[0]], axis=1).astype(BF16)
    qkvg = _norm_matmul(y, norm_mix[0], w_all, jnp.cos(ang), jnp.sin(ang), tn=ret_heads * dk,
                        n_rope_tiles=2, head_dim=dk, out_dtype=BF16, name="ret_proj")
    lg = jnp.log(1.0 - 2.0 ** (-5.0 - jnp.arange(ret_heads, dtype=F32)))
    gated_p, state_p = _ret_prompt(qkvg, ret_gn[0], lg, batch=batch, seq=seq, heads=ret_heads, dk=dk, dv=dv)
    gated_s, state_s = _ret_sample(qkvg, ret_gn[0], lg, state_ret[0], row0=n_prompt, t_len=dec_seq,
                                   heads=ret_heads, dk=dk, dv=dv)
    gated = jnp.concatenate([gated_p, gated_s], axis=0)
    y = _matmul_res(gated, ret_wo[0].astype(BF16), y, name="ret_out")
    y = _moe(y, norm_ffn[0], moe_wr[0], moe_br[0], moe_wgu[0], moe_bgu[0], moe_wd[0], moe_bd[0])

    half = rope_dim // 2
    ang = _rope_angles(pos, half)
    cos2 = jnp.concatenate([jnp.cos(ang), jnp.cos(ang)], axis=1)
    sin = jnp.sin(ang)
    n_all = y.shape[0]
    ones = jnp.ones((n_all, LANES), F32)
    zeros = jnp.zeros((n_all, LANES), F32)
    c_tab = lax.dynamic_update_slice(ones, cos2, (0, nope))
    sa_tab = lax.dynamic_update_slice(zeros, -sin, (0, nope))
    sb_tab = lax.dynamic_update_slice(zeros, sin, (0, nope + half))

    wdkv = mla_wdkv[0]
    kpe_cols = jnp.zeros((d, LANES), F32).at[:, nope:nope + rope_dim].set(wdkv[:, kv_rank:])
    w_down = jnp.concatenate([mla_wdq[0], wdkv[:, :kv_rank], kpe_cols], axis=1).astype(BF16)
    dqkv = _norm_matmul(y, norm_mix[1], w_down, zeros, zeros, tn=w_down.shape[1],
                        n_rope_tiles=0, head_dim=0, out_dtype=F32, name="mla_down")

    qk_dim = nope + rope_dim
    wuq_p = jnp.zeros((q_rank, mla_heads, LANES), F32).at[:, :, :qk_dim].set(
        mla_wuq[0].reshape(q_rank, mla_heads, qk_dim)).reshape(q_rank, mla_heads * LANES).astype(BF16)
    wuk_p = jnp.zeros((kv_rank, mla_heads, LANES), F32).at[:, :, :nope].set(mla_wuk[0]).reshape(
        kv_rank, mla_heads * LANES).astype(BF16)
    wuv_pairs = mla_wuv[0].reshape(kv_rank, mla_heads // 2, 2, v_head)
    wuv_p = jnp.zeros((kv_rank, mla_heads // 2, 2, 2, v_head), F32)
    wuv_p = wuv_p.at[:, :, 0, 0].set(wuv_pairs[:, :, 0]).at[:, :, 1, 1].set(wuv_pairs[:, :, 1])
    wuv_p = wuv_p.reshape(kv_rank, mla_heads * LANES).astype(BF16)
    ones_p = jnp.zeros((mla_heads // 2, 2, 2, v_head), F32).at[:, 0, 1].set(1.0).at[:, 1, 0].set(1.0)
    ones_p = ones_p.reshape(1, mla_heads * LANES)
    scale = (float(qk_dim) ** -0.5) * math.log2(math.e)
    q_cat, k_cat, v_ext, ckv_all, kpe_all = _mla_proj(
        dqkv, mla_gq[0], mla_gkv[0], wuq_p, wuk_p, wuv_p, ones_p, c_tab, sa_tab, sb_tab,
        q_rank=q_rank, kv_rank=kv_rank, heads=mla_heads, rope_half=half, scale=scale)
    kpe_rows = kpe_all[:, nope:nope + rope_dim]

    o_p = _flash(q_cat, k_cat, v_ext, batch=batch, seq=seq, heads=mla_heads, v_head=v_head)

    wukt_p = jnp.zeros((mla_heads, LANES, kv_rank), F32).at[:, :nope, :].set(
        jnp.transpose(mla_wuk[0], (1, 2, 0))).astype(BF16)
    q_lat = _qlat(q_cat, wukt_p, row0=n_prompt, nrows=n_sample, heads=mla_heads, kv_rank=kv_rank)
    rows = dec_seq * mla_heads
    q_pe = q_cat[n_prompt:].reshape(n_sample, mla_heads, LANES)[:, :, nope:qk_dim].reshape(dec_batch, rows, rope_dim)
    tpad = BF16_ROWS
    ckv_new = jnp.zeros((dec_batch, tpad, kv_rank), F32).at[:, :dec_seq].set(
        ckv_all[n_prompt:].reshape(dec_batch, dec_seq, kv_rank))
    kpe_new = jnp.zeros((dec_batch, tpad, rope_dim), F32).at[:, :dec_seq].set(
        kpe_rows[n_prompt:].reshape(dec_batch, dec_seq, rope_dim))
    o_lat = _paged_attn(page_table, q_lat.reshape(dec_batch, rows, kv_rank), q_pe, ckv_new, kpe_new,
                        cache_ckv[0], cache_kpe[0], heads=mla_heads)
    wuv_h = jnp.transpose(mla_wuv[0], (1, 0, 2)).reshape(mla_heads // 2, 2, kv_rank, v_head)
    wuv_pair = jnp.zeros((mla_heads // 2, 2, kv_rank, 2, v_head), F32)
    wuv_pair = wuv_pair.at[:, 0, :, 0, :].set(wuv_h[:, 0]).at[:, 1, :, 1, :].set(wuv_h[:, 1])
    wuv_pair = wuv_pair.reshape(mla_heads // 2, 2 * kv_rank, LANES).astype(BF16)
    o_s = _olat(o_lat.reshape(n_sample, mla_heads * kv_rank), wuv_pair, heads=mla_heads, kv_rank=kv_rank)

    o_all = jnp.concatenate([o_p, o_s], axis=0)
    y = _matmul_res(o_all, mla_wo[0].astype(BF16), y, name="mla_out")
    y = _moe(y, norm_ffn[1], moe_wr[1], moe_br[1], moe_wgu[1], moe_bgu[1], moe_wd[1], moe_bd[1])

    y = _rmsnorm(y, norm_final)
    return (
        y[:n_prompt].reshape(batch, seq, d),
        y[n_prompt:].reshape(dec_batch, dec_seq, d),
        state_p[None],
        state_s[None],
        ckv_all[:n_prompt].reshape(1, batch, seq, kv_rank),
        kpe_rows[:n_prompt].reshape(1, batch, seq, rope_dim),
        ckv_all[n_prompt:].reshape(1, dec_batch, dec_seq, kv_rank),
        kpe_rows[n_prompt:].reshape(1, dec_batch, dec_seq, rope_dim),
    )
```

```python
import functools
import math

import jax
import jax.numpy as jnp
from jax import lax
from jax.experimental import pallas as pl
from jax.experimental.pallas import tpu as pltpu

F32 = jnp.float32
BF16 = jnp.bfloat16

TOP_K = 4
SWIGLU_LIMIT = 7.0
SWIGLU_ALPHA = 1.702
NORM_EPS = 1e-6
GN_EPS = 1e-6
ROPE_BASE = 10000.0
LANES = 128
BF16_ROWS = 16
MXU_DIM = 256

ROW_TILE = 512
RET_CHUNK = 256
MOE_BLOCK = 256
COMBINE_TILE = 128
ATTN_TILE = 512
ATTN_HEADS = 4
DMA_UNROLL = 32
VMEM_LIMIT = 56 * 1024 * 1024

_NT = (((1,), (1,)), ((), ()))


def _cparams(n_axes, vmem=VMEM_LIMIT):
    return pltpu.CompilerParams(dimension_semantics=("arbitrary",) * n_axes, vmem_limit_bytes=vmem)


def _rms(x, g):
    return x * lax.rsqrt(jnp.mean(x * x, axis=-1, keepdims=True) + NORM_EPS) * g


def _norm_matmul_kernel(x_ref, g_ref, w_ref, cos_ref, sin_ref, o_ref, h_ref, *, n_rope_tiles, head_dim):
    j = pl.program_id(1)

    @pl.when(j == 0)
    def _():
        h_ref[...] = _rms(x_ref[...], g_ref[...]).astype(BF16)

    acc = jnp.dot(h_ref[...], w_ref[...], preferred_element_type=F32)
    tn = acc.shape[1]

    if n_rope_tiles:
        @pl.when(j < n_rope_tiles)
        def _():
            cos = cos_ref[...]
            sin = sin_ref[...]
            half = head_dim // 2
            for h in range(tn // head_dim):
                x1 = acc[:, h * head_dim:h * head_dim + half]
                x2 = acc[:, h * head_dim + half:(h + 1) * head_dim]
                o_ref[:, h * head_dim:h * head_dim + half] = (x1 * cos - x2 * sin).astype(o_ref.dtype)
                o_ref[:, h * head_dim + half:(h + 1) * head_dim] = (x1 * sin + x2 * cos).astype(o_ref.dtype)

        @pl.when(j >= n_rope_tiles)
        def _():
            o_ref[...] = acc.astype(o_ref.dtype)
    else:
        o_ref[...] = acc.astype(o_ref.dtype)


def _norm_matmul(x, g, w, cos, sin, *, tn, n_rope_tiles, head_dim, out_dtype, name):
    n, d = x.shape
    nout = w.shape[1]
    tm = ROW_TILE
    return pl.pallas_call(
        functools.partial(_norm_matmul_kernel, n_rope_tiles=n_rope_tiles, head_dim=head_dim),
        grid=(n // tm, nout // tn),
        in_specs=[
            pl.BlockSpec((tm, d), lambda i, j: (i, 0)),
            pl.BlockSpec((1, d), lambda i, j: (0, 0)),
            pl.BlockSpec((d, tn), lambda i, j: (0, j)),
            pl.BlockSpec((tm, cos.shape[1]), lambda i, j: (i, 0)),
            pl.BlockSpec((tm, sin.shape[1]), lambda i, j: (i, 0)),
        ],
        out_specs=pl.BlockSpec((tm, tn), lambda i, j: (i, j)),
        out_shape=jax.ShapeDtypeStruct((n, nout), out_dtype),
        scratch_shapes=[pltpu.VMEM((tm, d), BF16)],
        compiler_params=_cparams(2),
        name=name,
    )(x, g.reshape(1, d), w, cos, sin)


def _group_norm_gate(o, g, gn):
    mu = jnp.mean(o, axis=-1, keepdims=True)
    oc = o - mu
    var = jnp.mean(oc * oc, axis=-1, keepdims=True)
    on = oc * lax.rsqrt(var + GN_EPS) * gn
    gf = g.astype(F32)
    return (gf / (1.0 + jnp.exp(-gf))) * on


def _retention_step(q, k, v, state, lg):
    L = q.shape[0]
    row = lax.broadcasted_iota(jnp.int32, (L, L), 0)
    col = lax.broadcasted_iota(jnp.int32, (L, L), 1)
    diff = (row - col).astype(F32)
    decay = jnp.where(diff >= 0.0, jnp.exp(jnp.maximum(diff, 0.0) * lg), 0.0)
    scores = lax.dot_general(q, k, _NT, preferred_element_type=F32) * decay
    inner = jnp.dot(scores.astype(BF16), v, preferred_element_type=F32)
    idx = lax.broadcasted_iota(jnp.int32, (L, 1), 0).astype(F32)
    q_decay = jnp.exp((idx + 1.0) * lg)
    cross = jnp.dot(q, state.astype(BF16), preferred_element_type=F32) * q_decay
    k_decay = jnp.exp((L - 1.0 - idx) * lg)
    kd = (k.astype(F32) * k_decay).T.astype(BF16)
    chunk_decay = jnp.exp(jnp.full((1, 1), L, F32) * lg)
    new_state = chunk_decay * state + jnp.dot(kd, v, preferred_element_type=F32)
    return inner + cross, new_state


def _ret_prompt_kernel(lg_ref, q_ref, k_ref, v_ref, g_ref, gn_ref, o_ref, s_ref):
    h = pl.program_id(1)
    c = pl.program_id(2)

    @pl.when(c == 0)
    def _():
        s_ref[...] = jnp.zeros_like(s_ref)

    o, new_state = _retention_step(q_ref[...], k_ref[...], v_ref[...], s_ref[0, 0], lg_ref[h])
    s_ref[0, 0] = new_state
    o_ref[...] = _group_norm_gate(o, g_ref[...], gn_ref[...]).astype(o_ref.dtype)


def _ret_prompt(qkvg, gn, lg, *, batch, seq, heads, dk, dv):
    L = min(RET_CHUNK, seq)
    nc = seq // L
    kq = heads
    v0 = 2 * heads * dk // dv
    g0 = v0 + heads
    grid_spec = pltpu.PrefetchScalarGridSpec(
        num_scalar_prefetch=1,
        grid=(batch, heads, nc),
        in_specs=[
            pl.BlockSpec((L, dk), lambda b, h, c, lg: (b * nc + c, h)),
            pl.BlockSpec((L, dk), lambda b, h, c, lg: (b * nc + c, kq + h)),
            pl.BlockSpec((L, dv), lambda b, h, c, lg: (b * nc + c, v0 + h)),
            pl.BlockSpec((L, dv), lambda b, h, c, lg: (b * nc + c, g0 + h)),
            pl.BlockSpec((1, dv), lambda b, h, c, lg: (0, h)),
        ],
        out_specs=[
            pl.BlockSpec((L, dv), lambda b, h, c, lg: (b * nc + c, h)),
            pl.BlockSpec((1, 1, dk, dv), lambda b, h, c, lg: (b, h, 0, 0)),
        ],
    )
    return pl.pallas_call(
        _ret_prompt_kernel,
        grid_spec=grid_spec,
        out_shape=[
            jax.ShapeDtypeStruct((batch * seq, heads * dv), BF16),
            jax.ShapeDtypeStruct((batch, heads, dk, dv), F32),
        ],
        compiler_params=_cparams(3),
        name="ret_prompt",
    )(lg, qkvg, qkvg, qkvg, qkvg, gn.reshape(1, -1))


def _ret_sample_kernel(lg_ref, q_ref, k_ref, v_ref, g_ref, gn_ref, s_in_ref, o_ref, s_out_ref, *, t_len):
    h = pl.program_id(1)
    lg = lg_ref[h]
    nb = s_in_ref.shape[0]
    n = nb * t_len
    q = q_ref[...]
    k = k_ref[...]
    v = v_ref[...]
    g = g_ref[...]
    gn = gn_ref[...]
    row = lax.broadcasted_iota(jnp.int32, (n, n), 0)
    col = lax.broadcasted_iota(jnp.int32, (n, n), 1)
    diff = (row - col).astype(F32)
    keep = (row // t_len == col // t_len) & (row >= col)
    decay = jnp.where(keep, jnp.exp(jnp.maximum(diff, 0.0) * lg), 0.0)
    scores = lax.dot_general(q, k, _NT, preferred_element_type=F32) * decay
    inner = jnp.dot(scores.astype(BF16), v, preferred_element_type=F32)
    idx = (lax.broadcasted_iota(jnp.int32, (n, 1), 0) % t_len).astype(F32)
    q_decay = jnp.exp((idx + 1.0) * lg)
    kd = k.astype(F32) * jnp.exp((t_len - 1.0 - idx) * lg)
    chunk_decay = jnp.exp(jnp.full((1, 1), t_len, F32) * lg)
    group = BF16_ROWS // t_len
    seq_of_row = lax.broadcasted_iota(jnp.int32, (BF16_ROWS, 1), 0) // t_len
    for p in range(nb // group):
        rows = slice(p * BF16_ROWS, (p + 1) * BF16_ROWS)
        q16, kd16, v16 = q[rows], kd[rows], v[rows]
        cross = jnp.zeros((BF16_ROWS, v.shape[1]), F32)
        for j in range(group):
            b = p * group + j
            state = s_in_ref[b, 0]
            mine = seq_of_row == j
            cross = jnp.where(mine, jnp.dot(q16, state.astype(BF16), preferred_element_type=F32), cross)
            kdb = jnp.where(mine, kd16, 0.0).T.astype(BF16)
            s_out_ref[b, 0] = chunk_decay * state + jnp.dot(kdb, v16, preferred_element_type=F32)
        o = inner[rows] + cross * q_decay[rows]
        o_ref[rows, :] = _group_norm_gate(o, g[rows], gn).astype(o_ref.dtype)


def _ret_sample(qkvg, gn, lg, state, *, row0, t_len, heads, dk, dv):
    nbatch = state.shape[0]
    nb = 8
    rows = nb * t_len
    r0 = row0 // rows
    kq = heads
    v0 = 2 * heads * dk // dv
    g0 = v0 + heads
    grid_spec = pltpu.PrefetchScalarGridSpec(
        num_scalar_prefetch=1,
        grid=(nbatch // nb, heads),
        in_specs=[
            pl.BlockSpec((rows, dk), lambda i, h, lg: (r0 + i, h)),
            pl.BlockSpec((rows, dk), lambda i, h, lg: (r0 + i, kq + h)),
            pl.BlockSpec((rows, dv), lambda i, h, lg: (r0 + i, v0 + h)),
            pl.BlockSpec((rows, dv), lambda i, h, lg: (r0 + i, g0 + h)),
            pl.BlockSpec((1, dv), lambda i, h, lg: (0, h)),
            pl.BlockSpec((nb, 1, dk, dv), lambda i, h, lg: (i, h, 0, 0)),
        ],
        out_specs=[
            pl.BlockSpec((rows, dv), lambda i, h, lg: (i, h)),
            pl.BlockSpec((nb, 1, dk, dv), lambda i, h, lg: (i, h, 0, 0)),
        ],
    )
    return pl.pallas_call(
        functools.partial(_ret_sample_kernel, t_len=t_len),
        grid_spec=grid_spec,
        out_shape=[
            jax.ShapeDtypeStruct((nbatch * t_len, heads * dv), BF16),
            jax.ShapeDtypeStruct(state.shape, F32),
        ],
        compiler_params=_cparams(2),
        name="ret_sample",
    )(lg, qkvg, qkvg, qkvg, qkvg, gn.reshape(1, -1), state)


def _matmul_res_kernel(x_ref, w_ref, r_ref, o_ref):
    o_ref[...] = r_ref[...] + jnp.dot(x_ref[...], w_ref[...], preferred_element_type=F32)


def _matmul_res(x, w, res, *, name):
    n, k = x.shape
    d = w.shape[1]
    tm = ROW_TILE
    return pl.pallas_call(
        _matmul_res_kernel,
        grid=(n // tm,),
        in_specs=[
            pl.BlockSpec((tm, k), lambda i: (i, 0)),
            pl.BlockSpec((k, d), lambda i: (0, 0)),
            pl.BlockSpec((tm, d), lambda i: (i, 0)),
        ],
        out_specs=pl.BlockSpec((tm, d), lambda i: (i, 0)),
        out_shape=jax.ShapeDtypeStruct((n, d), F32),
        compiler_params=_cparams(1),
        name=name,
    )(x, w, res)


def _router_kernel(x_ref, g_ref, wh_ref, wl_ref, b_ref, idx_ref, gate_ref, *, n_experts):
    xn = _rms(x_ref[...], g_ref[...])
    hi = xn.astype(BF16)
    lo = (xn - hi.astype(F32)).astype(BF16)
    logits = (jnp.dot(hi, wh_ref[...], preferred_element_type=F32)
              + jnp.dot(lo, wh_ref[...], preferred_element_type=F32)
              + jnp.dot(hi, wl_ref[...], preferred_element_type=F32)) + b_ref[...]
    lane = lax.broadcasted_iota(jnp.int32, logits.shape, 1).astype(F32)
    neg = jnp.float32(-jnp.inf)
    work = jnp.where(lane < n_experts, logits, neg)
    vals, idxs = [], []
    for _ in range(TOP_K):
        m = jnp.max(work, axis=-1, keepdims=True)
        sel = jnp.min(jnp.where(work == m, lane, float(LANES)), axis=-1, keepdims=True)
        vals.append(m)
        idxs.append(sel)
        work = jnp.where(lane == sel, neg, work)
    es = [jnp.exp(v - vals[0]) for v in vals]
    denom = es[0]
    for e in es[1:]:
        denom = denom + e
    idx_out = jnp.zeros(logits.shape, F32)
    gate_out = jnp.zeros(logits.shape, F32)
    for kk in range(TOP_K):
        idx_out = jnp.where(lane == kk, idxs[kk], idx_out)
        gate_out = jnp.where(lane == kk, es[kk] / denom, gate_out)
    idx_ref[...] = idx_out.astype(jnp.int32)
    gate_ref[...] = gate_out


def _router(y, g, wr, br):
    n, d = y.shape
    e = wr.shape[1]
    tm = ROW_TILE
    wr_pad = jnp.zeros((d, LANES), F32).at[:, :e].set(wr)
    wh = wr_pad.astype(BF16)
    wl = (wr_pad - wh.astype(F32)).astype(BF16)
    b_pad = jnp.zeros((1, LANES), F32).at[0, :e].set(br)
    return pl.pallas_call(
        functools.partial(_router_kernel, n_experts=e),
        grid=(n // tm,),
        in_specs=[
            pl.BlockSpec((tm, d), lambda i: (i, 0)),
            pl.BlockSpec((1, d), lambda i: (0, 0)),
            pl.BlockSpec((d, LANES), lambda i: (0, 0)),
            pl.BlockSpec((d, LANES), lambda i: (0, 0)),
            pl.BlockSpec((1, LANES), lambda i: (0, 0)),
        ],
        out_specs=[
            pl.BlockSpec((tm, LANES), lambda i: (i, 0)),
            pl.BlockSpec((tm, LANES), lambda i: (i, 0)),
        ],
        out_shape=[
            jax.ShapeDtypeStruct((n, LANES), jnp.int32),
            jax.ShapeDtypeStruct((n, LANES), F32),
        ],
        compiler_params=_cparams(1),
        name="moe_router",
    )(y, g.reshape(1, d), wh, wl, b_pad)


def _split_gate_up_kernel(w_ref, p_ref, o_ref):
    half = o_ref.shape[1] // 2
    hp = MXU_DIM // 2
    for c in range(o_ref.shape[1] // MXU_DIM):
        y = jnp.dot(w_ref[:, c * MXU_DIM:(c + 1) * MXU_DIM].astype(BF16), p_ref[...], preferred_element_type=F32)
        o_ref[:, c * hp:(c + 1) * hp] = y[:, :hp].astype(BF16)
        o_ref[:, half + c * hp:half + (c + 1) * hp] = y[:, hp:].astype(BF16)


def _split_gate_up(wgu_all, layer):
    _, e, d, w2 = wgu_all.shape
    rows = e * d
    tm = ROW_TILE
    row0 = layer * rows // tm
    src = lax.broadcasted_iota(jnp.int32, (MXU_DIM, MXU_DIM), 0)
    dst = lax.broadcasted_iota(jnp.int32, (MXU_DIM, MXU_DIM), 1)
    perm = (dst == (src % 2) * (MXU_DIM // 2) + src // 2).astype(BF16)
    out = pl.pallas_call(
        _split_gate_up_kernel,
        grid=(rows // tm,),
        in_specs=[pl.BlockSpec((tm, w2), lambda i: (row0 + i, 0)),
                  pl.BlockSpec((MXU_DIM, MXU_DIM), lambda i: (0, 0))],
        out_specs=pl.BlockSpec((tm, w2), lambda i: (i, 0)),
        out_shape=jax.ShapeDtypeStruct((rows, w2), BF16),
        compiler_params=_cparams(1),
        name="moe_split_gate_up",
    )(wgu_all.reshape(-1, w2), perm)
    return out.reshape(e, d, w2)


def _row_copy(src_hbm, dst, sem, src_row, dst_row):
    return pltpu.make_async_copy(src_hbm.at[pl.ds(src_row, 1)], dst.at[pl.ds(dst_row, 1)], sem)


def _expert_kernel(be_ref, nv_ref, tok_ref, y_hbm, g_ref, wgu_ref, bgu_ref, wd_ref, bd_ref, o_ref, xbuf, sem,
                   *, d_expert):
    i = pl.program_id(0)
    nv = nv_ref[0]
    blk = o_ref.shape[0]
    slot = i % 2

    def gather(block, sl, wait):
        def body(r, _):
            cp = _row_copy(y_hbm, xbuf.at[sl], sem.at[sl], tok_ref[block * blk + r], r)
            cp.wait() if wait else cp.start()
            return 0
        lax.fori_loop(0, blk, body, 0, unroll=DMA_UNROLL)

    @pl.when(i == 0)
    def _():
        gather(0, 0, False)

    @pl.when(i < nv)
    def _():
        gather(i, slot, True)
        x = _rms(xbuf[slot], g_ref[...]).astype(BF16)
        nxt = jnp.minimum(i + 1, nv - 1)
        for r in range(blk):
            _row_copy(y_hbm, xbuf.at[1 - slot], sem.at[1 - slot], tok_ref[nxt * blk + r], r).start()
        hgu = jnp.dot(x, wgu_ref[0], preferred_element_type=F32) + bgu_ref[0]
        gate = jnp.minimum(hgu[:, :d_expert], SWIGLU_LIMIT)
        up = jnp.clip(hgu[:, d_expert:], -SWIGLU_LIMIT, SWIGLU_LIMIT)
        act = (up + 1.0) * gate * (1.0 / (1.0 + jnp.exp(-(gate * SWIGLU_ALPHA))))
        o_ref[...] = jnp.dot(act.astype(BF16), wd_ref[0, 0], preferred_element_type=F32) + bd_ref[0]

        @pl.when(i == nv - 1)
        def _():
            gather(i, 1 - slot, True)

    @pl.when(i >= nv)
    def _():
        o_ref[...] = jnp.zeros_like(o_ref)


def _experts(y, g, rows_tok, block_expert, n_valid, wgu, bgu, wd_all, layer, bd):
    n, d = y.shape
    r = rows_tok.shape[0]
    de = wd_all.shape[2]
    blk = MOE_BLOCK
    grid_spec = pltpu.PrefetchScalarGridSpec(
        num_scalar_prefetch=3,
        grid=(r // blk,),
        in_specs=[
            pl.BlockSpec(memory_space=pl.ANY),
            pl.BlockSpec((1, d), lambda i, be, nv, tok: (0, 0)),
            pl.BlockSpec((1, d, 2 * de), lambda i, be, nv, tok: (be[i], 0, 0)),
            pl.BlockSpec((1, 1, 2 * de), lambda i, be, nv, tok: (be[i], 0, 0)),
            pl.BlockSpec((1, 1, de, d), lambda i, be, nv, tok: (layer, be[i], 0, 0)),
            pl.BlockSpec((1, 1, d), lambda i, be, nv, tok: (be[i], 0, 0)),
        ],
        out_specs=pl.BlockSpec((blk, d), lambda i, be, nv, tok: (i, 0)),
        scratch_shapes=[pltpu.VMEM((2, blk, d), F32), pltpu.SemaphoreType.DMA((2,))],
    )
    return pl.pallas_call(
        functools.partial(_expert_kernel, d_expert=de),
        grid_spec=grid_spec,
        out_shape=jax.ShapeDtypeStruct((r, d), F32),
        compiler_params=_cparams(1),
        name="moe_experts",
    )(block_expert, n_valid, rows_tok, y, g.reshape(1, d), wgu, bgu, wd_all, bd)


def _combine_kernel(dest_ref, res_ref, gate_ref, yb_hbm, o_ref, buf, sem):
    i = pl.program_id(0)
    nsteps = pl.num_programs(0)
    tm = o_ref.shape[0]
    slot = i % 2

    def gather(tile, sl, wait):
        def body(t, _):
            for kk in range(TOP_K):
                cp = _row_copy(yb_hbm, buf.at[sl, kk], sem.at[sl], dest_ref[(tile * tm + t) * TOP_K + kk], t)
                cp.wait() if wait else cp.start()
            return 0
        lax.fori_loop(0, tm, body, 0, unroll=DMA_UNROLL // TOP_K)

    @pl.when(i == 0)
    def _():
        gather(0, 0, False)

    @pl.when(i + 1 < nsteps)
    def _():
        gather(i + 1, 1 - slot, False)

    gather(i, slot, True)
    acc = res_ref[...]
    gates = gate_ref[...]
    for kk in range(TOP_K):
        acc = acc + gates[:, kk:kk + 1] * buf[slot, kk]
    o_ref[...] = acc


def _combine(res, gate_pad, dest, yb):
    n, d = res.shape
    tm = COMBINE_TILE
    grid_spec = pltpu.PrefetchScalarGridSpec(
        num_scalar_prefetch=1,
        grid=(n // tm,),
        in_specs=[
            pl.BlockSpec((tm, d), lambda i, dest: (i, 0)),
            pl.BlockSpec((tm, LANES), lambda i, dest: (i, 0)),
            pl.BlockSpec(memory_space=pl.ANY),
        ],
        out_specs=pl.BlockSpec((tm, d), lambda i, dest: (i, 0)),
        scratch_shapes=[pltpu.VMEM((2, TOP_K, tm, d), F32), pltpu.SemaphoreType.DMA((2,))],
    )
    return pl.pallas_call(
        _combine_kernel,
        grid_spec=grid_spec,
        out_shape=jax.ShapeDtypeStruct((n, d), F32),
        compiler_params=_cparams(1),
        name="moe_combine",
    )(dest, res, gate_pad, yb)


def _moe(y, g, wr, br, wgu_all, bgu, wd_all, bd, layer):
    n, d = y.shape
    e = wr.shape[1]
    de = wd_all.shape[2]
    idx_pad, gate_pad = _router(y, g, wr, br)

    a = n * TOP_K
    blk = MOE_BLOCK
    flat_e = idx_pad[:, :TOP_K].reshape(a)
    onehot = (flat_e[:, None] == jnp.arange(e, dtype=jnp.int32)[None, :]).astype(jnp.int32)
    csum = jnp.cumsum(onehot, axis=0)
    counts = csum[-1]
    rank = jnp.take_along_axis(csum, flat_e[:, None], axis=1)[:, 0] - 1
    padded = ((counts + blk - 1) // blk) * blk
    ends = jnp.cumsum(padded)
    dest = ((ends - padded)[flat_e] + rank).astype(jnp.int32)
    n_blocks = -(-a // blk) + e
    r = n_blocks * blk
    flat_tok = jnp.arange(a, dtype=jnp.int32) // TOP_K
    rows_tok = jnp.zeros((r,), jnp.int32).at[dest].set(flat_tok)
    block_start = jnp.arange(n_blocks, dtype=jnp.int32) * blk
    block_expert = jnp.minimum(jnp.sum((ends[None, :] <= block_start[:, None]).astype(jnp.int32), axis=1), e - 1)
    n_valid = (ends[-1:] // blk).astype(jnp.int32)

    wgu_b = _split_gate_up(wgu_all, layer)
    bgu_b = jnp.concatenate([bgu[:, 0::2], bgu[:, 1::2]], axis=-1).reshape(e, 1, 2 * de)
    yb = _experts(y, g, rows_tok, block_expert, n_valid, wgu_b, bgu_b, wd_all, layer, bd.reshape(e, 1, d))
    return _combine(y, gate_pad, dest, yb)


def _rope_lanes(x, c, sa, sb, half):
    n = x.shape[1]
    return x * c + pltpu.roll(x, n - half, 1) * sa + pltpu.roll(x, half, 1) * sb


def _mla_proj_kernel(dqkv_ref, gq_ref, gkv_ref, wuq_ref, wuk_ref, wuv_ref, ones_ref, c_ref, sa_ref, sb_ref,
                     q_ref, k_ref, v_ref, ckv_ref, kpe_ref, *, q_rank, kv_rank, heads, rope_half, scale):
    x = dqkv_ref[...]
    c = c_ref[...]
    sa = sa_ref[...]
    sb = sb_ref[...]
    cq = _rms(x[:, :q_rank], gq_ref[...]).astype(BF16)
    ckv = _rms(x[:, q_rank:q_rank + kv_rank], gkv_ref[...])
    ckv_ref[...] = ckv
    kpe = _rope_lanes(x[:, q_rank + kv_rank:], c, sa, sb, rope_half)
    kpe_ref[...] = kpe
    ckv_b = ckv.astype(BF16)
    q = jnp.dot(cq, wuq_ref[...], preferred_element_type=F32)
    k = jnp.dot(ckv_b, wuk_ref[...], preferred_element_type=F32)
    for h in range(heads):
        cols = slice(h * LANES, (h + 1) * LANES)
        q_ref[:, cols] = (_rope_lanes(q[:, cols], c, sa, sb, rope_half) * scale).astype(BF16)
        k_ref[:, cols] = (k[:, cols] + kpe).astype(BF16)
    v_ref[...] = (jnp.dot(ckv_b, wuv_ref[...], preferred_element_type=F32) + ones_ref[...]).astype(BF16)


def _mla_proj(dqkv, gq, gkv, wuq_p, wuk_p, wuv_p, ones_p, c, sa, sb, *, q_rank, kv_rank, heads, rope_half, scale):
    n, w = dqkv.shape
    tm = ROW_TILE
    hv = wuv_p.shape[1]
    full = lambda shape: pl.BlockSpec(shape, lambda i: (0,) * len(shape))
    rows = lambda width: pl.BlockSpec((tm, width), lambda i: (i, 0))
    return pl.pallas_call(
        functools.partial(_mla_proj_kernel, q_rank=q_rank, kv_rank=kv_rank, heads=heads, rope_half=rope_half,
                          scale=scale),
        grid=(n // tm,),
        in_specs=[rows(w), full((1, q_rank)), full((1, kv_rank)), full(wuq_p.shape), full(wuk_p.shape),
                  full(wuv_p.shape), full((1, hv)), rows(LANES), rows(LANES), rows(LANES)],
        out_specs=[rows(heads * LANES), rows(heads * LANES), rows(hv), rows(kv_rank), rows(LANES)],
        out_shape=[
            jax.ShapeDtypeStruct((n, heads * LANES), BF16),
            jax.ShapeDtypeStruct((n, heads * LANES), BF16),
            jax.ShapeDtypeStruct((n, hv), BF16),
            jax.ShapeDtypeStruct((n, kv_rank), F32),
            jax.ShapeDtypeStruct((n, LANES), F32),
        ],
        compiler_params=_cparams(1),
        name="mla_proj",
    )(dqkv, gq.reshape(1, -1), gkv.reshape(1, -1), wuq_p, wuk_p, wuv_p, ones_p, c, sa, sb)


def _flash_kernel(q_ref, k_ref, v_ref, o_ref, *scratch, tile, heads, v_head):
    m_refs, acc_refs = scratch[:heads], scratch[heads:]
    qi = pl.program_id(2)
    for h in range(heads):
        m_refs[h][...] = jnp.full((tile, LANES), -jnp.inf, F32)
        acc_refs[h][...] = jnp.zeros((tile, LANES), F32)
    nchunk = tile // LANES

    def step(ki, masked):
        start = pl.multiple_of(ki * tile, tile)
        scores = []
        for h in range(heads):
            cols = slice(h * LANES, (h + 1) * LANES)
            s = lax.dot_general(q_ref[:, cols], k_ref[pl.ds(start, tile), cols], _NT, preferred_element_type=F32)
            if masked:
                row = lax.broadcasted_iota(jnp.int32, s.shape, 0)
                col = lax.broadcasted_iota(jnp.int32, s.shape, 1)
                s = jnp.where(col <= row, s, jnp.finfo(F32).min)
            scores.append(s)
        for h in range(heads):
            cols = slice(h * LANES, (h + 1) * LANES)
            chunks = [scores[h][:, c * LANES:(c + 1) * LANES] for c in range(nchunk)]
            part = chunks[0]
            for ch in chunks[1:]:
                part = jnp.maximum(part, ch)
            m_old = m_refs[h][...]
            m_new = jnp.maximum(m_old, jnp.max(part, axis=-1, keepdims=True))
            m_refs[h][...] = m_new
            p = jnp.concatenate([jnp.exp2(ch - m_new) for ch in chunks], axis=1).astype(BF16)
            acc_refs[h][...] = (jnp.exp2(m_old - m_new) * acc_refs[h][...]
                                + jnp.dot(p, v_ref[pl.ds(start, tile), cols], preferred_element_type=F32))

    def body(ki, carry):
        step(ki, False)
        return carry

    lax.fori_loop(0, qi, body, 0)
    step(qi, True)
    lane = lax.broadcasted_iota(jnp.int32, (tile, LANES), 1)
    for j in range(heads // 2):
        even = acc_refs[2 * j][...]
        odd = acc_refs[2 * j + 1][...]
        num = jnp.where(lane < v_head, even, odd)
        den = jnp.where(lane < v_head, pltpu.roll(even, v_head, 1), pltpu.roll(odd, v_head, 1))
        o_ref[:, j * LANES:(j + 1) * LANES] = (num / den).astype(o_ref.dtype)


def _flash(q_cat, k_cat, v_ext, *, batch, seq, heads, v_head):
    tile = min(ATTN_TILE, seq)
    nq = seq // tile
    hg = ATTN_HEADS
    return pl.pallas_call(
        functools.partial(_flash_kernel, tile=tile, heads=hg, v_head=v_head),
        grid=(batch, heads // hg, nq),
        in_specs=[
            pl.BlockSpec((tile, hg * LANES), lambda b, g, qi: (b * nq + qi, g)),
            pl.BlockSpec((seq, hg * LANES), lambda b, g, qi: (b, g)),
            pl.BlockSpec((seq, hg * LANES), lambda b, g, qi: (b, g)),
        ],
        out_specs=pl.BlockSpec((tile, hg * v_head), lambda b, g, qi: (b * nq + qi, g)),
        out_shape=jax.ShapeDtypeStruct((batch * seq, heads * v_head), BF16),
        scratch_shapes=[pltpu.VMEM((tile, LANES), F32)] * (2 * hg),
        compiler_params=_cparams(3),
        name="mla_flash",
    )(q_cat, k_cat, v_ext)


def _qlat_kernel(q_ref, w_ref, o_ref, *, heads, kv_rank):
    for h in range(heads):
        o_ref[:, h * kv_rank:(h + 1) * kv_rank] = jnp.dot(
            q_ref[:, h * LANES:(h + 1) * LANES], w_ref[h], preferred_element_type=F32).astype(o_ref.dtype)


def _qlat(q_cat, wukt_p, *, row0, nrows, heads, kv_rank):
    tm = min(ROW_TILE, nrows)
    r0 = row0 // tm
    return pl.pallas_call(
        functools.partial(_qlat_kernel, heads=heads, kv_rank=kv_rank),
        grid=(nrows // tm,),
        in_specs=[
            pl.BlockSpec((tm, heads * LANES), lambda i: (r0 + i, 0)),
            pl.BlockSpec(wukt_p.shape, lambda i: (0, 0, 0)),
        ],
        out_specs=pl.BlockSpec((tm, heads * kv_rank), lambda i: (i, 0)),
        out_shape=jax.ShapeDtypeStruct((nrows, heads * kv_rank), BF16),
        compiler_params=_cparams(1),
        name="mla_qlat",
    )(q_cat, wukt_p)


def _olat_kernel(x_ref, w_ref, o_ref, *, pairs, kv_rank):
    for p in range(pairs):
        o_ref[:, p * LANES:(p + 1) * LANES] = jnp.dot(
            x_ref[:, p * 2 * kv_rank:(p + 1) * 2 * kv_rank], w_ref[p], preferred_element_type=F32).astype(o_ref.dtype)


def _olat(o_lat, wuv_pair, *, heads, kv_rank):
    n = o_lat.shape[0]
    tm = min(ROW_TILE, n)
    return pl.pallas_call(
        functools.partial(_olat_kernel, pairs=heads // 2, kv_rank=kv_rank),
        grid=(n // tm,),
        in_specs=[
            pl.BlockSpec((tm, heads * kv_rank), lambda i: (i, 0)),
            pl.BlockSpec(wuv_pair.shape, lambda i: (0, 0, 0)),
        ],
        out_specs=pl.BlockSpec((tm, (heads // 2) * LANES), lambda i: (i, 0)),
        out_shape=jax.ShapeDtypeStruct((n, (heads // 2) * LANES), BF16),
        compiler_params=_cparams(1),
        name="mla_olat",
    )(o_lat, wuv_pair)


def _paged_copies(pt_ref, cache_ckv, cache_kpe, ckv_buf, kpe_buf, sem, b, slot, p, page):
    pg = pt_ref[b, p]
    return (
        pltpu.make_async_copy(cache_ckv.at[0, pg], ckv_buf.at[slot, pl.ds(p * page, page)], sem.at[0, slot]),
        pltpu.make_async_copy(cache_kpe.at[0, pg], kpe_buf.at[slot, :, pl.ds(p * page, page)], sem.at[1, slot]),
    )


def _paged_attn_kernel(pt_ref, qlat_ref, qpe_ref, cnew_ref, pnew_ref, cache_ckv, cache_kpe, o_ref,
                       ckv_buf, kpe_buf, sem, *, n_pages, page, heads):
    b = pl.program_id(0)
    nb = pl.num_programs(0)
    slot = b % 2

    def start_all(bb, sl):
        def body(p, _):
            for cp in _paged_copies(pt_ref, cache_ckv, cache_kpe, ckv_buf, kpe_buf, sem, bb, sl, p, page):
                cp.start()
            return 0
        lax.fori_loop(0, n_pages, body, 0)

    @pl.when(b == 0)
    def _():
        start_all(0, 0)

    @pl.when(b + 1 < nb)
    def _():
        start_all(b + 1, 1 - slot)

    def wait_body(p, _):
        for cp in _paged_copies(pt_ref, cache_ckv, cache_kpe, ckv_buf, kpe_buf, sem, b, slot, p, page):
            cp.wait()
        return 0
    lax.fori_loop(0, n_pages, wait_body, 0)

    qlat = qlat_ref[0]
    qpe = qpe_ref[0]
    ck = ckv_buf[slot].astype(BF16)
    kp = kpe_buf[slot].astype(BF16)
    s_past = (lax.dot_general(qlat, ck, _NT, preferred_element_type=F32)
              + jnp.dot(qpe, kp, preferred_element_type=F32))
    cn = cnew_ref[0].astype(BF16)
    pn = pnew_ref[0].astype(BF16)
    s_new = (lax.dot_general(qlat, cn, _NT, preferred_element_type=F32)
             + lax.dot_general(qpe, pn, _NT, preferred_element_type=F32))
    t_of_row = lax.broadcasted_iota(jnp.int32, s_new.shape, 0) // heads
    j = lax.broadcasted_iota(jnp.int32, s_new.shape, 1)
    s_new = jnp.where(j <= t_of_row, s_new, jnp.finfo(F32).min)
    m = jnp.maximum(jnp.max(s_past, axis=-1, keepdims=True), jnp.max(s_new, axis=-1, keepdims=True))
    p_past = jnp.exp2(s_past - m)
    p_new = jnp.exp2(s_new - m)
    denom = jnp.sum(p_past, axis=-1, keepdims=True) + jnp.sum(p_new, axis=-1, keepdims=True)
    o = (jnp.dot(p_past.astype(BF16), ck, preferred_element_type=F32)
         + jnp.dot(p_new.astype(BF16), cn, preferred_element_type=F32))
    o_ref[0] = (o / denom).astype(o_ref.dtype)


def _paged_attn(page_table, qlat, qpe, ckv_new, kpe_new, cache_ckv, cache_kpe, *, heads):
    nbatch, n_pages = page_table.shape
    page, kv_rank = cache_ckv.shape[2], cache_ckv.shape[3]
    rope = cache_kpe.shape[2]
    rows = qlat.shape[1]
    tpad = ckv_new.shape[1]
    grid_spec = pltpu.PrefetchScalarGridSpec(
        num_scalar_prefetch=1,
        grid=(nbatch,),
        in_specs=[
            pl.BlockSpec((1, rows, kv_rank), lambda b, pt: (b, 0, 0)),
            pl.BlockSpec((1, rows, rope), lambda b, pt: (b, 0, 0)),
            pl.BlockSpec((1, tpad, kv_rank), lambda b, pt: (b, 0, 0)),
            pl.BlockSpec((1, tpad, rope), lambda b, pt: (b, 0, 0)),
            pl.BlockSpec(memory_space=pl.ANY),
            pl.BlockSpec(memory_space=pl.ANY),
        ],
        out_specs=pl.BlockSpec((1, rows, kv_rank), lambda b, pt: (b, 0, 0)),
        scratch_shapes=[
            pltpu.VMEM((2, n_pages * page, kv_rank), F32),
            pltpu.VMEM((2, rope, n_pages * page), F32),
            pltpu.SemaphoreType.DMA((2, 2)),
        ],
    )
    return pl.pallas_call(
        functools.partial(_paged_attn_kernel, n_pages=n_pages, page=page, heads=heads),
        grid_spec=grid_spec,
        out_shape=jax.ShapeDtypeStruct((nbatch, rows, kv_rank), BF16),
        compiler_params=_cparams(1),
        name="mla_paged_attn",
    )(page_table, qlat, qpe, ckv_new, kpe_new, cache_ckv, cache_kpe)


def _rmsnorm_kernel(x_ref, g_ref, o_ref):
    o_ref[...] = _rms(x_ref[...], g_ref[...])


def _rmsnorm(x, g):
    n, d = x.shape
    tm = ROW_TILE
    return pl.pallas_call(
        _rmsnorm_kernel,
        grid=(n // tm,),
        in_specs=[pl.BlockSpec((tm, d), lambda i: (i, 0)), pl.BlockSpec((1, d), lambda i: (0, 0))],
        out_specs=pl.BlockSpec((tm, d), lambda i: (i, 0)),
        out_shape=jax.ShapeDtypeStruct((n, d), F32),
        compiler_params=_cparams(1),
        name="final_norm",
    )(x, g.reshape(1, d))


def _rope_angles(pos, half):
    inv = ROPE_BASE ** (-jnp.arange(half, dtype=F32) / half)
    return pos.astype(F32)[:, None] * inv[None, :]


def _token_positions(batch, seq, dec_batch, dec_seq, past_len):
    return jnp.concatenate([jnp.tile(jnp.arange(seq), batch), jnp.tile(past_len + jnp.arange(dec_seq), dec_batch)])


def kernel(x_prompt, x_sample, state_ret, cache_ckv, cache_kpe, page_table, norm_mix, norm_ffn, norm_final, ret_wq, ret_wk, ret_wv, ret_wg, ret_gn, ret_wo, mla_wdq, mla_gq, mla_wuq, mla_wdkv, mla_gkv, mla_wuk, mla_wuv, mla_wo, moe_wr, moe_br, moe_wgu, moe_bgu, moe_wd, moe_bd):
    batch, seq, d = x_prompt.shape
    dec_batch, dec_seq, _ = x_sample.shape
    ret_heads, dk, dv = state_ret.shape[2], state_ret.shape[3], state_ret.shape[4]
    page = cache_ckv.shape[2]
    kv_rank = cache_ckv.shape[3]
    rope_dim = cache_kpe.shape[3]
    past_len = page_table.shape[1] * page
    mla_heads, nope = mla_wuk.shape[2], mla_wuk.shape[3]
    v_head = mla_wuv.shape[3]
    q_rank = mla_wdq.shape[2]
    n_prompt = batch * seq
    n_sample = dec_batch * dec_seq
    assert nope + rope_dim <= LANES and 2 * v_head == LANES and dk == 2 * LANES

    y = jnp.concatenate([x_prompt.reshape(n_prompt, d), x_sample.reshape(n_sample, d)], axis=0)
    pos = _token_positions(batch, seq, dec_batch, dec_seq, past_len)

    ang = _rope_angles(pos, dk // 2)
    w_all = jnp.concatenate([ret_wq[0], ret_wk[0] * (dk ** -0.5), ret_wv[0], ret_wg[0]], axis=1).astype(BF16)
    qkvg = _norm_matmul(y, norm_mix[0], w_all, jnp.cos(ang), jnp.sin(ang), tn=ret_heads * dk,
                        n_rope_tiles=2, head_dim=dk, out_dtype=BF16, name="ret_proj")
    lg = jnp.log(1.0 - 2.0 ** (-5.0 - jnp.arange(ret_heads, dtype=F32)))
    gated_p, state_p = _ret_prompt(qkvg, ret_gn[0], lg, batch=batch, seq=seq, heads=ret_heads, dk=dk, dv=dv)
    gated_s, state_s = _ret_sample(qkvg, ret_gn[0], lg, state_ret[0], row0=n_prompt, t_len=dec_seq,
                                   heads=ret_heads, dk=dk, dv=dv)
    gated = jnp.concatenate([gated_p, gated_s], axis=0)
    y = _matmul_res(gated, ret_wo[0].astype(BF16), y, name="ret_out")
    wd_all = moe_wd.astype(BF16)
    y = _moe(y, norm_ffn[0], moe_wr[0], moe_br[0], moe_wgu, moe_bgu[0], wd_all, moe_bd[0], 0)

    half = rope_dim // 2
    ang = _rope_angles(pos, half)
    cos2 = jnp.concatenate([jnp.cos(ang), jnp.cos(ang)], axis=1)
    sin = jnp.sin(ang)
    n_all = y.shape[0]
    ones = jnp.ones((n_all, LANES), F32)
    zeros = jnp.zeros((n_all, LANES), F32)
    c_tab = lax.dynamic_update_slice(ones, cos2, (0, nope))
    sa_tab = lax.dynamic_update_slice(zeros, -sin, (0, nope))
    sb_tab = lax.dynamic_update_slice(zeros, sin, (0, nope + half))

    wdkv = mla_wdkv[0]
    kpe_cols = jnp.zeros((d, LANES), F32).at[:, nope:nope + rope_dim].set(wdkv[:, kv_rank:])
    w_down = jnp.concatenate([mla_wdq[0], wdkv[:, :kv_rank], kpe_cols], axis=1).astype(BF16)
    dqkv = _norm_matmul(y, norm_mix[1], w_down, zeros, zeros, tn=w_down.shape[1],
                        n_rope_tiles=0, head_dim=0, out_dtype=F32, name="mla_down")

    qk_dim = nope + rope_dim
    wuq_p = jnp.zeros((q_rank, mla_heads, LANES), F32).at[:, :, :qk_dim].set(
        mla_wuq[0].reshape(q_rank, mla_heads, qk_dim)).reshape(q_rank, mla_heads * LANES).astype(BF16)
    wuk_p = jnp.zeros((kv_rank, mla_heads, LANES), F32).at[:, :, :nope].set(mla_wuk[0]).reshape(
        kv_rank, mla_heads * LANES).astype(BF16)
    wuv_pairs = mla_wuv[0].reshape(kv_rank, mla_heads // 2, 2, v_head)
    wuv_p = jnp.zeros((kv_rank, mla_heads // 2, 2, 2, v_head), F32)
    wuv_p = wuv_p.at[:, :, 0, 0].set(wuv_pairs[:, :, 0]).at[:, :, 1, 1].set(wuv_pairs[:, :, 1])
    wuv_p = wuv_p.reshape(kv_rank, mla_heads * LANES).astype(BF16)
    ones_p = jnp.zeros((mla_heads // 2, 2, 2, v_head), F32).at[:, 0, 1].set(1.0).at[:, 1, 0].set(1.0)
    ones_p = ones_p.reshape(1, mla_heads * LANES)
    scale = (float(qk_dim) ** -0.5) * math.log2(math.e)
    q_cat, k_cat, v_ext, ckv_all, kpe_all = _mla_proj(
        dqkv, mla_gq[0], mla_gkv[0], wuq_p, wuk_p, wuv_p, ones_p, c_tab, sa_tab, sb_tab,
        q_rank=q_rank, kv_rank=kv_rank, heads=mla_heads, rope_half=half, scale=scale)
    kpe_rows = kpe_all[:, nope:nope + rope_dim]

    o_p = _flash(q_cat, k_cat, v_ext, batch=batch, seq=seq, heads=mla_heads, v_head=v_head)

    wukt_p = jnp.zeros((mla_heads, LANES, kv_rank), F32).at[:, :nope, :].set(
        jnp.transpose(mla_wuk[0], (1, 2, 0))).astype(BF16)
    q_lat = _qlat(q_cat, wukt_p, row0=n_prompt, nrows=n_sample, heads=mla_heads, kv_rank=kv_rank)
    rows = dec_seq * mla_heads
    q_pe = q_cat[n_prompt:].reshape(n_sample, mla_heads, LANES)[:, :, nope:qk_dim].reshape(dec_batch, rows, rope_dim)
    tpad = BF16_ROWS
    ckv_new = jnp.zeros((dec_batch, tpad, kv_rank), F32).at[:, :dec_seq].set(
        ckv_all[n_prompt:].reshape(dec_batch, dec_seq, kv_rank))
    kpe_new = jnp.zeros((dec_batch, tpad, rope_dim), F32).at[:, :dec_seq].set(
        kpe_rows[n_prompt:].reshape(dec_batch, dec_seq, rope_dim))
    o_lat = _paged_attn(page_table, q_lat.reshape(dec_batch, rows, kv_rank), q_pe, ckv_new, kpe_new,
                        cache_ckv, jnp.swapaxes(cache_kpe, 2, 3), heads=mla_heads)
    wuv_h = jnp.transpose(mla_wuv[0], (1, 0, 2)).reshape(mla_heads // 2, 2, kv_rank, v_head)
    wuv_pair = jnp.zeros((mla_heads // 2, 2, kv_rank, 2, v_head), F32)
    wuv_pair = wuv_pair.at[:, 0, :, 0, :].set(wuv_h[:, 0]).at[:, 1, :, 1, :].set(wuv_h[:, 1])
    wuv_pair = wuv_pair.reshape(mla_heads // 2, 2 * kv_rank, LANES).astype(BF16)
    o_s = _olat(o_lat.reshape(n_sample, mla_heads * kv_rank), wuv_pair, heads=mla_heads, kv_rank=kv_rank)

    o_all = jnp.concatenate([o_p, o_s], axis=0)
    y = _matmul_res(o_all, mla_wo[0].astype(BF16), y, name="mla_out")
    y = _moe(y, norm_ffn[1], moe_wr[1], moe_br[1], moe_wgu, moe_bgu[1], wd_all, moe_bd[1], 1)

    y = _rmsnorm(y, norm_final)
    return (
        y[:n_prompt].reshape(batch, seq, d),
        y[n_prompt:].reshape(dec_batch, dec_seq, d),
        state_p[None],
        state_s[None],
        ckv_all[:n_prompt].reshape(1, batch, seq, kv_rank),
        kpe_rows[:n_prompt].reshape(1, batch, seq, rope_dim),
        ckv_all[n_prompt:].reshape(1, dec_batch, dec_seq, kv_rank),
        kpe_rows[n_prompt:].reshape(1, dec_batch, dec_seq, rope_dim),
    )
```

```python
import functools
import math

import jax
import jax.numpy as jnp
from jax import lax
from jax.experimental import pallas as pl
from jax.experimental.pallas import tpu as pltpu

F32 = jnp.float32
BF16 = jnp.bfloat16

TOP_K = 4
SWIGLU_LIMIT = 7.0
SWIGLU_ALPHA = 1.702
NORM_EPS = 1e-6
GN_EPS = 1e-6
ROPE_BASE = 10000.0
LANES = 128
BF16_ROWS = 16
MXU_DIM = 256

ROW_TILE = 512
RET_CHUNK = 256
MOE_BLOCK = 256
COMBINE_TILE = 128
DISPATCH_TILE = 256
ATTN_TILE = 512
ATTN_HEADS = 4
DMA_UNROLL = 32
VMEM_LIMIT = 56 * 1024 * 1024

_NT = (((1,), (1,)), ((), ()))


def _cparams(n_axes, vmem=VMEM_LIMIT):
    return pltpu.CompilerParams(dimension_semantics=("arbitrary",) * n_axes, vmem_limit_bytes=vmem)


def _rms(x, g):
    return x * lax.rsqrt(jnp.mean(x * x, axis=-1, keepdims=True) + NORM_EPS) * g


def _norm_matmul_kernel(x_ref, g_ref, w_ref, cos_ref, sin_ref, o_ref, h_ref, *, n_rope_tiles, head_dim):
    j = pl.program_id(1)

    @pl.when(j == 0)
    def _():
        h_ref[...] = _rms(x_ref[...], g_ref[...]).astype(BF16)

    acc = jnp.dot(h_ref[...], w_ref[...], preferred_element_type=F32)
    tn = acc.shape[1]

    if n_rope_tiles:
        @pl.when(j < n_rope_tiles)
        def _():
            cos = cos_ref[...]
            sin = sin_ref[...]
            half = head_dim // 2
            for h in range(tn // head_dim):
                x1 = acc[:, h * head_dim:h * head_dim + half]
                x2 = acc[:, h * head_dim + half:(h + 1) * head_dim]
                o_ref[:, h * head_dim:h * head_dim + half] = (x1 * cos - x2 * sin).astype(o_ref.dtype)
                o_ref[:, h * head_dim + half:(h + 1) * head_dim] = (x1 * sin + x2 * cos).astype(o_ref.dtype)

        @pl.when(j >= n_rope_tiles)
        def _():
            o_ref[...] = acc.astype(o_ref.dtype)
    else:
        o_ref[...] = acc.astype(o_ref.dtype)


def _norm_matmul(x, g, w, cos, sin, *, tn, n_rope_tiles, head_dim, out_dtype, name):
    n, d = x.shape
    nout = w.shape[1]
    tm = ROW_TILE
    return pl.pallas_call(
        functools.partial(_norm_matmul_kernel, n_rope_tiles=n_rope_tiles, head_dim=head_dim),
        grid=(n // tm, nout // tn),
        in_specs=[
            pl.BlockSpec((tm, d), lambda i, j: (i, 0)),
            pl.BlockSpec((1, d), lambda i, j: (0, 0)),
            pl.BlockSpec((d, tn), lambda i, j: (0, j)),
            pl.BlockSpec((tm, cos.shape[1]), lambda i, j: (i, 0)),
            pl.BlockSpec((tm, sin.shape[1]), lambda i, j: (i, 0)),
        ],
        out_specs=pl.BlockSpec((tm, tn), lambda i, j: (i, j)),
        out_shape=jax.ShapeDtypeStruct((n, nout), out_dtype),
        scratch_shapes=[pltpu.VMEM((tm, d), BF16)],
        compiler_params=_cparams(2),
        name=name,
    )(x, g.reshape(1, d), w, cos, sin)


def _group_norm_gate(o, g, gn):
    mu = jnp.mean(o, axis=-1, keepdims=True)
    oc = o - mu
    var = jnp.mean(oc * oc, axis=-1, keepdims=True)
    on = oc * lax.rsqrt(var + GN_EPS) * gn
    gf = g.astype(F32)
    return (gf / (1.0 + jnp.exp(-gf))) * on


def _retention_step(q, k, v, state, lg):
    L = q.shape[0]
    row = lax.broadcasted_iota(jnp.int32, (L, L), 0)
    col = lax.broadcasted_iota(jnp.int32, (L, L), 1)
    diff = (row - col).astype(F32)
    decay = jnp.where(diff >= 0.0, jnp.exp(jnp.maximum(diff, 0.0) * lg), 0.0)
    scores = lax.dot_general(q, k, _NT, preferred_element_type=F32) * decay
    inner = jnp.dot(scores.astype(BF16), v, preferred_element_type=F32)
    idx = lax.broadcasted_iota(jnp.int32, (L, 1), 0).astype(F32)
    q_decay = jnp.exp((idx + 1.0) * lg)
    cross = jnp.dot(q, state.astype(BF16), preferred_element_type=F32) * q_decay
    k_decay = jnp.exp((L - 1.0 - idx) * lg)
    kd = (k.astype(F32) * k_decay).T.astype(BF16)
    chunk_decay = jnp.exp(jnp.full((1, 1), L, F32) * lg)
    new_state = chunk_decay * state + jnp.dot(kd, v, preferred_element_type=F32)
    return inner + cross, new_state


def _ret_prompt_kernel(lg_ref, q_ref, k_ref, v_ref, g_ref, gn_ref, o_ref, s_ref):
    h = pl.program_id(1)
    c = pl.program_id(2)

    @pl.when(c == 0)
    def _():
        s_ref[...] = jnp.zeros_like(s_ref)

    o, new_state = _retention_step(q_ref[...], k_ref[...], v_ref[...], s_ref[0, 0], lg_ref[h])
    s_ref[0, 0] = new_state
    o_ref[...] = _group_norm_gate(o, g_ref[...], gn_ref[...]).astype(o_ref.dtype)


def _ret_prompt(qkvg, gn, lg, *, batch, seq, heads, dk, dv):
    L = min(RET_CHUNK, seq)
    nc = seq // L
    kq = heads
    v0 = 2 * heads * dk // dv
    g0 = v0 + heads
    grid_spec = pltpu.PrefetchScalarGridSpec(
        num_scalar_prefetch=1,
        grid=(batch, heads, nc),
        in_specs=[
            pl.BlockSpec((L, dk), lambda b, h, c, lg: (b * nc + c, h)),
            pl.BlockSpec((L, dk), lambda b, h, c, lg: (b * nc + c, kq + h)),
            pl.BlockSpec((L, dv), lambda b, h, c, lg: (b * nc + c, v0 + h)),
            pl.BlockSpec((L, dv), lambda b, h, c, lg: (b * nc + c, g0 + h)),
            pl.BlockSpec((1, dv), lambda b, h, c, lg: (0, h)),
        ],
        out_specs=[
            pl.BlockSpec((L, dv), lambda b, h, c, lg: (b * nc + c, h)),
            pl.BlockSpec((1, 1, dk, dv), lambda b, h, c, lg: (b, h, 0, 0)),
        ],
    )
    return pl.pallas_call(
        _ret_prompt_kernel,
        grid_spec=grid_spec,
        out_shape=[
            jax.ShapeDtypeStruct((batch * seq, heads * dv), BF16),
            jax.ShapeDtypeStruct((batch, heads, dk, dv), F32),
        ],
        compiler_params=_cparams(3),
        name="ret_prompt",
    )(lg, qkvg, qkvg, qkvg, qkvg, gn.reshape(1, -1))


def _ret_sample_kernel(lg_ref, q_ref, k_ref, v_ref, g_ref, gn_ref, s_in_ref, o_ref, s_out_ref, *, t_len):
    h = pl.program_id(1)
    lg = lg_ref[h]
    nb = s_in_ref.shape[0]
    n = nb * t_len
    q = q_ref[...]
    k = k_ref[...]
    v = v_ref[...]
    g = g_ref[...]
    gn = gn_ref[...]
    row = lax.broadcasted_iota(jnp.int32, (n, n), 0)
    col = lax.broadcasted_iota(jnp.int32, (n, n), 1)
    diff = (row - col).astype(F32)
    keep = (row // t_len == col // t_len) & (row >= col)
    decay = jnp.where(keep, jnp.exp(jnp.maximum(diff, 0.0) * lg), 0.0)
    scores = lax.dot_general(q, k, _NT, preferred_element_type=F32) * decay
    inner = jnp.dot(scores.astype(BF16), v, preferred_element_type=F32)
    idx = (lax.broadcasted_iota(jnp.int32, (n, 1), 0) % t_len).astype(F32)
    q_decay = jnp.exp((idx + 1.0) * lg)
    kd = k.astype(F32) * jnp.exp((t_len - 1.0 - idx) * lg)
    chunk_decay = jnp.exp(jnp.full((1, 1), t_len, F32) * lg)
    group = BF16_ROWS // t_len
    seq_of_row = lax.broadcasted_iota(jnp.int32, (BF16_ROWS, 1), 0) // t_len
    for p in range(nb // group):
        rows = slice(p * BF16_ROWS, (p + 1) * BF16_ROWS)
        q16, kd16, v16 = q[rows], kd[rows], v[rows]
        cross = jnp.zeros((BF16_ROWS, v.shape[1]), F32)
        for j in range(group):
            b = p * group + j
            state = s_in_ref[b, 0]
            mine = seq_of_row == j
            cross = jnp.where(mine, jnp.dot(q16, state.astype(BF16), preferred_element_type=F32), cross)
            kdb = jnp.where(mine, kd16, 0.0).T.astype(BF16)
            s_out_ref[b, 0] = chunk_decay * state + jnp.dot(kdb, v16, preferred_element_type=F32)
        o = inner[rows] + cross * q_decay[rows]
        o_ref[rows, :] = _group_norm_gate(o, g[rows], gn).astype(o_ref.dtype)


def _ret_sample(qkvg, gn, lg, state, *, row0, t_len, heads, dk, dv):
    nbatch = state.shape[0]
    nb = 8
    rows = nb * t_len
    r0 = row0 // rows
    kq = heads
    v0 = 2 * heads * dk // dv
    g0 = v0 + heads
    grid_spec = pltpu.PrefetchScalarGridSpec(
        num_scalar_prefetch=1,
        grid=(nbatch // nb, heads),
        in_specs=[
            pl.BlockSpec((rows, dk), lambda i, h, lg: (r0 + i, h)),
            pl.BlockSpec((rows, dk), lambda i, h, lg: (r0 + i, kq + h)),
            pl.BlockSpec((rows, dv), lambda i, h, lg: (r0 + i, v0 + h)),
            pl.BlockSpec((rows, dv), lambda i, h, lg: (r0 + i, g0 + h)),
            pl.BlockSpec((1, dv), lambda i, h, lg: (0, h)),
            pl.BlockSpec((nb, 1, dk, dv), lambda i, h, lg: (i, h, 0, 0)),
        ],
        out_specs=[
            pl.BlockSpec((rows, dv), lambda i, h, lg: (i, h)),
            pl.BlockSpec((nb, 1, dk, dv), lambda i, h, lg: (i, h, 0, 0)),
        ],
    )
    return pl.pallas_call(
        functools.partial(_ret_sample_kernel, t_len=t_len),
        grid_spec=grid_spec,
        out_shape=[
            jax.ShapeDtypeStruct((nbatch * t_len, heads * dv), BF16),
            jax.ShapeDtypeStruct(state.shape, F32),
        ],
        compiler_params=_cparams(2),
        name="ret_sample",
    )(lg, qkvg, qkvg, qkvg, qkvg, gn.reshape(1, -1), state)


def _matmul_res_kernel(x_ref, w_ref, r_ref, o_ref):
    o_ref[...] = r_ref[...] + jnp.dot(x_ref[...], w_ref[...], preferred_element_type=F32)


def _matmul_res(x, w, res, *, name):
    n, k = x.shape
    d = w.shape[1]
    tm = ROW_TILE
    return pl.pallas_call(
        _matmul_res_kernel,
        grid=(n // tm,),
        in_specs=[
            pl.BlockSpec((tm, k), lambda i: (i, 0)),
            pl.BlockSpec((k, d), lambda i: (0, 0)),
            pl.BlockSpec((tm, d), lambda i: (i, 0)),
        ],
        out_specs=pl.BlockSpec((tm, d), lambda i: (i, 0)),
        out_shape=jax.ShapeDtypeStruct((n, d), F32),
        compiler_params=_cparams(1),
        name=name,
    )(x, w, res)


def _router_kernel(x_ref, g_ref, wh_ref, wl_ref, b_ref, idx_ref, gate_ref, *, n_experts):
    xn = _rms(x_ref[...], g_ref[...])
    hi = xn.astype(BF16)
    lo = (xn - hi.astype(F32)).astype(BF16)
    logits = (jnp.dot(hi, wh_ref[...], preferred_element_type=F32)
              + jnp.dot(lo, wh_ref[...], preferred_element_type=F32)
              + jnp.dot(hi, wl_ref[...], preferred_element_type=F32)) + b_ref[...]
    lane = lax.broadcasted_iota(jnp.int32, logits.shape, 1).astype(F32)
    neg = jnp.float32(-jnp.inf)
    work = jnp.where(lane < n_experts, logits, neg)
    vals, idxs = [], []
    for _ in range(TOP_K):
        m = jnp.max(work, axis=-1, keepdims=True)
        sel = jnp.min(jnp.where(work == m, lane, float(LANES)), axis=-1, keepdims=True)
        vals.append(m)
        idxs.append(sel)
        work = jnp.where(lane == sel, neg, work)
    es = [jnp.exp(v - vals[0]) for v in vals]
    denom = es[0]
    for e in es[1:]:
        denom = denom + e
    idx_out = jnp.zeros(logits.shape, F32)
    gate_out = jnp.zeros(logits.shape, F32)
    for kk in range(TOP_K):
        idx_out = jnp.where(lane == kk, idxs[kk], idx_out)
        gate_out = jnp.where(lane == kk, es[kk] / denom, gate_out)
    idx_ref[...] = idx_out.astype(jnp.int32)
    gate_ref[...] = gate_out


def _router(y, g, wr, br):
    n, d = y.shape
    e = wr.shape[1]
    tm = ROW_TILE
    wr_pad = jnp.zeros((d, LANES), F32).at[:, :e].set(wr)
    wh = wr_pad.astype(BF16)
    wl = (wr_pad - wh.astype(F32)).astype(BF16)
    b_pad = jnp.zeros((1, LANES), F32).at[0, :e].set(br)
    return pl.pallas_call(
        functools.partial(_router_kernel, n_experts=e),
        grid=(n // tm,),
        in_specs=[
            pl.BlockSpec((tm, d), lambda i: (i, 0)),
            pl.BlockSpec((1, d), lambda i: (0, 0)),
            pl.BlockSpec((d, LANES), lambda i: (0, 0)),
            pl.BlockSpec((d, LANES), lambda i: (0, 0)),
            pl.BlockSpec((1, LANES), lambda i: (0, 0)),
        ],
        out_specs=[
            pl.BlockSpec((tm, LANES), lambda i: (i, 0)),
            pl.BlockSpec((tm, LANES), lambda i: (i, 0)),
        ],
        out_shape=[
            jax.ShapeDtypeStruct((n, LANES), jnp.int32),
            jax.ShapeDtypeStruct((n, LANES), F32),
        ],
        compiler_params=_cparams(1),
        name="moe_router",
    )(y, g.reshape(1, d), wh, wl, b_pad)


def _split_gate_up_kernel(w_ref, p_ref, o_ref):
    half = o_ref.shape[1] // 2
    hp = MXU_DIM // 2
    for c in range(o_ref.shape[1] // MXU_DIM):
        y = jnp.dot(w_ref[:, c * MXU_DIM:(c + 1) * MXU_DIM].astype(BF16), p_ref[...], preferred_element_type=F32)
        o_ref[:, c * hp:(c + 1) * hp] = y[:, :hp].astype(BF16)
        o_ref[:, half + c * hp:half + (c + 1) * hp] = y[:, hp:].astype(BF16)


def _split_gate_up(wgu_all, layer):
    _, e, d, w2 = wgu_all.shape
    rows = e * d
    tm = ROW_TILE
    row0 = layer * rows // tm
    src = lax.broadcasted_iota(jnp.int32, (MXU_DIM, MXU_DIM), 0)
    dst = lax.broadcasted_iota(jnp.int32, (MXU_DIM, MXU_DIM), 1)
    perm = (dst == (src % 2) * (MXU_DIM // 2) + src // 2).astype(BF16)
    out = pl.pallas_call(
        _split_gate_up_kernel,
        grid=(rows // tm,),
        in_specs=[pl.BlockSpec((tm, w2), lambda i: (row0 + i, 0)),
                  pl.BlockSpec((MXU_DIM, MXU_DIM), lambda i: (0, 0))],
        out_specs=pl.BlockSpec((tm, w2), lambda i: (i, 0)),
        out_shape=jax.ShapeDtypeStruct((rows, w2), BF16),
        compiler_params=_cparams(1),
        name="moe_split_gate_up",
    )(wgu_all.reshape(-1, w2), perm)
    return out.reshape(e, d, w2)


def _row_copy(src, dst, sem, src_row, dst_row):
    return pltpu.make_async_copy(src.at[pl.ds(src_row, 1)], dst.at[pl.ds(dst_row, 1)], sem)


def _dispatch_kernel(dest_ref, lo_ref, hi_ref, y_ref, xb_hbm, zblk, sem, *, n_experts):
    i = pl.program_id(0)
    tm = y_ref.shape[0]
    blk = zblk.shape[0]

    def zero_pads(e, wait):
        def body(r, _):
            cp = _row_copy(zblk, xb_hbm, sem.at[1], 0, r)
            cp.wait() if wait else cp.start()
            return 0
        lax.fori_loop(lo_ref[e], hi_ref[e], body, 0)

    def zero_block(b, _):
        cp = pltpu.make_async_copy(zblk, xb_hbm.at[pl.ds(b * blk, blk)], sem.at[1])
        cp.start()
        cp.wait()
        return 0

    @pl.when(i == 0)
    def _():
        zblk[...] = jnp.zeros_like(zblk)
        for e in range(n_experts):
            zero_pads(e, False)
            zero_pads(e, True)
        lax.fori_loop(hi_ref[n_experts - 1] // blk, xb_hbm.shape[0] // blk, zero_block, 0)

    def scatter(wait):
        def body(t, _):
            for kk in range(TOP_K):
                cp = _row_copy(y_ref, xb_hbm, sem.at[0], t, dest_ref[(i * tm + t) * TOP_K + kk])
                cp.wait() if wait else cp.start()
            return 0
        lax.fori_loop(0, tm, body, 0, unroll=DMA_UNROLL // TOP_K)

    scatter(False)
    scatter(True)


def _dispatch(y, dest, pad_lo, pad_hi, n_rows):
    n, d = y.shape
    tm = DISPATCH_TILE
    grid_spec = pltpu.PrefetchScalarGridSpec(
        num_scalar_prefetch=3,
        grid=(n // tm,),
        in_specs=[pl.BlockSpec((tm, d), lambda i, dest, lo, hi: (i, 0))],
        out_specs=pl.BlockSpec(memory_space=pl.ANY),
        scratch_shapes=[pltpu.VMEM((MOE_BLOCK, d), F32), pltpu.SemaphoreType.DMA((2,))],
    )
    return pl.pallas_call(
        functools.partial(_dispatch_kernel, n_experts=pad_lo.shape[0]),
        grid_spec=grid_spec,
        out_shape=jax.ShapeDtypeStruct((n_rows, d), F32),
        compiler_params=_cparams(1),
        name="moe_dispatch",
    )(dest, pad_lo, pad_hi, y)


def _expert_kernel(be_ref, nv_ref, x_ref, g_ref, wgu_ref, bgu_ref, wd_ref, bd_ref, o_ref, *, d_expert):
    i = pl.program_id(0)

    @pl.when(i < nv_ref[0])
    def _():
        x = _rms(x_ref[...], g_ref[...]).astype(BF16)
        hgu = jnp.dot(x, wgu_ref[0], preferred_element_type=F32) + bgu_ref[0]
        gate = jnp.minimum(hgu[:, :d_expert], SWIGLU_LIMIT)
        up = jnp.clip(hgu[:, d_expert:], -SWIGLU_LIMIT, SWIGLU_LIMIT)
        act = (up + 1.0) * gate * (1.0 / (1.0 + jnp.exp(-(gate * SWIGLU_ALPHA))))
        o_ref[...] = jnp.dot(act.astype(BF16), wd_ref[0, 0], preferred_element_type=F32) + bd_ref[0]

    @pl.when(i >= nv_ref[0])
    def _():
        o_ref[...] = jnp.zeros_like(o_ref)


def _experts(xb, g, block_expert, n_valid, wgu, bgu, wd_all, layer, bd):
    r, d = xb.shape
    de = wd_all.shape[2]
    blk = MOE_BLOCK
    grid_spec = pltpu.PrefetchScalarGridSpec(
        num_scalar_prefetch=2,
        grid=(r // blk,),
        in_specs=[
            pl.BlockSpec((blk, d), lambda i, be, nv: (jnp.minimum(i, nv[0] - 1), 0)),
            pl.BlockSpec((1, d), lambda i, be, nv: (0, 0)),
            pl.BlockSpec((1, d, 2 * de), lambda i, be, nv: (be[i], 0, 0)),
            pl.BlockSpec((1, 1, 2 * de), lambda i, be, nv: (be[i], 0, 0)),
            pl.BlockSpec((1, 1, de, d), lambda i, be, nv: (layer, be[i], 0, 0)),
            pl.BlockSpec((1, 1, d), lambda i, be, nv: (be[i], 0, 0)),
        ],
        out_specs=pl.BlockSpec((blk, d), lambda i, be, nv: (i, 0)),
    )
    return pl.pallas_call(
        functools.partial(_expert_kernel, d_expert=de),
        grid_spec=grid_spec,
        out_shape=jax.ShapeDtypeStruct((r, d), F32),
        compiler_params=_cparams(1),
        name="moe_experts",
    )(block_expert, n_valid, xb, g.reshape(1, d), wgu, bgu, wd_all, bd)


def _combine_kernel(dest_ref, res_ref, gate_ref, yb_hbm, o_ref, buf, sem):
    i = pl.program_id(0)
    nsteps = pl.num_programs(0)
    tm = o_ref.shape[0]
    slot = i % 2

    def gather(tile, sl, wait):
        def body(t, _):
            for kk in range(TOP_K):
                cp = _row_copy(yb_hbm, buf.at[sl, kk], sem.at[sl], dest_ref[(tile * tm + t) * TOP_K + kk], t)
                cp.wait() if wait else cp.start()
            return 0
        lax.fori_loop(0, tm, body, 0, unroll=DMA_UNROLL // TOP_K)

    @pl.when(i == 0)
    def _():
        gather(0, 0, False)

    @pl.when(i + 1 < nsteps)
    def _():
        gather(i + 1, 1 - slot, False)

    gather(i, slot, True)
    acc = res_ref[...]
    gates = gate_ref[...]
    for kk in range(TOP_K):
        acc = acc + gates[:, kk:kk + 1] * buf[slot, kk]
    o_ref[...] = acc


def _combine(res, gate_pad, dest, yb):
    n, d = res.shape
    tm = COMBINE_TILE
    grid_spec = pltpu.PrefetchScalarGridSpec(
        num_scalar_prefetch=1,
        grid=(n // tm,),
        in_specs=[
            pl.BlockSpec((tm, d), lambda i, dest: (i, 0)),
            pl.BlockSpec((tm, LANES), lambda i, dest: (i, 0)),
            pl.BlockSpec(memory_space=pl.ANY),
        ],
        out_specs=pl.BlockSpec((tm, d), lambda i, dest: (i, 0)),
        scratch_shapes=[pltpu.VMEM((2, TOP_K, tm, d), F32), pltpu.SemaphoreType.DMA((2,))],
    )
    return pl.pallas_call(
        _combine_kernel,
        grid_spec=grid_spec,
        out_shape=jax.ShapeDtypeStruct((n, d), F32),
        compiler_params=_cparams(1),
        name="moe_combine",
    )(dest, res, gate_pad, yb)


def _moe(y, g, wr, br, wgu_all, bgu, wd_all, bd, layer):
    n, d = y.shape
    e = wr.shape[1]
    de = wd_all.shape[2]
    idx_pad, gate_pad = _router(y, g, wr, br)

    a = n * TOP_K
    blk = MOE_BLOCK
    flat_e = idx_pad[:, :TOP_K].reshape(a)
    onehot = (flat_e[:, None] == jnp.arange(e, dtype=jnp.int32)[None, :]).astype(jnp.int32)
    csum = jnp.cumsum(onehot, axis=0)
    counts = csum[-1]
    rank = jnp.take_along_axis(csum, flat_e[:, None], axis=1)[:, 0] - 1
    padded = ((counts + blk - 1) // blk) * blk
    ends = jnp.cumsum(padded)
    dest = ((ends - padded)[flat_e] + rank).astype(jnp.int32)
    n_blocks = -(-a // blk) + e
    r = n_blocks * blk
    block_start = jnp.arange(n_blocks, dtype=jnp.int32) * blk
    block_expert = jnp.minimum(jnp.sum((ends[None, :] <= block_start[:, None]).astype(jnp.int32), axis=1), e - 1)
    n_valid = (ends[-1:] // blk).astype(jnp.int32)
    pad_lo = (ends - padded + counts).astype(jnp.int32)

    xb = _dispatch(y, dest, pad_lo, ends.astype(jnp.int32), r)
    wgu_b = _split_gate_up(wgu_all, layer)
    bgu_b = jnp.concatenate([bgu[:, 0::2], bgu[:, 1::2]], axis=-1).reshape(e, 1, 2 * de)
    yb = _experts(xb, g, block_expert, n_valid, wgu_b, bgu_b, wd_all, layer, bd.reshape(e, 1, d))
    return _combine(y, gate_pad, dest, yb)


def _rope_lanes(x, c, sa, sb, half):
    n = x.shape[1]
    return x * c + pltpu.roll(x, n - half, 1) * sa + pltpu.roll(x, half, 1) * sb


def _mla_proj_kernel(dqkv_ref, gq_ref, gkv_ref, wuq_ref, wuk_ref, wuv_ref, ones_ref, c_ref, sa_ref, sb_ref,
                     q_ref, k_ref, v_ref, ckv_ref, kpe_ref, *, q_rank, kv_rank, heads, rope_half, scale):
    x = dqkv_ref[...]
    c = c_ref[...]
    sa = sa_ref[...]
    sb = sb_ref[...]
    cq = _rms(x[:, :q_rank], gq_ref[...]).astype(BF16)
    ckv = _rms(x[:, q_rank:q_rank + kv_rank], gkv_ref[...])
    ckv_ref[...] = ckv
    kpe = _rope_lanes(x[:, q_rank + kv_rank:], c, sa, sb, rope_half)
    kpe_ref[...] = kpe
    ckv_b = ckv.astype(BF16)
    q = jnp.dot(cq, wuq_ref[...], preferred_element_type=F32)
    k = jnp.dot(ckv_b, wuk_ref[...], preferred_element_type=F32)
    for h in range(heads):
        cols = slice(h * LANES, (h + 1) * LANES)
        q_ref[:, cols] = (_rope_lanes(q[:, cols], c, sa, sb, rope_half) * scale).astype(BF16)
        k_ref[:, cols] = (k[:, cols] + kpe).astype(BF16)
    v_ref[...] = (jnp.dot(ckv_b, wuv_ref[...], preferred_element_type=F32) + ones_ref[...]).astype(BF16)


def _mla_proj(dqkv, gq, gkv, wuq_p, wuk_p, wuv_p, ones_p, c, sa, sb, *, q_rank, kv_rank, heads, rope_half, scale):
    n, w = dqkv.shape
    tm = ROW_TILE
    hv = wuv_p.shape[1]
    full = lambda shape: pl.BlockSpec(shape, lambda i: (0,) * len(shape))
    rows = lambda width: pl.BlockSpec((tm, width), lambda i: (i, 0))
    return pl.pallas_call(
        functools.partial(_mla_proj_kernel, q_rank=q_rank, kv_rank=kv_rank, heads=heads, rope_half=rope_half,
                          scale=scale),
        grid=(n // tm,),
        in_specs=[rows(w), full((1, q_rank)), full((1, kv_rank)), full(wuq_p.shape), full(wuk_p.shape),
                  full(wuv_p.shape), full((1, hv)), rows(LANES), rows(LANES), rows(LANES)],
        out_specs=[rows(heads * LANES), rows(heads * LANES), rows(hv), rows(kv_rank), rows(LANES)],
        out_shape=[
            jax.ShapeDtypeStruct((n, heads * LANES), BF16),
            jax.ShapeDtypeStruct((n, heads * LANES), BF16),
            jax.ShapeDtypeStruct((n, hv), BF16),
            jax.ShapeDtypeStruct((n, kv_rank), F32),
            jax.ShapeDtypeStruct((n, LANES), F32),
        ],
        compiler_params=_cparams(1),
        name="mla_proj",
    )(dqkv, gq.reshape(1, -1), gkv.reshape(1, -1), wuq_p, wuk_p, wuv_p, ones_p, c, sa, sb)


def _flash_kernel(q_ref, k_ref, v_ref, o_ref, *scratch, tile, heads, v_head):
    m_refs, acc_refs = scratch[:heads], scratch[heads:]
    qi = pl.program_id(2)
    for h in range(heads):
        m_refs[h][...] = jnp.full((tile, LANES), -jnp.inf, F32)
        acc_refs[h][...] = jnp.zeros((tile, LANES), F32)
    nchunk = tile // LANES

    def step(ki, masked):
        start = pl.multiple_of(ki * tile, tile)
        scores = []
        for h in range(heads):
            cols = slice(h * LANES, (h + 1) * LANES)
            s = lax.dot_general(q_ref[:, cols], k_ref[pl.ds(start, tile), cols], _NT, preferred_element_type=F32)
            if masked:
                row = lax.broadcasted_iota(jnp.int32, s.shape, 0)
                col = lax.broadcasted_iota(jnp.int32, s.shape, 1)
                s = jnp.where(col <= row, s, jnp.finfo(F32).min)
            scores.append(s)
        for h in range(heads):
            cols = slice(h * LANES, (h + 1) * LANES)
            chunks = [scores[h][:, c * LANES:(c + 1) * LANES] for c in range(nchunk)]
            part = chunks[0]
            for ch in chunks[1:]:
                part = jnp.maximum(part, ch)
            m_old = m_refs[h][...]
            m_new = jnp.maximum(m_old, jnp.max(part, axis=-1, keepdims=True))
            m_refs[h][...] = m_new
            p = jnp.concatenate([jnp.exp2(ch - m_new) for ch in chunks], axis=1).astype(BF16)
            acc_refs[h][...] = (jnp.exp2(m_old - m_new) * acc_refs[h][...]
                                + jnp.dot(p, v_ref[pl.ds(start, tile), cols], preferred_element_type=F32))

    def body(ki, carry):
        step(ki, False)
        return carry

    lax.fori_loop(0, qi, body, 0)
    step(qi, True)
    lane = lax.broadcasted_iota(jnp.int32, (tile, LANES), 1)
    for j in range(heads // 2):
        even = acc_refs[2 * j][...]
        odd = acc_refs[2 * j + 1][...]
        num = jnp.where(lane < v_head, even, odd)
        den = jnp.where(lane < v_head, pltpu.roll(even, v_head, 1), pltpu.roll(odd, v_head, 1))
        o_ref[:, j * LANES:(j + 1) * LANES] = (num / den).astype(o_ref.dtype)


def _flash(q_cat, k_cat, v_ext, *, batch, seq, heads, v_head):
    tile = min(ATTN_TILE, seq)
    nq = seq // tile
    hg = ATTN_HEADS
    return pl.pallas_call(
        functools.partial(_flash_kernel, tile=tile, heads=hg, v_head=v_head),
        grid=(batch, heads // hg, nq),
        in_specs=[
            pl.BlockSpec((tile, hg * LANES), lambda b, g, qi: (b * nq + qi, g)),
            pl.BlockSpec((seq, hg * LANES), lambda b, g, qi: (b, g)),
            pl.BlockSpec((seq, hg * LANES), lambda b, g, qi: (b, g)),
        ],
        out_specs=pl.BlockSpec((tile, hg * v_head), lambda b, g, qi: (b * nq + qi, g)),
        out_shape=jax.ShapeDtypeStruct((batch * seq, heads * v_head), BF16),
        scratch_shapes=[pltpu.VMEM((tile, LANES), F32)] * (2 * hg),
        compiler_params=_cparams(3),
        name="mla_flash",
    )(q_cat, k_cat, v_ext)


def _qlat_kernel(q_ref, w_ref, o_ref, *, heads, kv_rank):
    for h in range(heads):
        o_ref[:, h * kv_rank:(h + 1) * kv_rank] = jnp.dot(
            q_ref[:, h * LANES:(h + 1) * LANES], w_ref[h], preferred_element_type=F32).astype(o_ref.dtype)


def _qlat(q_cat, wukt_p, *, row0, nrows, heads, kv_rank):
    tm = min(ROW_TILE, nrows)
    r0 = row0 // tm
    return pl.pallas_call(
        functools.partial(_qlat_kernel, heads=heads, kv_rank=kv_rank),
        grid=(nrows // tm,),
        in_specs=[
            pl.BlockSpec((tm, heads * LANES), lambda i: (r0 + i, 0)),
            pl.BlockSpec(wukt_p.shape, lambda i: (0, 0, 0)),
        ],
        out_specs=pl.BlockSpec((tm, heads * kv_rank), lambda i: (i, 0)),
        out_shape=jax.ShapeDtypeStruct((nrows, heads * kv_rank), BF16),
        compiler_params=_cparams(1),
        name="mla_qlat",
    )(q_cat, wukt_p)


def _olat_kernel(x_ref, w_ref, o_ref, *, pairs, kv_rank):
    for p in range(pairs):
        o_ref[:, p * LANES:(p + 1) * LANES] = jnp.dot(
            x_ref[:, p * 2 * kv_rank:(p + 1) * 2 * kv_rank], w_ref[p], preferred_element_type=F32).astype(o_ref.dtype)


def _olat(o_lat, wuv_pair, *, heads, kv_rank):
    n = o_lat.shape[0]
    tm = min(ROW_TILE, n)
    return pl.pallas_call(
        functools.partial(_olat_kernel, pairs=heads // 2, kv_rank=kv_rank),
        grid=(n // tm,),
        in_specs=[
            pl.BlockSpec((tm, heads * kv_rank), lambda i: (i, 0)),
            pl.BlockSpec(wuv_pair.shape, lambda i: (0, 0, 0)),
        ],
        out_specs=pl.BlockSpec((tm, (heads // 2) * LANES), lambda i: (i, 0)),
        out_shape=jax.ShapeDtypeStruct((n, (heads // 2) * LANES), BF16),
        compiler_params=_cparams(1),
        name="mla_olat",
    )(o_lat, wuv_pair)


def _paged_copies(pt_ref, cache_ckv, cache_kpe, ckv_buf, kpe_buf, sem, b, slot, p, page):
    pg = pt_ref[b, p]
    return (
        pltpu.make_async_copy(cache_ckv.at[0, pg], ckv_buf.at[slot, pl.ds(p * page, page)], sem.at[0, slot]),
        pltpu.make_async_copy(cache_kpe.at[0, pg], kpe_buf.at[slot, :, pl.ds(p * page, page)], sem.at[1, slot]),
    )


def _paged_attn_kernel(pt_ref, qlat_ref, qpe_ref, cnew_ref, pnew_ref, cache_ckv, cache_kpe, o_ref,
                       ckv_buf, kpe_buf, sem, *, n_pages, page, heads):
    b = pl.program_id(0)
    nb = pl.num_programs(0)
    slot = b % 2

    def start_all(bb, sl):
        def body(p, _):
            for cp in _paged_copies(pt_ref, cache_ckv, cache_kpe, ckv_buf, kpe_buf, sem, bb, sl, p, page):
                cp.start()
            return 0
        lax.fori_loop(0, n_pages, body, 0)

    @pl.when(b == 0)
    def _():
        start_all(0, 0)

    @pl.when(b + 1 < nb)
    def _():
        start_all(b + 1, 1 - slot)

    def wait_body(p, _):
        for cp in _paged_copies(pt_ref, cache_ckv, cache_kpe, ckv_buf, kpe_buf, sem, b, slot, p, page):
            cp.wait()
        return 0
    lax.fori_loop(0, n_pages, wait_body, 0)

    qlat = qlat_ref[0]
    qpe = qpe_ref[0]
    ck = ckv_buf[slot].astype(BF16)
    kp = kpe_buf[slot].astype(BF16)
    s_past = (lax.dot_general(qlat, ck, _NT, preferred_element_type=F32)
              + jnp.dot(qpe, kp, preferred_element_type=F32))
    cn = cnew_ref[0].astype(BF16)
    pn = pnew_ref[0].astype(BF16)
    s_new = (lax.dot_general(qlat, cn, _NT, preferred_element_type=F32)
             + lax.dot_general(qpe, pn, _NT, preferred_element_type=F32))
    t_of_row = lax.broadcasted_iota(jnp.int32, s_new.shape, 0) // heads
    j = lax.broadcasted_iota(jnp.int32, s_new.shape, 1)
    s_new = jnp.where(j <= t_of_row, s_new, jnp.finfo(F32).min)
    m = jnp.maximum(jnp.max(s_past, axis=-1, keepdims=True), jnp.max(s_new, axis=-1, keepdims=True))
    p_past = jnp.exp2(s_past - m)
    p_new = jnp.exp2(s_new - m)
    denom = jnp.sum(p_past, axis=-1, keepdims=True) + jnp.sum(p_new, axis=-1, keepdims=True)
    o = (jnp.dot(p_past.astype(BF16), ck, preferred_element_type=F32)
         + jnp.dot(p_new.astype(BF16), cn, preferred_element_type=F32))
    o_ref[0] = (o / denom).astype(o_ref.dtype)


def _paged_attn(page_table, qlat, qpe, ckv_new, kpe_new, cache_ckv, cache_kpe, *, heads):
    nbatch, n_pages = page_table.shape
    page, kv_rank = cache_ckv.shape[2], cache_ckv.shape[3]
    rope = cache_kpe.shape[2]
    rows = qlat.shape[1]
    tpad = ckv_new.shape[1]
    grid_spec = pltpu.PrefetchScalarGridSpec(
        num_scalar_prefetch=1,
        grid=(nbatch,),
        in_specs=[
            pl.BlockSpec((1, rows, kv_rank), lambda b, pt: (b, 0, 0)),
            pl.BlockSpec((1, rows, rope), lambda b, pt: (b, 0, 0)),
            pl.BlockSpec((1, tpad, kv_rank), lambda b, pt: (b, 0, 0)),
            pl.BlockSpec((1, tpad, rope), lambda b, pt: (b, 0, 0)),
            pl.BlockSpec(memory_space=pl.ANY),
            pl.BlockSpec(memory_space=pl.ANY),
        ],
        out_specs=pl.BlockSpec((1, rows, kv_rank), lambda b, pt: (b, 0, 0)),
        scratch_shapes=[
            pltpu.VMEM((2, n_pages * page, kv_rank), F32),
            pltpu.VMEM((2, rope, n_pages * page), F32),
            pltpu.SemaphoreType.DMA((2, 2)),
        ],
    )
    return pl.pallas_call(
        functools.partial(_paged_attn_kernel, n_pages=n_pages, page=page, heads=heads),
        grid_spec=grid_spec,
        out_shape=jax.ShapeDtypeStruct((nbatch, rows, kv_rank), BF16),
        compiler_params=_cparams(1),
        name="mla_paged_attn",
    )(page_table, qlat, qpe, ckv_new, kpe_new, cache_ckv, cache_kpe)


def _rmsnorm_kernel(x_ref, g_ref, o_ref):
    o_ref[...] = _rms(x_ref[...], g_ref[...])


def _rmsnorm(x, g):
    n, d = x.shape
    tm = ROW_TILE
    return pl.pallas_call(
        _rmsnorm_kernel,
        grid=(n // tm,),
        in_specs=[pl.BlockSpec((tm, d), lambda i: (i, 0)), pl.BlockSpec((1, d), lambda i: (0, 0))],
        out_specs=pl.BlockSpec((tm, d), lambda i: (i, 0)),
        out_shape=jax.ShapeDtypeStruct((n, d), F32),
        compiler_params=_cparams(1),
        name="final_norm",
    )(x, g.reshape(1, d))


def _rope_angles(pos, half):
    inv = ROPE_BASE ** (-jnp.arange(half, dtype=F32) / half)
    return pos.astype(F32)[:, None] * inv[None, :]


def _token_positions(batch, seq, dec_batch, dec_seq, past_len):
    return jnp.concatenate([jnp.tile(jnp.arange(seq), batch), jnp.tile(past_len + jnp.arange(dec_seq), dec_batch)])


def kernel(x_prompt, x_sample, state_ret, cache_ckv, cache_kpe, page_table, norm_mix, norm_ffn, norm_final, ret_wq, ret_wk, ret_wv, ret_wg, ret_gn, ret_wo, mla_wdq, mla_gq, mla_wuq, mla_wdkv, mla_gkv, mla_wuk, mla_wuv, mla_wo, moe_wr, moe_br, moe_wgu, moe_bgu, moe_wd, moe_bd):
    batch, seq, d = x_prompt.shape
    dec_batch, dec_seq, _ = x_sample.shape
    ret_heads, dk, dv = state_ret.shape[2], state_ret.shape[3], state_ret.shape[4]
    page = cache_ckv.shape[2]
    kv_rank = cache_ckv.shape[3]
    rope_dim = cache_kpe.shape[3]
    past_len = page_table.shape[1] * page
    mla_heads, nope = mla_wuk.shape[2], mla_wuk.shape[3]
    v_head = mla_wuv.shape[3]
    q_rank = mla_wdq.shape[2]
    n_prompt = batch * seq
    n_sample = dec_batch * dec_seq
    assert nope + rope_dim <= LANES and 2 * v_head == LANES and dk == 2 * LANES

    y = jnp.concatenate([x_prompt.reshape(n_prompt, d), x_sample.reshape(n_sample, d)], axis=0)
    pos = _token_positions(batch, seq, dec_batch, dec_seq, past_len)

    ang = _rope_angles(pos, dk // 2)
    w_all = jnp.concatenate([ret_wq[0], ret_wk[0] * (dk ** -0.5), ret_wv[0], ret_wg[0]], axis=1).astype(BF16)
    qkvg = _norm_matmul(y, norm_mix[0], w_all, jnp.cos(ang), jnp.sin(ang), tn=ret_heads * dk,
                        n_rope_tiles=2, head_dim=dk, out_dtype=BF16, name="ret_proj")
    lg = jnp.log(1.0 - 2.0 ** (-5.0 - jnp.arange(ret_heads, dtype=F32)))
    gated_p, state_p = _ret_prompt(qkvg, ret_gn[0], lg, batch=batch, seq=seq, heads=ret_heads, dk=dk, dv=dv)
    gated_s, state_s = _ret_sample(qkvg, ret_gn[0], lg, state_ret[0], row0=n_prompt, t_len=dec_seq,
                                   heads=ret_heads, dk=dk, dv=dv)
    gated = jnp.concatenate([gated_p, gated_s], axis=0)
    y = _matmul_res(gated, ret_wo[0].astype(BF16), y, name="ret_out")
    wd_all = moe_wd.astype(BF16)
    y = _moe(y, norm_ffn[0], moe_wr[0], moe_br[0], moe_wgu, moe_bgu[0], wd_all, moe_bd[0], 0)

    half = rope_dim // 2
    ang = _rope_angles(pos, half)
    cos2 = jnp.concatenate([jnp.cos(ang), jnp.cos(ang)], axis=1)
    sin = jnp.sin(ang)
    n_all = y.shape[0]
    ones = jnp.ones((n_all, LANES), F32)
    zeros = jnp.zeros((n_all, LANES), F32)
    c_tab = lax.dynamic_update_slice(ones, cos2, (0, nope))
    sa_tab = lax.dynamic_update_slice(zeros, -sin, (0, nope))
    sb_tab = lax.dynamic_update_slice(zeros, sin, (0, nope + half))

    wdkv = mla_wdkv[0]
    kpe_cols = jnp.zeros((d, LANES), F32).at[:, nope:nope + rope_dim].set(wdkv[:, kv_rank:])
    w_down = jnp.concatenate([mla_wdq[0], wdkv[:, :kv_rank], kpe_cols], axis=1).astype(BF16)
    dqkv = _norm_matmul(y, norm_mix[1], w_down, zeros, zeros, tn=w_down.shape[1],
                        n_rope_tiles=0, head_dim=0, out_dtype=F32, name="mla_down")

    qk_dim = nope + rope_dim
    wuq_p = jnp.zeros((q_rank, mla_heads, LANES), F32).at[:, :, :qk_dim].set(
        mla_wuq[0].reshape(q_rank, mla_heads, qk_dim)).reshape(q_rank, mla_heads * LANES).astype(BF16)
    wuk_p = jnp.zeros((kv_rank, mla_heads, LANES), F32).at[:, :, :nope].set(mla_wuk[0]).reshape(
        kv_rank, mla_heads * LANES).astype(BF16)
    wuv_pairs = mla_wuv[0].reshape(kv_rank, mla_heads // 2, 2, v_head)
    wuv_p = jnp.zeros((kv_rank, mla_heads // 2, 2, 2, v_head), F32)
    wuv_p = wuv_p.at[:, :, 0, 0].set(wuv_pairs[:, :, 0]).at[:, :, 1, 1].set(wuv_pairs[:, :, 1])
    wuv_p = wuv_p.reshape(kv_rank, mla_heads * LANES).astype(BF16)
    ones_p = jnp.zeros((mla_heads // 2, 2, 2, v_head), F32).at[:, 0, 1].set(1.0).at[:, 1, 0].set(1.0)
    ones_p = ones_p.reshape(1, mla_heads * LANES)
    scale = (float(qk_dim) ** -0.5) * math.log2(math.e)
    q_cat, k_cat, v_ext, ckv_all, kpe_all = _mla_proj(
        dqkv, mla_gq[0], mla_gkv[0], wuq_p, wuk_p, wuv_p, ones_p, c_tab, sa_tab, sb_tab,
        q_rank=q_rank, kv_rank=kv_rank, heads=mla_heads, rope_half=half, scale=scale)
    kpe_rows = kpe_all[:, nope:nope + rope_dim]

    o_p = _flash(q_cat, k_cat, v_ext, batch=batch, seq=seq, heads=mla_heads, v_head=v_head)

    wukt_p = jnp.zeros((mla_heads, LANES, kv_rank), F32).at[:, :nope, :].set(
        jnp.transpose(mla_wuk[0], (1, 2, 0))).astype(BF16)
    q_lat = _qlat(q_cat, wukt_p, row0=n_prompt, nrows=n_sample, heads=mla_heads, kv_rank=kv_rank)
    rows = dec_seq * mla_heads
    q_pe = q_cat[n_prompt:].reshape(n_sample, mla_heads, LANES)[:, :, nope:qk_dim].reshape(dec_batch, rows, rope_dim)
    tpad = BF16_ROWS
    ckv_new = jnp.zeros((dec_batch, tpad, kv_rank), F32).at[:, :dec_seq].set(
        ckv_all[n_prompt:].reshape(dec_batch, dec_seq, kv_rank))
    kpe_new = jnp.zeros((dec_batch, tpad, rope_dim), F32).at[:, :dec_seq].set(
        kpe_rows[n_prompt:].reshape(dec_batch, dec_seq, rope_dim))
    o_lat = _paged_attn(page_table, q_lat.reshape(dec_batch, rows, kv_rank), q_pe, ckv_new, kpe_new,
                        cache_ckv, jnp.swapaxes(cache_kpe, 2, 3), heads=mla_heads)
    wuv_h = jnp.transpose(mla_wuv[0], (1, 0, 2)).reshape(mla_heads // 2, 2, kv_rank, v_head)
    wuv_pair = jnp.zeros((mla_heads // 2, 2, kv_rank, 2, v_head), F32)
    wuv_pair = wuv_pair.at[:, 0, :, 0, :].set(wuv_h[:, 0]).at[:, 1, :, 1, :].set(wuv_h[:, 1])
    wuv_pair = wuv_pair.reshape(mla_heads // 2, 2 * kv_rank, LANES).astype(BF16)
    o_s = _olat(o_lat.reshape(n_sample, mla_heads * kv_rank), wuv_pair, heads=mla_heads, kv_rank=kv_rank)

    o_all = jnp.concatenate([o_p, o_s], axis=0)
    y = _matmul_res(o_all, mla_wo[0].astype(BF16), y, name="mla_out")
    y = _moe(y, norm_ffn[1], moe_wr[1], moe_br[1], moe_wgu, moe_bgu[1], wd_all, moe_bd[1], 1)

    y = _rmsnorm(y, norm_final)
    return (
        y[:n_prompt].reshape(batch, seq, d),
        y[n_prompt:].reshape(dec_batch, dec_seq, d),
        state_p[None],
        state_s[None],
        ckv_all[:n_prompt].reshape(1, batch, seq, kv_rank),
        kpe_rows[:n_prompt].reshape(1, batch, seq, rope_dim),
        ckv_all[n_prompt:].reshape(1, dec_batch, dec_seq, kv_rank),
        kpe_rows[n_prompt:].reshape(1, dec_batch, dec_seq, rope_dim),
    )
```

```python
import functools
import math

import jax
import jax.numpy as jnp
from jax import lax
from jax.experimental import pallas as pl
from jax.experimental.pallas import tpu as pltpu

F32 = jnp.float32
BF16 = jnp.bfloat16

TOP_K = 4
SWIGLU_LIMIT = 7.0
SWIGLU_ALPHA = 1.702
NORM_EPS = 1e-6
GN_EPS = 1e-6
ROPE_BASE = 10000.0
LANES = 128
BF16_ROWS = 16
MXU_DIM = 256

ROW_TILE = 512
PROJ_ROW_TILE = 1024
RET_CHUNK = 256
MOE_BLOCK = 256
COMBINE_TILE = 128
DISPATCH_TILE = 256
ATTN_TILE = 512
ATTN_HEADS = 4
DMA_UNROLL = 32
VMEM_LIMIT = 56 * 1024 * 1024

_NT = (((1,), (1,)), ((), ()))


def _cparams(n_axes, vmem=VMEM_LIMIT):
    return pltpu.CompilerParams(dimension_semantics=("arbitrary",) * n_axes, vmem_limit_bytes=vmem)


def _rms(x, g):
    return x * lax.rsqrt(jnp.mean(x * x, axis=-1, keepdims=True) + NORM_EPS) * g


def _norm_matmul_kernel(x_ref, g_ref, w_ref, cos_ref, sin_ref, o_ref, h_ref, *, n_rope_tiles, head_dim):
    j = pl.program_id(1)

    @pl.when(j == 0)
    def _():
        h_ref[...] = _rms(x_ref[...], g_ref[...]).astype(BF16)

    acc = jnp.dot(h_ref[...], w_ref[...], preferred_element_type=F32)
    tn = acc.shape[1]

    if n_rope_tiles:
        @pl.when(j < n_rope_tiles)
        def _():
            cos = cos_ref[...]
            sin = sin_ref[...]
            half = head_dim // 2
            for h in range(tn // head_dim):
                x1 = acc[:, h * head_dim:h * head_dim + half]
                x2 = acc[:, h * head_dim + half:(h + 1) * head_dim]
                o_ref[:, h * head_dim:h * head_dim + half] = (x1 * cos - x2 * sin).astype(o_ref.dtype)
                o_ref[:, h * head_dim + half:(h + 1) * head_dim] = (x1 * sin + x2 * cos).astype(o_ref.dtype)

        @pl.when(j >= n_rope_tiles)
        def _():
            o_ref[...] = acc.astype(o_ref.dtype)
    else:
        o_ref[...] = acc.astype(o_ref.dtype)


def _norm_matmul(x, g, w, cos, sin, *, tn, n_rope_tiles, head_dim, out_dtype, name):
    n, d = x.shape
    nout = w.shape[1]
    tm = PROJ_ROW_TILE if n % PROJ_ROW_TILE == 0 else ROW_TILE
    return pl.pallas_call(
        functools.partial(_norm_matmul_kernel, n_rope_tiles=n_rope_tiles, head_dim=head_dim),
        grid=(n // tm, nout // tn),
        in_specs=[
            pl.BlockSpec((tm, d), lambda i, j: (i, 0)),
            pl.BlockSpec((1, d), lambda i, j: (0, 0)),
            pl.BlockSpec((d, tn), lambda i, j: (0, j)),
            pl.BlockSpec((tm, cos.shape[1]), lambda i, j: (i, 0)),
            pl.BlockSpec((tm, sin.shape[1]), lambda i, j: (i, 0)),
        ],
        out_specs=pl.BlockSpec((tm, tn), lambda i, j: (i, j)),
        out_shape=jax.ShapeDtypeStruct((n, nout), out_dtype),
        scratch_shapes=[pltpu.VMEM((tm, d), BF16)],
        compiler_params=_cparams(2),
        name=name,
    )(x, g.reshape(1, d), w, cos, sin)


def _group_norm_gate(o, g, gn):
    mu = jnp.mean(o, axis=-1, keepdims=True)
    oc = o - mu
    var = jnp.mean(oc * oc, axis=-1, keepdims=True)
    on = oc * lax.rsqrt(var + GN_EPS) * gn
    gf = g.astype(F32)
    return (gf / (1.0 + jnp.exp(-gf))) * on


def _retention_step(q, k, v, state, lg):
    L = q.shape[0]
    row = lax.broadcasted_iota(jnp.int32, (L, L), 0)
    col = lax.broadcasted_iota(jnp.int32, (L, L), 1)
    diff = (row - col).astype(F32)
    decay = jnp.where(diff >= 0.0, jnp.exp(jnp.maximum(diff, 0.0) * lg), 0.0)
    scores = lax.dot_general(q, k, _NT, preferred_element_type=F32) * decay
    inner = jnp.dot(scores.astype(BF16), v, preferred_element_type=F32)
    idx = lax.broadcasted_iota(jnp.int32, (L, 1), 0).astype(F32)
    q_decay = jnp.exp((idx + 1.0) * lg)
    cross = jnp.dot(q, state.astype(BF16), preferred_element_type=F32) * q_decay
    k_decay = jnp.exp((L - 1.0 - idx) * lg)
    kd = (k.astype(F32) * k_decay).T.astype(BF16)
    chunk_decay = jnp.exp(jnp.full((1, 1), L, F32) * lg)
    new_state = chunk_decay * state + jnp.dot(kd, v, preferred_element_type=F32)
    return inner + cross, new_state


def _ret_prompt_kernel(lg_ref, q_ref, k_ref, v_ref, g_ref, gn_ref, o_ref, s_ref):
    h = pl.program_id(1)
    c = pl.program_id(2)

    @pl.when(c == 0)
    def _():
        s_ref[...] = jnp.zeros_like(s_ref)

    o, new_state = _retention_step(q_ref[...], k_ref[...], v_ref[...], s_ref[0, 0], lg_ref[h])
    s_ref[0, 0] = new_state
    o_ref[...] = _group_norm_gate(o, g_ref[...], gn_ref[...]).astype(o_ref.dtype)


def _ret_prompt(qkvg, gn, lg, *, batch, seq, heads, dk, dv):
    L = min(RET_CHUNK, seq)
    nc = seq // L
    kq = heads
    v0 = 2 * heads * dk // dv
    g0 = v0 + heads
    grid_spec = pltpu.PrefetchScalarGridSpec(
        num_scalar_prefetch=1,
        grid=(batch, heads, nc),
        in_specs=[
            pl.BlockSpec((L, dk), lambda b, h, c, lg: (b * nc + c, h)),
            pl.BlockSpec((L, dk), lambda b, h, c, lg: (b * nc + c, kq + h)),
            pl.BlockSpec((L, dv), lambda b, h, c, lg: (b * nc + c, v0 + h)),
            pl.BlockSpec((L, dv), lambda b, h, c, lg: (b * nc + c, g0 + h)),
            pl.BlockSpec((1, dv), lambda b, h, c, lg: (0, h)),
        ],
        out_specs=[
            pl.BlockSpec((L, dv), lambda b, h, c, lg: (b * nc + c, h)),
            pl.BlockSpec((1, 1, dk, dv), lambda b, h, c, lg: (b, h, 0, 0)),
        ],
    )
    return pl.pallas_call(
        _ret_prompt_kernel,
        grid_spec=grid_spec,
        out_shape=[
            jax.ShapeDtypeStruct((batch * seq, heads * dv), BF16),
            jax.ShapeDtypeStruct((batch, heads, dk, dv), F32),
        ],
        compiler_params=_cparams(3),
        name="ret_prompt",
    )(lg, qkvg, qkvg, qkvg, qkvg, gn.reshape(1, -1))


def _ret_sample_kernel(lg_ref, q_ref, k_ref, v_ref, g_ref, gn_ref, s_in_ref, o_ref, s_out_ref, *, t_len):
    h = pl.program_id(1)
    lg = lg_ref[h]
    nb = s_in_ref.shape[0]
    n = nb * t_len
    q = q_ref[...]
    k = k_ref[...]
    v = v_ref[...]
    g = g_ref[...]
    gn = gn_ref[...]
    row = lax.broadcasted_iota(jnp.int32, (n, n), 0)
    col = lax.broadcasted_iota(jnp.int32, (n, n), 1)
    diff = (row - col).astype(F32)
    keep = (row // t_len == col // t_len) & (row >= col)
    decay = jnp.where(keep, jnp.exp(jnp.maximum(diff, 0.0) * lg), 0.0)
    scores = lax.dot_general(q, k, _NT, preferred_element_type=F32) * decay
    inner = jnp.dot(scores.astype(BF16), v, preferred_element_type=F32)
    idx = (lax.broadcasted_iota(jnp.int32, (n, 1), 0) % t_len).astype(F32)
    q_decay = jnp.exp((idx + 1.0) * lg)
    kd = k.astype(F32) * jnp.exp((t_len - 1.0 - idx) * lg)
    chunk_decay = jnp.exp(jnp.full((1, 1), t_len, F32) * lg)
    group = BF16_ROWS // t_len
    seq_of_row = lax.broadcasted_iota(jnp.int32, (BF16_ROWS, 1), 0) // t_len
    for p in range(nb // group):
        rows = slice(p * BF16_ROWS, (p + 1) * BF16_ROWS)
        q16, kd16, v16 = q[rows], kd[rows], v[rows]
        cross = jnp.zeros((BF16_ROWS, v.shape[1]), F32)
        for j in range(group):
            b = p * group + j
            state = s_in_ref[b, 0]
            mine = seq_of_row == j
            cross = jnp.where(mine, jnp.dot(q16, state.astype(BF16), preferred_element_type=F32), cross)
            kdb = jnp.where(mine, kd16, 0.0).T.astype(BF16)
            s_out_ref[b, 0] = chunk_decay * state + jnp.dot(kdb, v16, preferred_element_type=F32)
        o = inner[rows] + cross * q_decay[rows]
        o_ref[rows, :] = _group_norm_gate(o, g[rows], gn).astype(o_ref.dtype)


def _ret_sample(qkvg, gn, lg, state, *, row0, t_len, heads, dk, dv):
    nbatch = state.shape[0]
    nb = 8
    rows = nb * t_len
    r0 = row0 // rows
    kq = heads
    v0 = 2 * heads * dk // dv
    g0 = v0 + heads
    grid_spec = pltpu.PrefetchScalarGridSpec(
        num_scalar_prefetch=1,
        grid=(nbatch // nb, heads),
        in_specs=[
            pl.BlockSpec((rows, dk), lambda i, h, lg: (r0 + i, h)),
            pl.BlockSpec((rows, dk), lambda i, h, lg: (r0 + i, kq + h)),
            pl.BlockSpec((rows, dv), lambda i, h, lg: (r0 + i, v0 + h)),
            pl.BlockSpec((rows, dv), lambda i, h, lg: (r0 + i, g0 + h)),
            pl.BlockSpec((1, dv), lambda i, h, lg: (0, h)),
            pl.BlockSpec((nb, 1, dk, dv), lambda i, h, lg: (i, h, 0, 0)),
        ],
        out_specs=[
            pl.BlockSpec((rows, dv), lambda i, h, lg: (i, h)),
            pl.BlockSpec((nb, 1, dk, dv), lambda i, h, lg: (i, h, 0, 0)),
        ],
    )
    return pl.pallas_call(
        functools.partial(_ret_sample_kernel, t_len=t_len),
        grid_spec=grid_spec,
        out_shape=[
            jax.ShapeDtypeStruct((nbatch * t_len, heads * dv), BF16),
            jax.ShapeDtypeStruct(state.shape, F32),
        ],
        compiler_params=_cparams(2),
        name="ret_sample",
    )(lg, qkvg, qkvg, qkvg, qkvg, gn.reshape(1, -1), state)


def _matmul_res_kernel(xp_ref, xs_ref, w_ref, r_ref, o_ref, *, prompt_tiles):
    i = pl.program_id(0)

    @pl.when(i < prompt_tiles)
    def _():
        o_ref[...] = r_ref[...] + jnp.dot(xp_ref[...], w_ref[...], preferred_element_type=F32)

    @pl.when(i >= prompt_tiles)
    def _():
        o_ref[...] = r_ref[...] + jnp.dot(xs_ref[...], w_ref[...], preferred_element_type=F32)


def _matmul_res(x_prompt, x_sample, w, res, *, name):
    n, d = res.shape
    k = w.shape[0]
    tm = ROW_TILE
    pt = x_prompt.shape[0] // tm
    return pl.pallas_call(
        functools.partial(_matmul_res_kernel, prompt_tiles=pt),
        grid=(n // tm,),
        in_specs=[
            pl.BlockSpec((tm, k), lambda i: (jnp.minimum(i, pt - 1), 0)),
            pl.BlockSpec((tm, k), lambda i: (jnp.maximum(i - pt, 0), 0)),
            pl.BlockSpec((k, d), lambda i: (0, 0)),
            pl.BlockSpec((tm, d), lambda i: (i, 0)),
        ],
        out_specs=pl.BlockSpec((tm, d), lambda i: (i, 0)),
        out_shape=jax.ShapeDtypeStruct((n, d), F32),
        compiler_params=_cparams(1),
        name=name,
    )(x_prompt, x_sample, w, res)


def _router_kernel(x_ref, g_ref, wh_ref, wl_ref, b_ref, idx_ref, gate_ref, *, n_experts):
    xn = _rms(x_ref[...], g_ref[...])
    hi = xn.astype(BF16)
    lo = (xn - hi.astype(F32)).astype(BF16)
    logits = (jnp.dot(hi, wh_ref[...], preferred_element_type=F32)
              + jnp.dot(lo, wh_ref[...], preferred_element_type=F32)
              + jnp.dot(hi, wl_ref[...], preferred_element_type=F32)) + b_ref[...]
    lane = lax.broadcasted_iota(jnp.int32, logits.shape, 1).astype(F32)
    neg = jnp.float32(-jnp.inf)
    work = jnp.where(lane < n_experts, logits, neg)
    vals, idxs = [], []
    for _ in range(TOP_K):
        m = jnp.max(work, axis=-1, keepdims=True)
        sel = jnp.min(jnp.where(work == m, lane, float(LANES)), axis=-1, keepdims=True)
        vals.append(m)
        idxs.append(sel)
        work = jnp.where(lane == sel, neg, work)
    es = [jnp.exp(v - vals[0]) for v in vals]
    denom = es[0]
    for e in es[1:]:
        denom = denom + e
    idx_out = jnp.zeros(logits.shape, F32)
    gate_out = jnp.zeros(logits.shape, F32)
    for kk in range(TOP_K):
        idx_out = jnp.where(lane == kk, idxs[kk], idx_out)
        gate_out = jnp.where(lane == kk, es[kk] / denom, gate_out)
    idx_ref[...] = idx_out.astype(jnp.int32)
    gate_ref[...] = gate_out


def _router(y, g, wr, br):
    n, d = y.shape
    e = wr.shape[1]
    tm = ROW_TILE
    wr_pad = jnp.zeros((d, LANES), F32).at[:, :e].set(wr)
    wh = wr_pad.astype(BF16)
    wl = (wr_pad - wh.astype(F32)).astype(BF16)
    b_pad = jnp.zeros((1, LANES), F32).at[0, :e].set(br)
    return pl.pallas_call(
        functools.partial(_router_kernel, n_experts=e),
        grid=(n // tm,),
        in_specs=[
            pl.BlockSpec((tm, d), lambda i: (i, 0)),
            pl.BlockSpec((1, d), lambda i: (0, 0)),
            pl.BlockSpec((d, LANES), lambda i: (0, 0)),
            pl.BlockSpec((d, LANES), lambda i: (0, 0)),
            pl.BlockSpec((1, LANES), lambda i: (0, 0)),
        ],
        out_specs=[
            pl.BlockSpec((tm, LANES), lambda i: (i, 0)),
            pl.BlockSpec((tm, LANES), lambda i: (i, 0)),
        ],
        out_shape=[
            jax.ShapeDtypeStruct((n, LANES), jnp.int32),
            jax.ShapeDtypeStruct((n, LANES), F32),
        ],
        compiler_params=_cparams(1),
        name="moe_router",
    )(y, g.reshape(1, d), wh, wl, b_pad)


def _split_gate_up_kernel(w_ref, p_ref, o_ref):
    half = o_ref.shape[1] // 2
    hp = MXU_DIM // 2
    for c in range(o_ref.shape[1] // MXU_DIM):
        y = jnp.dot(w_ref[:, c * MXU_DIM:(c + 1) * MXU_DIM].astype(BF16), p_ref[...], preferred_element_type=F32)
        o_ref[:, c * hp:(c + 1) * hp] = y[:, :hp].astype(BF16)
        o_ref[:, half + c * hp:half + (c + 1) * hp] = y[:, hp:].astype(BF16)


def _split_gate_up(wgu_all, layer):
    _, e, d, w2 = wgu_all.shape
    rows = e * d
    tm = ROW_TILE
    row0 = layer * rows // tm
    src = lax.broadcasted_iota(jnp.int32, (MXU_DIM, MXU_DIM), 0)
    dst = lax.broadcasted_iota(jnp.int32, (MXU_DIM, MXU_DIM), 1)
    perm = (dst == (src % 2) * (MXU_DIM // 2) + src // 2).astype(BF16)
    out = pl.pallas_call(
        _split_gate_up_kernel,
        grid=(rows // tm,),
        in_specs=[pl.BlockSpec((tm, w2), lambda i: (row0 + i, 0)),
                  pl.BlockSpec((MXU_DIM, MXU_DIM), lambda i: (0, 0))],
        out_specs=pl.BlockSpec((tm, w2), lambda i: (i, 0)),
        out_shape=jax.ShapeDtypeStruct((rows, w2), BF16),
        compiler_params=_cparams(1),
        name="moe_split_gate_up",
    )(wgu_all.reshape(-1, w2), perm)
    return out.reshape(e, d, w2)


def _row_copy(src, dst, sem, src_row, dst_row):
    return pltpu.make_async_copy(src.at[pl.ds(src_row, 1)], dst.at[pl.ds(dst_row, 1)], sem)


def _dispatch_kernel(dest_ref, lo_ref, hi_ref, y_ref, xb_hbm, zblk, sem, *, n_experts):
    i = pl.program_id(0)
    tm = y_ref.shape[0]
    blk = zblk.shape[0]

    def zero_pads(e, wait):
        def body(r, _):
            cp = _row_copy(zblk, xb_hbm, sem.at[1], 0, r)
            cp.wait() if wait else cp.start()
            return 0
        lax.fori_loop(lo_ref[e], hi_ref[e], body, 0)

    def zero_block(b, _):
        cp = pltpu.make_async_copy(zblk, xb_hbm.at[pl.ds(b * blk, blk)], sem.at[1])
        cp.start()
        cp.wait()
        return 0

    @pl.when(i == 0)
    def _():
        zblk[...] = jnp.zeros_like(zblk)
        for e in range(n_experts):
            zero_pads(e, False)
            zero_pads(e, True)
        lax.fori_loop(hi_ref[n_experts - 1] // blk, xb_hbm.shape[0] // blk, zero_block, 0)

    def scatter(wait):
        def body(t, _):
            for kk in range(TOP_K):
                cp = _row_copy(y_ref, xb_hbm, sem.at[0], t, dest_ref[(i * tm + t) * TOP_K + kk])
                cp.wait() if wait else cp.start()
            return 0
        lax.fori_loop(0, tm, body, 0, unroll=DMA_UNROLL // TOP_K)

    scatter(False)
    scatter(True)


def _dispatch(y, dest, pad_lo, pad_hi, n_rows):
    n, d = y.shape
    tm = DISPATCH_TILE
    grid_spec = pltpu.PrefetchScalarGridSpec(
        num_scalar_prefetch=3,
        grid=(n // tm,),
        in_specs=[pl.BlockSpec((tm, d), lambda i, dest, lo, hi: (i, 0))],
        out_specs=pl.BlockSpec(memory_space=pl.ANY),
        scratch_shapes=[pltpu.VMEM((MOE_BLOCK, d), F32), pltpu.SemaphoreType.DMA((2,))],
    )
    return pl.pallas_call(
        functools.partial(_dispatch_kernel, n_experts=pad_lo.shape[0]),
        grid_spec=grid_spec,
        out_shape=jax.ShapeDtypeStruct((n_rows, d), F32),
        compiler_params=_cparams(1),
        name="moe_dispatch",
    )(dest, pad_lo, pad_hi, y)


def _expert_kernel(be_ref, nv_ref, x_ref, g_ref, wgu_ref, bgu_ref, wd_ref, bd_ref, o_ref, *, d_expert):
    i = pl.program_id(0)

    @pl.when(i < nv_ref[0])
    def _():
        x = _rms(x_ref[...], g_ref[...]).astype(BF16)
        hgu = jnp.dot(x, wgu_ref[0], preferred_element_type=F32) + bgu_ref[0]
        gate = jnp.minimum(hgu[:, :d_expert], SWIGLU_LIMIT)
        up = jnp.clip(hgu[:, d_expert:], -SWIGLU_LIMIT, SWIGLU_LIMIT)
        act = (up + 1.0) * gate * (1.0 / (1.0 + jnp.exp(-(gate * SWIGLU_ALPHA))))
        o_ref[...] = jnp.dot(act.astype(BF16), wd_ref[0, 0], preferred_element_type=F32) + bd_ref[0]

    @pl.when(i >= nv_ref[0])
    def _():
        o_ref[...] = jnp.zeros_like(o_ref)


def _experts(xb, g, block_expert, n_valid, wgu, bgu, wd_all, layer, bd):
    r, d = xb.shape
    de = wd_all.shape[2]
    blk = MOE_BLOCK
    grid_spec = pltpu.PrefetchScalarGridSpec(
        num_scalar_prefetch=2,
        grid=(r // blk,),
        in_specs=[
            pl.BlockSpec((blk, d), lambda i, be, nv: (jnp.minimum(i, nv[0] - 1), 0)),
            pl.BlockSpec((1, d), lambda i, be, nv: (0, 0)),
            pl.BlockSpec((1, d, 2 * de), lambda i, be, nv: (be[i], 0, 0)),
            pl.BlockSpec((1, 1, 2 * de), lambda i, be, nv: (be[i], 0, 0)),
            pl.BlockSpec((1, 1, de, d), lambda i, be, nv: (layer, be[i], 0, 0)),
            pl.BlockSpec((1, 1, d), lambda i, be, nv: (be[i], 0, 0)),
        ],
        out_specs=pl.BlockSpec((blk, d), lambda i, be, nv: (i, 0)),
    )
    return pl.pallas_call(
        functools.partial(_expert_kernel, d_expert=de),
        grid_spec=grid_spec,
        out_shape=jax.ShapeDtypeStruct((r, d), F32),
        compiler_params=_cparams(1),
        name="moe_experts",
    )(block_expert, n_valid, xb, g.reshape(1, d), wgu, bgu, wd_all, bd)


def _combine_kernel(dest_ref, res_ref, gate_ref, yb_hbm, o_ref, buf, sem):
    i = pl.program_id(0)
    nsteps = pl.num_programs(0)
    tm = o_ref.shape[0]
    slot = i % 2

    def gather(tile, sl, wait):
        def body(t, _):
            for kk in range(TOP_K):
                cp = _row_copy(yb_hbm, buf.at[sl, kk], sem.at[sl], dest_ref[(tile * tm + t) * TOP_K + kk], t)
                cp.wait() if wait else cp.start()
            return 0
        lax.fori_loop(0, tm, body, 0, unroll=DMA_UNROLL // TOP_K)

    @pl.when(i == 0)
    def _():
        gather(0, 0, False)

    @pl.when(i + 1 < nsteps)
    def _():
        gather(i + 1, 1 - slot, False)

    gather(i, slot, True)
    acc = res_ref[...]
    gates = gate_ref[...]
    for kk in range(TOP_K):
        acc = acc + gates[:, kk:kk + 1] * buf[slot, kk]
    o_ref[...] = acc


def _combine(res, gate_pad, dest, yb):
    n, d = res.shape
    tm = COMBINE_TILE
    grid_spec = pltpu.PrefetchScalarGridSpec(
        num_scalar_prefetch=1,
        grid=(n // tm,),
        in_specs=[
            pl.BlockSpec((tm, d), lambda i, dest: (i, 0)),
            pl.BlockSpec((tm, LANES), lambda i, dest: (i, 0)),
            pl.BlockSpec(memory_space=pl.ANY),
        ],
        out_specs=pl.BlockSpec((tm, d), lambda i, dest: (i, 0)),
        scratch_shapes=[pltpu.VMEM((2, TOP_K, tm, d), F32), pltpu.SemaphoreType.DMA((2,))],
    )
    return pl.pallas_call(
        _combine_kernel,
        grid_spec=grid_spec,
        out_shape=jax.ShapeDtypeStruct((n, d), F32),
        compiler_params=_cparams(1),
        name="moe_combine",
    )(dest, res, gate_pad, yb)


def _moe(y, g, wr, br, wgu_all, bgu, wd_all, bd, layer):
    n, d = y.shape
    e = wr.shape[1]
    de = wd_all.shape[2]
    idx_pad, gate_pad = _router(y, g, wr, br)

    a = n * TOP_K
    blk = MOE_BLOCK
    flat_e = idx_pad[:, :TOP_K].reshape(a)
    onehot = (flat_e[:, None] == jnp.arange(e, dtype=jnp.int32)[None, :]).astype(jnp.int32)
    csum = jnp.cumsum(onehot, axis=0)
    counts = csum[-1]
    rank = jnp.take_along_axis(csum, flat_e[:, None], axis=1)[:, 0] - 1
    padded = ((counts + blk - 1) // blk) * blk
    ends = jnp.cumsum(padded)
    dest = ((ends - padded)[flat_e] + rank).astype(jnp.int32)
    n_blocks = -(-a // blk) + e
    r = n_blocks * blk
    block_start = jnp.arange(n_blocks, dtype=jnp.int32) * blk
    block_expert = jnp.minimum(jnp.sum((ends[None, :] <= block_start[:, None]).astype(jnp.int32), axis=1), e - 1)
    n_valid = (ends[-1:] // blk).astype(jnp.int32)
    pad_lo = (ends - padded + counts).astype(jnp.int32)

    xb = _dispatch(y, dest, pad_lo, ends.astype(jnp.int32), r)
    wgu_b = _split_gate_up(wgu_all, layer)
    bgu_b = jnp.concatenate([bgu[:, 0::2], bgu[:, 1::2]], axis=-1).reshape(e, 1, 2 * de)
    yb = _experts(xb, g, block_expert, n_valid, wgu_b, bgu_b, wd_all, layer, bd.reshape(e, 1, d))
    return _combine(y, gate_pad, dest, yb)


def _rope_lanes(x, c, sa, sb, half):
    n = x.shape[1]
    return x * c + pltpu.roll(x, n - half, 1) * sa + pltpu.roll(x, half, 1) * sb


def _mla_proj_kernel(dqkv_ref, gq_ref, gkv_ref, wuq_ref, wuk_ref, wuv_ref, ones_ref, c_ref, sa_ref, sb_ref,
                     q_ref, k_ref, v_ref, ckv_ref, kpe_ref, *, q_rank, kv_rank, heads, rope_half, scale):
    x = dqkv_ref[...]
    c = c_ref[...]
    sa = sa_ref[...]
    sb = sb_ref[...]
    cq = _rms(x[:, :q_rank], gq_ref[...]).astype(BF16)
    ckv = _rms(x[:, q_rank:q_rank + kv_rank], gkv_ref[...])
    ckv_ref[...] = ckv
    kpe = _rope_lanes(x[:, q_rank + kv_rank:], c, sa, sb, rope_half)
    kpe_ref[...] = kpe
    ckv_b = ckv.astype(BF16)
    q = jnp.dot(cq, wuq_ref[...], preferred_element_type=F32)
    k = jnp.dot(ckv_b, wuk_ref[...], preferred_element_type=F32)
    for h in range(heads):
        cols = slice(h * LANES, (h + 1) * LANES)
        q_ref[:, cols] = (_rope_lanes(q[:, cols], c, sa, sb, rope_half) * scale).astype(BF16)
        k_ref[:, cols] = (k[:, cols] + kpe).astype(BF16)
    v_ref[...] = (jnp.dot(ckv_b, wuv_ref[...], preferred_element_type=F32) + ones_ref[...]).astype(BF16)


def _mla_proj(dqkv, gq, gkv, wuq_p, wuk_p, wuv_p, ones_p, c, sa, sb, *, q_rank, kv_rank, heads, rope_half, scale):
    n, w = dqkv.shape
    tm = ROW_TILE
    hv = wuv_p.shape[1]
    full = lambda shape: pl.BlockSpec(shape, lambda i: (0,) * len(shape))
    rows = lambda width: pl.BlockSpec((tm, width), lambda i: (i, 0))
    return pl.pallas_call(
        functools.partial(_mla_proj_kernel, q_rank=q_rank, kv_rank=kv_rank, heads=heads, rope_half=rope_half,
                          scale=scale),
        grid=(n // tm,),
        in_specs=[rows(w), full((1, q_rank)), full((1, kv_rank)), full(wuq_p.shape), full(wuk_p.shape),
                  full(wuv_p.shape), full((1, hv)), rows(LANES), rows(LANES), rows(LANES)],
        out_specs=[rows(heads * LANES), rows(heads * LANES), rows(hv), rows(kv_rank), rows(LANES)],
        out_shape=[
            jax.ShapeDtypeStruct((n, heads * LANES), BF16),
            jax.ShapeDtypeStruct((n, heads * LANES), BF16),
            jax.ShapeDtypeStruct((n, hv), BF16),
            jax.ShapeDtypeStruct((n, kv_rank), F32),
            jax.ShapeDtypeStruct((n, LANES), F32),
        ],
        compiler_params=_cparams(1),
        name="mla_proj",
    )(dqkv, gq.reshape(1, -1), gkv.reshape(1, -1), wuq_p, wuk_p, wuv_p, ones_p, c, sa, sb)


def _flash_kernel(q_ref, k_ref, v_ref, o_ref, *scratch, tile, heads, v_head):
    m_refs, acc_refs = scratch[:heads], scratch[heads:]
    qi = pl.program_id(2)
    for h in range(heads):
        m_refs[h][...] = jnp.full((tile, LANES), -jnp.inf, F32)
        acc_refs[h][...] = jnp.zeros((tile, LANES), F32)
    nchunk = tile // LANES

    def step(ki, masked):
        start = pl.multiple_of(ki * tile, tile)
        scores = []
        for h in range(heads):
            cols = slice(h * LANES, (h + 1) * LANES)
            s = lax.dot_general(q_ref[:, cols], k_ref[pl.ds(start, tile), cols], _NT, preferred_element_type=F32)
            if masked:
                row = lax.broadcasted_iota(jnp.int32, s.shape, 0)
                col = lax.broadcasted_iota(jnp.int32, s.shape, 1)
                s = jnp.where(col <= row, s, jnp.finfo(F32).min)
            scores.append(s)
        for h in range(heads):
            cols = slice(h * LANES, (h + 1) * LANES)
            chunks = [scores[h][:, c * LANES:(c + 1) * LANES] for c in range(nchunk)]
            part = chunks[0]
            for ch in chunks[1:]:
                part = jnp.maximum(part, ch)
            m_old = m_refs[h][...]
            m_new = jnp.maximum(m_old, jnp.max(part, axis=-1, keepdims=True))
            m_refs[h][...] = m_new
            p = jnp.concatenate([jnp.exp2(ch - m_new) for ch in chunks], axis=1).astype(BF16)
            acc_refs[h][...] = (jnp.exp2(m_old - m_new) * acc_refs[h][...]
                                + jnp.dot(p, v_ref[pl.ds(start, tile), cols], preferred_element_type=F32))

    def body(kp, carry):
        step(2 * kp, False)
        step(2 * kp + 1, False)
        return carry

    lax.fori_loop(0, qi // 2, body, 0)

    @pl.when(qi % 2 == 1)
    def _():
        step(qi - 1, False)

    step(qi, True)
    lane = lax.broadcasted_iota(jnp.int32, (tile, LANES), 1)
    for j in range(heads // 2):
        even = acc_refs[2 * j][...]
        odd = acc_refs[2 * j + 1][...]
        num = jnp.where(lane < v_head, even, odd)
        den = jnp.where(lane < v_head, pltpu.roll(even, v_head, 1), pltpu.roll(odd, v_head, 1))
        o_ref[:, j * LANES:(j + 1) * LANES] = (num / den).astype(o_ref.dtype)


def _flash(q_cat, k_cat, v_ext, *, batch, seq, heads, v_head):
    tile = min(ATTN_TILE, seq)
    nq = seq // tile
    hg = ATTN_HEADS
    return pl.pallas_call(
        functools.partial(_flash_kernel, tile=tile, heads=hg, v_head=v_head),
        grid=(batch, heads // hg, nq),
        in_specs=[
            pl.BlockSpec((tile, hg * LANES), lambda b, g, qi: (b * nq + qi, g)),
            pl.BlockSpec((seq, hg * LANES), lambda b, g, qi: (b, g)),
            pl.BlockSpec((seq, hg * LANES), lambda b, g, qi: (b, g)),
        ],
        out_specs=pl.BlockSpec((tile, hg * v_head), lambda b, g, qi: (b * nq + qi, g)),
        out_shape=jax.ShapeDtypeStruct((batch * seq, heads * v_head), BF16),
        scratch_shapes=[pltpu.VMEM((tile, LANES), F32)] * (2 * hg),
        compiler_params=_cparams(3),
        name="mla_flash",
    )(q_cat, k_cat, v_ext)


def _qlat_kernel(q_ref, w_ref, o_ref, *, heads, kv_rank):
    for h in range(heads):
        o_ref[:, h * kv_rank:(h + 1) * kv_rank] = jnp.dot(
            q_ref[:, h * LANES:(h + 1) * LANES], w_ref[h], preferred_element_type=F32).astype(o_ref.dtype)


def _qlat(q_cat, wukt_p, *, row0, nrows, heads, kv_rank):
    tm = min(ROW_TILE, nrows)
    r0 = row0 // tm
    return pl.pallas_call(
        functools.partial(_qlat_kernel, heads=heads, kv_rank=kv_rank),
        grid=(nrows // tm,),
        in_specs=[
            pl.BlockSpec((tm, heads * LANES), lambda i: (r0 + i, 0)),
            pl.BlockSpec(wukt_p.shape, lambda i: (0, 0, 0)),
        ],
        out_specs=pl.BlockSpec((tm, heads * kv_rank), lambda i: (i, 0)),
        out_shape=jax.ShapeDtypeStruct((nrows, heads * kv_rank), BF16),
        compiler_params=_cparams(1),
        name="mla_qlat",
    )(q_cat, wukt_p)


def _olat_kernel(x_ref, w_ref, o_ref, *, pairs, kv_rank):
    for p in range(pairs):
        o_ref[:, p * LANES:(p + 1) * LANES] = jnp.dot(
            x_ref[:, p * 2 * kv_rank:(p + 1) * 2 * kv_rank], w_ref[p], preferred_element_type=F32).astype(o_ref.dtype)


def _olat(o_lat, wuv_pair, *, heads, kv_rank):
    n = o_lat.shape[0]
    tm = min(ROW_TILE, n)
    return pl.pallas_call(
        functools.partial(_olat_kernel, pairs=heads // 2, kv_rank=kv_rank),
        grid=(n // tm,),
        in_specs=[
            pl.BlockSpec((tm, heads * kv_rank), lambda i: (i, 0)),
            pl.BlockSpec(wuv_pair.shape, lambda i: (0, 0, 0)),
        ],
        out_specs=pl.BlockSpec((tm, (heads // 2) * LANES), lambda i: (i, 0)),
        out_shape=jax.ShapeDtypeStruct((n, (heads // 2) * LANES), BF16),
        compiler_params=_cparams(1),
        name="mla_olat",
    )(o_lat, wuv_pair)


def _paged_copies(pt_ref, cache_ckv, cache_kpe, ckv_buf, kpe_buf, sem, b, slot, p, page):
    pg = pt_ref[b, p]
    return (
        pltpu.make_async_copy(cache_ckv.at[0, pg], ckv_buf.at[slot, pl.ds(p * page, page)], sem.at[0, slot]),
        pltpu.make_async_copy(cache_kpe.at[0, pg], kpe_buf.at[slot, :, pl.ds(p * page, page)], sem.at[1, slot]),
    )


def _paged_attn_kernel(pt_ref, qlat_ref, qpe_ref, cnew_ref, pnew_ref, cache_ckv, cache_kpe, o_ref,
                       ckv_buf, kpe_buf, sem, *, n_pages, page, heads):
    b = pl.program_id(0)
    nb = pl.num_programs(0)
    slot = b % 2

    def start_all(bb, sl):
        def body(p, _):
            for cp in _paged_copies(pt_ref, cache_ckv, cache_kpe, ckv_buf, kpe_buf, sem, bb, sl, p, page):
                cp.start()
            return 0
        lax.fori_loop(0, n_pages, body, 0)

    @pl.when(b == 0)
    def _():
        start_all(0, 0)

    @pl.when(b + 1 < nb)
    def _():
        start_all(b + 1, 1 - slot)

    def wait_body(p, _):
        for cp in _paged_copies(pt_ref, cache_ckv, cache_kpe, ckv_buf, kpe_buf, sem, b, slot, p, page):
            cp.wait()
        return 0
    lax.fori_loop(0, n_pages, wait_body, 0)

    qlat = qlat_ref[0]
    qpe = qpe_ref[0]
    ck = ckv_buf[slot].astype(BF16)
    kp = kpe_buf[slot].astype(BF16)
    s_past = (lax.dot_general(qlat, ck, _NT, preferred_element_type=F32)
              + jnp.dot(qpe, kp, preferred_element_type=F32))
    cn = cnew_ref[0].astype(BF16)
    pn = pnew_ref[0].astype(BF16)
    s_new = (lax.dot_general(qlat, cn, _NT, preferred_element_type=F32)
             + lax.dot_general(qpe, pn, _NT, preferred_element_type=F32))
    t_of_row = lax.broadcasted_iota(jnp.int32, s_new.shape, 0) // heads
    j = lax.broadcasted_iota(jnp.int32, s_new.shape, 1)
    s_new = jnp.where(j <= t_of_row, s_new, jnp.finfo(F32).min)
    m = jnp.maximum(jnp.max(s_past, axis=-1, keepdims=True), jnp.max(s_new, axis=-1, keepdims=True))
    p_past = jnp.exp2(s_past - m)
    p_new = jnp.exp2(s_new - m)
    denom = jnp.sum(p_past, axis=-1, keepdims=True) + jnp.sum(p_new, axis=-1, keepdims=True)
    o = (jnp.dot(p_past.astype(BF16), ck, preferred_element_type=F32)
         + jnp.dot(p_new.astype(BF16), cn, preferred_element_type=F32))
    o_ref[0] = (o / denom).astype(o_ref.dtype)


def _paged_attn(page_table, qlat, qpe, ckv_new, kpe_new, cache_ckv, cache_kpe, *, heads):
    nbatch, n_pages = page_table.shape
    page, kv_rank = cache_ckv.shape[2], cache_ckv.shape[3]
    rope = cache_kpe.shape[2]
    rows = qlat.shape[1]
    tpad = ckv_new.shape[1]
    grid_spec = pltpu.PrefetchScalarGridSpec(
        num_scalar_prefetch=1,
        grid=(nbatch,),
        in_specs=[
            pl.BlockSpec((1, rows, kv_rank), lambda b, pt: (b, 0, 0)),
            pl.BlockSpec((1, rows, rope), lambda b, pt: (b, 0, 0)),
            pl.BlockSpec((1, tpad, kv_rank), lambda b, pt: (b, 0, 0)),
            pl.BlockSpec((1, tpad, rope), lambda b, pt: (b, 0, 0)),
            pl.BlockSpec(memory_space=pl.ANY),
            pl.BlockSpec(memory_space=pl.ANY),
        ],
        out_specs=pl.BlockSpec((1, rows, kv_rank), lambda b, pt: (b, 0, 0)),
        scratch_shapes=[
            pltpu.VMEM((2, n_pages * page, kv_rank), F32),
            pltpu.VMEM((2, rope, n_pages * page), F32),
            pltpu.SemaphoreType.DMA((2, 2)),
        ],
    )
    return pl.pallas_call(
        functools.partial(_paged_attn_kernel, n_pages=n_pages, page=page, heads=heads),
        grid_spec=grid_spec,
        out_shape=jax.ShapeDtypeStruct((nbatch, rows, kv_rank), BF16),
        compiler_params=_cparams(1),
        name="mla_paged_attn",
    )(page_table, qlat, qpe, ckv_new, kpe_new, cache_ckv, cache_kpe)


def _rmsnorm_kernel(x_ref, g_ref, op_ref, os_ref, *, prompt_tiles):
    i = pl.program_id(0)
    y = _rms(x_ref[...], g_ref[...])

    @pl.when(i < prompt_tiles)
    def _():
        op_ref[...] = y

    @pl.when(i >= prompt_tiles)
    def _():
        os_ref[...] = y


def _rmsnorm(x, g, n_prompt):
    n, d = x.shape
    tm = ROW_TILE
    pt = n_prompt // tm
    return pl.pallas_call(
        functools.partial(_rmsnorm_kernel, prompt_tiles=pt),
        grid=(n // tm,),
        in_specs=[pl.BlockSpec((tm, d), lambda i: (i, 0)), pl.BlockSpec((1, d), lambda i: (0, 0))],
        out_specs=[
            pl.BlockSpec((tm, d), lambda i: (jnp.minimum(i, pt - 1), 0)),
            pl.BlockSpec((tm, d), lambda i: (jnp.maximum(i - pt, 0), 0)),
        ],
        out_shape=[jax.ShapeDtypeStruct((n_prompt, d), F32), jax.ShapeDtypeStruct((n - n_prompt, d), F32)],
        compiler_params=_cparams(1),
        name="final_norm",
    )(x, g.reshape(1, d))


def _rope_angles(pos, half):
    inv = ROPE_BASE ** (-jnp.arange(half, dtype=F32) / half)
    return pos.astype(F32)[:, None] * inv[None, :]


def _token_positions(batch, seq, dec_batch, dec_seq, past_len):
    return jnp.concatenate([jnp.tile(jnp.arange(seq), batch), jnp.tile(past_len + jnp.arange(dec_seq), dec_batch)])


def kernel(x_prompt, x_sample, state_ret, cache_ckv, cache_kpe, page_table, norm_mix, norm_ffn, norm_final, ret_wq, ret_wk, ret_wv, ret_wg, ret_gn, ret_wo, mla_wdq, mla_gq, mla_wuq, mla_wdkv, mla_gkv, mla_wuk, mla_wuv, mla_wo, moe_wr, moe_br, moe_wgu, moe_bgu, moe_wd, moe_bd):
    batch, seq, d = x_prompt.shape
    dec_batch, dec_seq, _ = x_sample.shape
    ret_heads, dk, dv = state_ret.shape[2], state_ret.shape[3], state_ret.shape[4]
    page = cache_ckv.shape[2]
    kv_rank = cache_ckv.shape[3]
    rope_dim = cache_kpe.shape[3]
    past_len = page_table.shape[1] * page
    mla_heads, nope = mla_wuk.shape[2], mla_wuk.shape[3]
    v_head = mla_wuv.shape[3]
    q_rank = mla_wdq.shape[2]
    n_prompt = batch * seq
    n_sample = dec_batch * dec_seq
    assert nope + rope_dim <= LANES and 2 * v_head == LANES and dk == 2 * LANES

    y = jnp.concatenate([x_prompt.reshape(n_prompt, d), x_sample.reshape(n_sample, d)], axis=0)
    pos = _token_positions(batch, seq, dec_batch, dec_seq, past_len)

    ang = _rope_angles(pos, dk // 2)
    w_all = jnp.concatenate([ret_wq[0], ret_wk[0] * (dk ** -0.5), ret_wv[0], ret_wg[0]], axis=1).astype(BF16)
    qkvg = _norm_matmul(y, norm_mix[0], w_all, jnp.cos(ang), jnp.sin(ang), tn=ret_heads * dk,
                        n_rope_tiles=2, head_dim=dk, out_dtype=BF16, name="ret_proj")
    lg = jnp.log(1.0 - 2.0 ** (-5.0 - jnp.arange(ret_heads, dtype=F32)))
    gated_p, state_p = _ret_prompt(qkvg, ret_gn[0], lg, batch=batch, seq=seq, heads=ret_heads, dk=dk, dv=dv)
    gated_s, state_s = _ret_sample(qkvg, ret_gn[0], lg, state_ret[0], row0=n_prompt, t_len=dec_seq,
                                   heads=ret_heads, dk=dk, dv=dv)
    y = _matmul_res(gated_p, gated_s, ret_wo[0].astype(BF16), y, name="ret_out")
    wd_all = moe_wd.astype(BF16)
    y = _moe(y, norm_ffn[0], moe_wr[0], moe_br[0], moe_wgu, moe_bgu[0], wd_all, moe_bd[0], 0)

    half = rope_dim // 2
    ang = _rope_angles(pos, half)
    cos2 = jnp.concatenate([jnp.cos(ang), jnp.cos(ang)], axis=1)
    sin = jnp.sin(ang)
    n_all = y.shape[0]
    ones = jnp.ones((n_all, LANES), F32)
    zeros = jnp.zeros((n_all, LANES), F32)
    c_tab = lax.dynamic_update_slice(ones, cos2, (0, nope))
    sa_tab = lax.dynamic_update_slice(zeros, -sin, (0, nope))
    sb_tab = lax.dynamic_update_slice(zeros, sin, (0, nope + half))

    wdkv = mla_wdkv[0]
    kpe_cols = jnp.zeros((d, LANES), F32).at[:, nope:nope + rope_dim].set(wdkv[:, kv_rank:])
    w_down = jnp.concatenate([mla_wdq[0], wdkv[:, :kv_rank], kpe_cols], axis=1).astype(BF16)
    dqkv = _norm_matmul(y, norm_mix[1], w_down, zeros, zeros, tn=w_down.shape[1],
                        n_rope_tiles=0, head_dim=0, out_dtype=F32, name="mla_down")

    qk_dim = nope + rope_dim
    wuq_p = jnp.zeros((q_rank, mla_heads, LANES), F32).at[:, :, :qk_dim].set(
        mla_wuq[0].reshape(q_rank, mla_heads, qk_dim)).reshape(q_rank, mla_heads * LANES).astype(BF16)
    wuk_p = jnp.zeros((kv_rank, mla_heads, LANES), F32).at[:, :, :nope].set(mla_wuk[0]).reshape(
        kv_rank, mla_heads * LANES).astype(BF16)
    wuv_pairs = mla_wuv[0].reshape(kv_rank, mla_heads // 2, 2, v_head)
    wuv_p = jnp.zeros((kv_rank, mla_heads // 2, 2, 2, v_head), F32)
    wuv_p = wuv_p.at[:, :, 0, 0].set(wuv_pairs[:, :, 0]).at[:, :, 1, 1].set(wuv_pairs[:, :, 1])
    wuv_p = wuv_p.reshape(kv_rank, mla_heads * LANES).astype(BF16)
    ones_p = jnp.zeros((mla_heads // 2, 2, 2, v_head), F32).at[:, 0, 1].set(1.0).at[:, 1, 0].set(1.0)
    ones_p = ones_p.reshape(1, mla_heads * LANES)
    scale = (float(qk_dim) ** -0.5) * math.log2(math.e)
    q_cat, k_cat, v_ext, ckv_all, kpe_all = _mla_proj(
        dqkv, mla_gq[0], mla_gkv[0], wuq_p, wuk_p, wuv_p, ones_p, c_tab, sa_tab, sb_tab,
        q_rank=q_rank, kv_rank=kv_rank, heads=mla_heads, rope_half=half, scale=scale)
    kpe_rows = kpe_all[:, nope:nope + rope_dim]

    o_p = _flash(q_cat, k_cat, v_ext, batch=batch, seq=seq, heads=mla_heads, v_head=v_head)

    wukt_p = jnp.zeros((mla_heads, LANES, kv_rank), F32).at[:, :nope, :].set(
        jnp.transpose(mla_wuk[0], (1, 2, 0))).astype(BF16)
    q_lat = _qlat(q_cat, wukt_p, row0=n_prompt, nrows=n_sample, heads=mla_heads, kv_rank=kv_rank)
    rows = dec_seq * mla_heads
    q_pe = q_cat[n_prompt:].reshape(n_sample, mla_heads, LANES)[:, :, nope:qk_dim].reshape(dec_batch, rows, rope_dim)
    tpad = BF16_ROWS
    ckv_new = jnp.zeros((dec_batch, tpad, kv_rank), F32).at[:, :dec_seq].set(
        ckv_all[n_prompt:].reshape(dec_batch, dec_seq, kv_rank))
    kpe_new = jnp.zeros((dec_batch, tpad, rope_dim), F32).at[:, :dec_seq].set(
        kpe_rows[n_prompt:].reshape(dec_batch, dec_seq, rope_dim))
    o_lat = _paged_attn(page_table, q_lat.reshape(dec_batch, rows, kv_rank), q_pe, ckv_new, kpe_new,
                        cache_ckv, jnp.swapaxes(cache_kpe, 2, 3), heads=mla_heads)
    wuv_h = jnp.transpose(mla_wuv[0], (1, 0, 2)).reshape(mla_heads // 2, 2, kv_rank, v_head)
    wuv_pair = jnp.zeros((mla_heads // 2, 2, kv_rank, 2, v_head), F32)
    wuv_pair = wuv_pair.at[:, 0, :, 0, :].set(wuv_h[:, 0]).at[:, 1, :, 1, :].set(wuv_h[:, 1])
    wuv_pair = wuv_pair.reshape(mla_heads // 2, 2 * kv_rank, LANES).astype(BF16)
    o_s = _olat(o_lat.reshape(n_sample, mla_heads * kv_rank), wuv_pair, heads=mla_heads, kv_rank=kv_rank)

    y = _matmul_res(o_p, o_s, mla_wo[0].astype(BF16), y, name="mla_out")
    y = _moe(y, norm_ffn[1], moe_wr[1], moe_br[1], moe_wgu, moe_bgu[1], wd_all, moe_bd[1], 1)

    y_p, y_s = _rmsnorm(y, norm_final, n_prompt)
    return (
        y_p.reshape(batch, seq, d),
        y_s.reshape(dec_batch, dec_seq, d),
        state_p[None],
        state_s[None],
        ckv_all[:n_prompt].reshape(1, batch, seq, kv_rank),
        kpe_rows[:n_prompt].reshape(1, batch, seq, rope_dim),
        ckv_all[n_prompt:].reshape(1, dec_batch, dec_seq, kv_rank),
        kpe_rows[n_prompt:].reshape(1, dec_batch, dec_seq, rope_dim),
    )
```

```python
import functools
import math

import jax
import jax.numpy as jnp
from jax import lax
from jax.experimental import pallas as pl
from jax.experimental.pallas import tpu as pltpu

F32 = jnp.float32
BF16 = jnp.bfloat16

TOP_K = 4
SWIGLU_LIMIT = 7.0
SWIGLU_ALPHA = 1.702
NORM_EPS = 1e-6
GN_EPS = 1e-6
ROPE_BASE = 10000.0
LANES = 128
BF16_ROWS = 16
MXU_DIM = 256

ROW_TILE = 512
PROJ_ROW_TILE = 1024
RET_CHUNK = 256
RET_HEADS_PER_STEP = 2
MOE_BLOCK = 256
COMBINE_TILE = 128
DISPATCH_TILE = 256
ATTN_TILE = 512
ATTN_HEADS = 4
DMA_UNROLL = 32
VMEM_LIMIT = 56 * 1024 * 1024

_NT = (((1,), (1,)), ((), ()))


def _cparams(n_axes, vmem=VMEM_LIMIT):
    return pltpu.CompilerParams(dimension_semantics=("arbitrary",) * n_axes, vmem_limit_bytes=vmem)


def _rms(x, g):
    return x * lax.rsqrt(jnp.mean(x * x, axis=-1, keepdims=True) + NORM_EPS) * g


def _norm_matmul_kernel(x_ref, g_ref, w_ref, cos_ref, sin_ref, o_ref, h_ref, *, n_rope_tiles, head_dim):
    j = pl.program_id(1)

    @pl.when(j == 0)
    def _():
        h_ref[...] = _rms(x_ref[...], g_ref[...]).astype(BF16)

    acc = jnp.dot(h_ref[...], w_ref[...], preferred_element_type=F32)
    tn = acc.shape[1]

    if n_rope_tiles:
        @pl.when(j < n_rope_tiles)
        def _():
            cos = cos_ref[...]
            sin = sin_ref[...]
            half = head_dim // 2
            for h in range(tn // head_dim):
                x1 = acc[:, h * head_dim:h * head_dim + half]
                x2 = acc[:, h * head_dim + half:(h + 1) * head_dim]
                o_ref[:, h * head_dim:h * head_dim + half] = (x1 * cos - x2 * sin).astype(o_ref.dtype)
                o_ref[:, h * head_dim + half:(h + 1) * head_dim] = (x1 * sin + x2 * cos).astype(o_ref.dtype)

        @pl.when(j >= n_rope_tiles)
        def _():
            o_ref[...] = acc.astype(o_ref.dtype)
    else:
        o_ref[...] = acc.astype(o_ref.dtype)


def _proj_tile(n):
    return PROJ_ROW_TILE if n % PROJ_ROW_TILE == 0 else ROW_TILE


def _table_tile(i, prompt_tiles, seq_tiles):
    return jnp.where(i < prompt_tiles, i % seq_tiles, seq_tiles)


def _norm_matmul(x, g, w, cos, sin, *, n_prompt, seq, tn, n_rope_tiles, head_dim, out_dtype, name):
    n, d = x.shape
    nout = w.shape[1]
    tm = _proj_tile(n)
    pt, st = n_prompt // tm, seq // tm
    return pl.pallas_call(
        functools.partial(_norm_matmul_kernel, n_rope_tiles=n_rope_tiles, head_dim=head_dim),
        grid=(n // tm, nout // tn),
        in_specs=[
            pl.BlockSpec((tm, d), lambda i, j: (i, 0)),
            pl.BlockSpec((1, d), lambda i, j: (0, 0)),
            pl.BlockSpec((d, tn), lambda i, j: (0, j)),
            pl.BlockSpec((tm, cos.shape[1]), lambda i, j: (_table_tile(i, pt, st), 0)),
            pl.BlockSpec((tm, sin.shape[1]), lambda i, j: (_table_tile(i, pt, st), 0)),
        ],
        out_specs=pl.BlockSpec((tm, tn), lambda i, j: (i, j)),
        out_shape=jax.ShapeDtypeStruct((n, nout), out_dtype),
        scratch_shapes=[pltpu.VMEM((tm, d), BF16)],
        compiler_params=_cparams(2),
        name=name,
    )(x, g.reshape(1, d), w, cos, sin)


def _group_norm_gate(o, g, gn):
    mu = jnp.mean(o, axis=-1, keepdims=True)
    oc = o - mu
    var = jnp.mean(oc * oc, axis=-1, keepdims=True)
    on = oc * lax.rsqrt(var + GN_EPS) * gn
    gf = g.astype(F32)
    return (gf / (1.0 + jnp.exp(-gf))) * on


def _retention_step(q, k, v, state, lg):
    L = q.shape[0]
    row = lax.broadcasted_iota(jnp.int32, (L, L), 0)
    col = lax.broadcasted_iota(jnp.int32, (L, L), 1)
    diff = (row - col).astype(F32)
    decay = jnp.where(diff >= 0.0, jnp.exp(jnp.maximum(diff, 0.0) * lg), 0.0)
    scores = lax.dot_general(q, k, _NT, preferred_element_type=F32) * decay
    inner = jnp.dot(scores.astype(BF16), v, preferred_element_type=F32)
    idx = lax.broadcasted_iota(jnp.int32, (L, 1), 0).astype(F32)
    q_decay = jnp.exp((idx + 1.0) * lg)
    cross = jnp.dot(q, state.astype(BF16), preferred_element_type=F32) * q_decay
    k_decay = jnp.exp((L - 1.0 - idx) * lg)
    kd = (k.astype(F32) * k_decay).T.astype(BF16)
    chunk_decay = jnp.exp(jnp.full((1, 1), L, F32) * lg)
    new_state = chunk_decay * state + jnp.dot(kd, v, preferred_element_type=F32)
    return inner + cross, new_state


def _ret_prompt_kernel(lg_ref, q_ref, k_ref, v_ref, g_ref, gn_ref, o_ref, s_ref):
    hg = pl.program_id(1)
    c = pl.program_id(2)
    hp = s_ref.shape[1]
    dk, dv = s_ref.shape[2], s_ref.shape[3]

    @pl.when(c == 0)
    def _():
        s_ref[...] = jnp.zeros_like(s_ref)

    steps = []
    for j in range(hp):
        qk_cols = slice(j * dk, (j + 1) * dk)
        v_cols = slice(j * dv, (j + 1) * dv)
        steps.append(_retention_step(q_ref[:, qk_cols], k_ref[:, qk_cols], v_ref[:, v_cols], s_ref[0, j],
                                     lg_ref[hg * hp + j]))
    for j, (o, new_state) in enumerate(steps):
        v_cols = slice(j * dv, (j + 1) * dv)
        s_ref[0, j] = new_state
        o_ref[:, v_cols] = _group_norm_gate(o, g_ref[:, v_cols], gn_ref[:, v_cols]).astype(o_ref.dtype)


def _ret_prompt(qkvg, gn, lg, *, batch, seq, heads, dk, dv):
    L = min(RET_CHUNK, seq)
    nc = seq // L
    hp = RET_HEADS_PER_STEP
    ng = heads // hp
    kq = ng
    v0 = 2 * heads * dk // (hp * dv)
    g0 = v0 + ng
    grid_spec = pltpu.PrefetchScalarGridSpec(
        num_scalar_prefetch=1,
        grid=(batch, ng, nc),
        in_specs=[
            pl.BlockSpec((L, hp * dk), lambda b, h, c, lg: (b * nc + c, h)),
            pl.BlockSpec((L, hp * dk), lambda b, h, c, lg: (b * nc + c, kq + h)),
            pl.BlockSpec((L, hp * dv), lambda b, h, c, lg: (b * nc + c, v0 + h)),
            pl.BlockSpec((L, hp * dv), lambda b, h, c, lg: (b * nc + c, g0 + h)),
            pl.BlockSpec((1, hp * dv), lambda b, h, c, lg: (0, h)),
        ],
        out_specs=[
            pl.BlockSpec((L, hp * dv), lambda b, h, c, lg: (b * nc + c, h)),
            pl.BlockSpec((1, hp, dk, dv), lambda b, h, c, lg: (b, h, 0, 0)),
        ],
    )
    return pl.pallas_call(
        _ret_prompt_kernel,
        grid_spec=grid_spec,
        out_shape=[
            jax.ShapeDtypeStruct((batch * seq, heads * dv), BF16),
            jax.ShapeDtypeStruct((batch, heads, dk, dv), F32),
        ],
        compiler_params=_cparams(3),
        name="ret_prompt",
    )(lg, qkvg, qkvg, qkvg, qkvg, gn.reshape(1, -1))


def _ret_sample_kernel(lg_ref, q_ref, k_ref, v_ref, g_ref, gn_ref, s_in_ref, o_ref, s_out_ref, *, t_len):
    h = pl.program_id(1)
    lg = lg_ref[h]
    nb = s_in_ref.shape[0]
    n = nb * t_len
    q = q_ref[...]
    k = k_ref[...]
    v = v_ref[...]
    g = g_ref[...]
    gn = gn_ref[...]
    row = lax.broadcasted_iota(jnp.int32, (n, n), 0)
    col = lax.broadcasted_iota(jnp.int32, (n, n), 1)
    diff = (row - col).astype(F32)
    keep = (row // t_len == col // t_len) & (row >= col)
    decay = jnp.where(keep, jnp.exp(jnp.maximum(diff, 0.0) * lg), 0.0)
    scores = lax.dot_general(q, k, _NT, preferred_element_type=F32) * decay
    inner = jnp.dot(scores.astype(BF16), v, preferred_element_type=F32)
    idx = (lax.broadcasted_iota(jnp.int32, (n, 1), 0) % t_len).astype(F32)
    q_decay = jnp.exp((idx + 1.0) * lg)
    kd = k.astype(F32) * jnp.exp((t_len - 1.0 - idx) * lg)
    chunk_decay = jnp.exp(jnp.full((1, 1), t_len, F32) * lg)
    group = BF16_ROWS // t_len
    seq_of_row = lax.broadcasted_iota(jnp.int32, (BF16_ROWS, 1), 0) // t_len
    for p in range(nb // group):
        rows = slice(p * BF16_ROWS, (p + 1) * BF16_ROWS)
        q16, kd16, v16 = q[rows], kd[rows], v[rows]
        cross = jnp.zeros((BF16_ROWS, v.shape[1]), F32)
        for j in range(group):
            b = p * group + j
            state = s_in_ref[b, 0]
            mine = seq_of_row == j
            cross = jnp.where(mine, jnp.dot(q16, state.astype(BF16), preferred_element_type=F32), cross)
            kdb = jnp.where(mine, kd16, 0.0).T.astype(BF16)
            s_out_ref[b, 0] = chunk_decay * state + jnp.dot(kdb, v16, preferred_element_type=F32)
        o = inner[rows] + cross * q_decay[rows]
        o_ref[rows, :] = _group_norm_gate(o, g[rows], gn).astype(o_ref.dtype)


def _ret_sample(qkvg, gn, lg, state, *, row0, t_len, heads, dk, dv):
    nbatch = state.shape[0]
    nb = 8
    rows = nb * t_len
    r0 = row0 // rows
    kq = heads
    v0 = 2 * heads * dk // dv
    g0 = v0 + heads
    grid_spec = pltpu.PrefetchScalarGridSpec(
        num_scalar_prefetch=1,
        grid=(nbatch // nb, heads),
        in_specs=[
            pl.BlockSpec((rows, dk), lambda i, h, lg: (r0 + i, h)),
            pl.BlockSpec((rows, dk), lambda i, h, lg: (r0 + i, kq + h)),
            pl.BlockSpec((rows, dv), lambda i, h, lg: (r0 + i, v0 + h)),
            pl.BlockSpec((rows, dv), lambda i, h, lg: (r0 + i, g0 + h)),
            pl.BlockSpec((1, dv), lambda i, h, lg: (0, h)),
            pl.BlockSpec((nb, 1, dk, dv), lambda i, h, lg: (i, h, 0, 0)),
        ],
        out_specs=[
            pl.BlockSpec((rows, dv), lambda i, h, lg: (i, h)),
            pl.BlockSpec((nb, 1, dk, dv), lambda i, h, lg: (i, h, 0, 0)),
        ],
    )
    return pl.pallas_call(
        functools.partial(_ret_sample_kernel, t_len=t_len),
        grid_spec=grid_spec,
        out_shape=[
            jax.ShapeDtypeStruct((nbatch * t_len, heads * dv), BF16),
            jax.ShapeDtypeStruct(state.shape, F32),
        ],
        compiler_params=_cparams(2),
        name="ret_sample",
    )(lg, qkvg, qkvg, qkvg, qkvg, gn.reshape(1, -1), state)


def _matmul_res_kernel(xp_ref, xs_ref, w_ref, r_ref, o_ref, *, prompt_tiles):
    i = pl.program_id(0)

    @pl.when(i < prompt_tiles)
    def _():
        o_ref[...] = r_ref[...] + jnp.dot(xp_ref[...], w_ref[...], preferred_element_type=F32)

    @pl.when(i >= prompt_tiles)
    def _():
        o_ref[...] = r_ref[...] + jnp.dot(xs_ref[...], w_ref[...], preferred_element_type=F32)


def _matmul_res(x_prompt, x_sample, w, res, *, name):
    n, d = res.shape
    k = w.shape[0]
    tm = ROW_TILE
    pt = x_prompt.shape[0] // tm
    return pl.pallas_call(
        functools.partial(_matmul_res_kernel, prompt_tiles=pt),
        grid=(n // tm,),
        in_specs=[
            pl.BlockSpec((tm, k), lambda i: (jnp.minimum(i, pt - 1), 0)),
            pl.BlockSpec((tm, k), lambda i: (jnp.maximum(i - pt, 0), 0)),
            pl.BlockSpec((k, d), lambda i: (0, 0)),
            pl.BlockSpec((tm, d), lambda i: (i, 0)),
        ],
        out_specs=pl.BlockSpec((tm, d), lambda i: (i, 0)),
        out_shape=jax.ShapeDtypeStruct((n, d), F32),
        compiler_params=_cparams(1),
        name=name,
    )(x_prompt, x_sample, w, res)


def _router_kernel(x_ref, g_ref, wh_ref, wl_ref, b_ref, idx_ref, gate_ref, *, n_experts):
    xn = _rms(x_ref[...], g_ref[...])
    hi = xn.astype(BF16)
    lo = (xn - hi.astype(F32)).astype(BF16)
    logits = (jnp.dot(hi, wh_ref[...], preferred_element_type=F32)
              + jnp.dot(lo, wh_ref[...], preferred_element_type=F32)
              + jnp.dot(hi, wl_ref[...], preferred_element_type=F32)) + b_ref[...]
    lane = lax.broadcasted_iota(jnp.int32, logits.shape, 1).astype(F32)
    neg = jnp.float32(-jnp.inf)
    work = jnp.where(lane < n_experts, logits, neg)
    vals, idxs = [], []
    for _ in range(TOP_K):
        m = jnp.max(work, axis=-1, keepdims=True)
        sel = jnp.min(jnp.where(work == m, lane, float(LANES)), axis=-1, keepdims=True)
        vals.append(m)
        idxs.append(sel)
        work = jnp.where(lane == sel, neg, work)
    es = [jnp.exp(v - vals[0]) for v in vals]
    denom = es[0]
    for e in es[1:]:
        denom = denom + e
    idx_out = jnp.zeros(logits.shape, F32)
    gate_out = jnp.zeros(logits.shape, F32)
    for kk in range(TOP_K):
        idx_out = jnp.where(lane == kk, idxs[kk], idx_out)
        gate_out = jnp.where(lane == kk, es[kk] / denom, gate_out)
    idx_ref[...] = idx_out.astype(jnp.int32)
    gate_ref[...] = gate_out


def _router(y, g, wr, br):
    n, d = y.shape
    e = wr.shape[1]
    tm = ROW_TILE
    wr_pad = jnp.zeros((d, LANES), F32).at[:, :e].set(wr)
    wh = wr_pad.astype(BF16)
    wl = (wr_pad - wh.astype(F32)).astype(BF16)
    b_pad = jnp.zeros((1, LANES), F32).at[0, :e].set(br)
    return pl.pallas_call(
        functools.partial(_router_kernel, n_experts=e),
        grid=(n // tm,),
        in_specs=[
            pl.BlockSpec((tm, d), lambda i: (i, 0)),
            pl.BlockSpec((1, d), lambda i: (0, 0)),
            pl.BlockSpec((d, LANES), lambda i: (0, 0)),
            pl.BlockSpec((d, LANES), lambda i: (0, 0)),
            pl.BlockSpec((1, LANES), lambda i: (0, 0)),
        ],
        out_specs=[
            pl.BlockSpec((tm, LANES), lambda i: (i, 0)),
            pl.BlockSpec((tm, LANES), lambda i: (i, 0)),
        ],
        out_shape=[
            jax.ShapeDtypeStruct((n, LANES), jnp.int32),
            jax.ShapeDtypeStruct((n, LANES), F32),
        ],
        compiler_params=_cparams(1),
        name="moe_router",
    )(y, g.reshape(1, d), wh, wl, b_pad)


def _split_gate_up_kernel(w_ref, p_ref, o_ref):
    half = o_ref.shape[1] // 2
    hp = MXU_DIM // 2
    for c in range(o_ref.shape[1] // MXU_DIM):
        y = jnp.dot(w_ref[:, c * MXU_DIM:(c + 1) * MXU_DIM].astype(BF16), p_ref[...], preferred_element_type=F32)
        o_ref[:, c * hp:(c + 1) * hp] = y[:, :hp].astype(BF16)
        o_ref[:, half + c * hp:half + (c + 1) * hp] = y[:, hp:].astype(BF16)


def _split_gate_up(wgu_all, layer):
    _, e, d, w2 = wgu_all.shape
    rows = e * d
    tm = ROW_TILE
    row0 = layer * rows // tm
    src = lax.broadcasted_iota(jnp.int32, (MXU_DIM, MXU_DIM), 0)
    dst = lax.broadcasted_iota(jnp.int32, (MXU_DIM, MXU_DIM), 1)
    perm = (dst == (src % 2) * (MXU_DIM // 2) + src // 2).astype(BF16)
    out = pl.pallas_call(
        _split_gate_up_kernel,
        grid=(rows // tm,),
        in_specs=[pl.BlockSpec((tm, w2), lambda i: (row0 + i, 0)),
                  pl.BlockSpec((MXU_DIM, MXU_DIM), lambda i: (0, 0))],
        out_specs=pl.BlockSpec((tm, w2), lambda i: (i, 0)),
        out_shape=jax.ShapeDtypeStruct((rows, w2), BF16),
        compiler_params=_cparams(1),
        name="moe_split_gate_up",
    )(wgu_all.reshape(-1, w2), perm)
    return out.reshape(e, d, w2)


def _row_copy(src, dst, sem, src_row, dst_row):
    return pltpu.make_async_copy(src.at[pl.ds(src_row, 1)], dst.at[pl.ds(dst_row, 1)], sem)


def _dispatch_kernel(dest_ref, lo_ref, hi_ref, y_ref, xb_hbm, zblk, sem, *, n_experts):
    i = pl.program_id(0)
    tm = y_ref.shape[0]
    blk = zblk.shape[0]

    def zero_pads(e, wait):
        def body(r, _):
            cp = _row_copy(zblk, xb_hbm, sem.at[1], 0, r)
            cp.wait() if wait else cp.start()
            return 0
        lax.fori_loop(lo_ref[e], hi_ref[e], body, 0)

    def zero_block(b, _):
        cp = pltpu.make_async_copy(zblk, xb_hbm.at[pl.ds(b * blk, blk)], sem.at[1])
        cp.start()
        cp.wait()
        return 0

    @pl.when(i == 0)
    def _():
        zblk[...] = jnp.zeros_like(zblk)
        for e in range(n_experts):
            zero_pads(e, False)
            zero_pads(e, True)
        lax.fori_loop(hi_ref[n_experts - 1] // blk, xb_hbm.shape[0] // blk, zero_block, 0)

    def scatter(wait):
        def body(t, _):
            for kk in range(TOP_K):
                cp = _row_copy(y_ref, xb_hbm, sem.at[0], t, dest_ref[(i * tm + t) * TOP_K + kk])
                cp.wait() if wait else cp.start()
            return 0
        lax.fori_loop(0, tm, body, 0, unroll=DMA_UNROLL // TOP_K)

    scatter(False)
    scatter(True)


def _dispatch(y, dest, pad_lo, pad_hi, n_rows):
    n, d = y.shape
    tm = DISPATCH_TILE
    grid_spec = pltpu.PrefetchScalarGridSpec(
        num_scalar_prefetch=3,
        grid=(n // tm,),
        in_specs=[pl.BlockSpec((tm, d), lambda i, dest, lo, hi: (i, 0))],
        out_specs=pl.BlockSpec(memory_space=pl.ANY),
        scratch_shapes=[pltpu.VMEM((MOE_BLOCK, d), F32), pltpu.SemaphoreType.DMA((2,))],
    )
    return pl.pallas_call(
        functools.partial(_dispatch_kernel, n_experts=pad_lo.shape[0]),
        grid_spec=grid_spec,
        out_shape=jax.ShapeDtypeStruct((n_rows, d), F32),
        compiler_params=_cparams(1),
        name="moe_dispatch",
    )(dest, pad_lo, pad_hi, y)


def _expert_kernel(be_ref, nv_ref, x_ref, g_ref, wgu_ref, bgu_ref, wd_ref, bd_ref, o_ref, wd_bf, *, d_expert):
    i = pl.program_id(0)

    @pl.when((i == 0) | (be_ref[i] != be_ref[jnp.maximum(i - 1, 0)]))
    def _():
        wd_bf[...] = wd_ref[0, 0].astype(BF16)

    @pl.when(i < nv_ref[0])
    def _():
        x = _rms(x_ref[...], g_ref[...]).astype(BF16)
        hgu = jnp.dot(x, wgu_ref[0], preferred_element_type=F32) + bgu_ref[0]
        gate = jnp.minimum(hgu[:, :d_expert], SWIGLU_LIMIT)
        up = jnp.clip(hgu[:, d_expert:], -SWIGLU_LIMIT, SWIGLU_LIMIT)
        act = (up + 1.0) * gate * (1.0 / (1.0 + jnp.exp(-(gate * SWIGLU_ALPHA))))
        o_ref[...] = jnp.dot(act.astype(BF16), wd_bf[...], preferred_element_type=F32) + bd_ref[0]

    @pl.when(i >= nv_ref[0])
    def _():
        o_ref[...] = jnp.zeros_like(o_ref)


def _experts(xb, g, block_expert, n_valid, wgu, bgu, wd_all, layer, bd):
    r, d = xb.shape
    de = wd_all.shape[2]
    blk = MOE_BLOCK
    grid_spec = pltpu.PrefetchScalarGridSpec(
        num_scalar_prefetch=2,
        grid=(r // blk,),
        in_specs=[
            pl.BlockSpec((blk, d), lambda i, be, nv: (jnp.minimum(i, nv[0] - 1), 0)),
            pl.BlockSpec((1, d), lambda i, be, nv: (0, 0)),
            pl.BlockSpec((1, d, 2 * de), lambda i, be, nv: (be[i], 0, 0)),
            pl.BlockSpec((1, 1, 2 * de), lambda i, be, nv: (be[i], 0, 0)),
            pl.BlockSpec((1, 1, de, d), lambda i, be, nv: (layer, be[i], 0, 0)),
            pl.BlockSpec((1, 1, d), lambda i, be, nv: (be[i], 0, 0)),
        ],
        out_specs=pl.BlockSpec((blk, d), lambda i, be, nv: (i, 0)),
        scratch_shapes=[pltpu.VMEM((de, d), BF16)],
    )
    return pl.pallas_call(
        functools.partial(_expert_kernel, d_expert=de),
        grid_spec=grid_spec,
        out_shape=jax.ShapeDtypeStruct((r, d), F32),
        compiler_params=_cparams(1),
        name="moe_experts",
    )(block_expert, n_valid, xb, g.reshape(1, d), wgu, bgu, wd_all, bd)


def _combine_kernel(dest_ref, res_ref, gate_ref, yb_hbm, o_ref, buf, sem):
    i = pl.program_id(0)
    nsteps = pl.num_programs(0)
    tm = o_ref.shape[0]
    slot = i % 2

    def gather(tile, sl, wait):
        def body(t, _):
            for kk in range(TOP_K):
                cp = _row_copy(yb_hbm, buf.at[sl, kk], sem.at[sl], dest_ref[(tile * tm + t) * TOP_K + kk], t)
                cp.wait() if wait else cp.start()
            return 0
        lax.fori_loop(0, tm, body, 0, unroll=DMA_UNROLL // TOP_K)

    @pl.when(i == 0)
    def _():
        gather(0, 0, False)

    @pl.when(i + 1 < nsteps)
    def _():
        gather(i + 1, 1 - slot, False)

    gather(i, slot, True)
    acc = res_ref[...]
    gates = gate_ref[...]
    for kk in range(TOP_K):
        acc = acc + gates[:, kk:kk + 1] * buf[slot, kk]
    o_ref[...] = acc


def _combine(res, gate_pad, dest, yb):
    n, d = res.shape
    tm = COMBINE_TILE
    grid_spec = pltpu.PrefetchScalarGridSpec(
        num_scalar_prefetch=1,
        grid=(n // tm,),
        in_specs=[
            pl.BlockSpec((tm, d), lambda i, dest: (i, 0)),
            pl.BlockSpec((tm, LANES), lambda i, dest: (i, 0)),
            pl.BlockSpec(memory_space=pl.ANY),
        ],
        out_specs=pl.BlockSpec((tm, d), lambda i, dest: (i, 0)),
        scratch_shapes=[pltpu.VMEM((2, TOP_K, tm, d), F32), pltpu.SemaphoreType.DMA((2,))],
    )
    return pl.pallas_call(
        _combine_kernel,
        grid_spec=grid_spec,
        out_shape=jax.ShapeDtypeStruct((n, d), F32),
        compiler_params=_cparams(1),
        name="moe_combine",
    )(dest, res, gate_pad, yb)


def _moe(y, g, wr, br, wgu_all, bgu, wd_all, bd, layer):
    n, d = y.shape
    e = wr.shape[1]
    de = wd_all.shape[2]
    idx_pad, gate_pad = _router(y, g, wr, br)

    a = n * TOP_K
    blk = MOE_BLOCK
    flat_e = idx_pad[:, :TOP_K].reshape(a)
    onehot = (flat_e[:, None] == jnp.arange(e, dtype=jnp.int32)[None, :]).astype(jnp.int32)
    csum = jnp.cumsum(onehot, axis=0)
    counts = csum[-1]
    rank = jnp.take_along_axis(csum, flat_e[:, None], axis=1)[:, 0] - 1
    padded = ((counts + blk - 1) // blk) * blk
    ends = jnp.cumsum(padded)
    dest = ((ends - padded)[flat_e] + rank).astype(jnp.int32)
    n_blocks = -(-a // blk) + e
    r = n_blocks * blk
    block_start = jnp.arange(n_blocks, dtype=jnp.int32) * blk
    block_expert = jnp.minimum(jnp.sum((ends[None, :] <= block_start[:, None]).astype(jnp.int32), axis=1), e - 1)
    n_valid = (ends[-1:] // blk).astype(jnp.int32)
    pad_lo = (ends - padded + counts).astype(jnp.int32)

    xb = _dispatch(y, dest, pad_lo, ends.astype(jnp.int32), r)
    wgu_b = _split_gate_up(wgu_all, layer)
    bgu_b = jnp.concatenate([bgu[:, 0::2], bgu[:, 1::2]], axis=-1).reshape(e, 1, 2 * de)
    yb = _experts(xb, g, block_expert, n_valid, wgu_b, bgu_b, wd_all, layer, bd.reshape(e, 1, d))
    return _combine(y, gate_pad, dest, yb)


def _rope_lanes(x, c, sa, sb, half):
    n = x.shape[1]
    return x * c + pltpu.roll(x, n - half, 1) * sa + pltpu.roll(x, half, 1) * sb


def _mla_proj_kernel(dqkv_ref, gq_ref, gkv_ref, wuq_ref, wuk_ref, wuv_ref, ones_ref, c_ref, sa_ref, sb_ref,
                     q_ref, k_ref, v_ref, ckv_ref, kpe_ref, *, q_rank, kv_rank, heads, rope_half, scale):
    x = dqkv_ref[...]
    c = c_ref[...]
    sa = sa_ref[...]
    sb = sb_ref[...]
    cq = _rms(x[:, :q_rank], gq_ref[...]).astype(BF16)
    ckv = _rms(x[:, q_rank:q_rank + kv_rank], gkv_ref[...])
    ckv_ref[...] = ckv
    kpe = _rope_lanes(x[:, q_rank + kv_rank:], c, sa, sb, rope_half)
    kpe_ref[...] = kpe
    ckv_b = ckv.astype(BF16)
    q = jnp.dot(cq, wuq_ref[...], preferred_element_type=F32)
    k = jnp.dot(ckv_b, wuk_ref[...], preferred_element_type=F32)
    for h in range(heads):
        cols = slice(h * LANES, (h + 1) * LANES)
        q_ref[:, cols] = (_rope_lanes(q[:, cols], c, sa, sb, rope_half) * scale).astype(BF16)
        k_ref[:, cols] = (k[:, cols] + kpe).astype(BF16)
    v_ref[...] = (jnp.dot(ckv_b, wuv_ref[...], preferred_element_type=F32) + ones_ref[...]).astype(BF16)


def _mla_proj(dqkv, gq, gkv, wuq_p, wuk_p, wuv_p, ones_p, c, sa, sb, *, n_prompt, seq, q_rank, kv_rank, heads,
              rope_half, scale):
    n, w = dqkv.shape
    tm = ROW_TILE
    pt, st = n_prompt // tm, seq // tm
    hv = wuv_p.shape[1]
    full = lambda shape: pl.BlockSpec(shape, lambda i: (0,) * len(shape))
    rows = lambda width: pl.BlockSpec((tm, width), lambda i: (i, 0))
    table = pl.BlockSpec((tm, LANES), lambda i: (_table_tile(i, pt, st), 0))
    return pl.pallas_call(
        functools.partial(_mla_proj_kernel, q_rank=q_rank, kv_rank=kv_rank, heads=heads, rope_half=rope_half,
                          scale=scale),
        grid=(n // tm,),
        in_specs=[rows(w), full((1, q_rank)), full((1, kv_rank)), full(wuq_p.shape), full(wuk_p.shape),
                  full(wuv_p.shape), full((1, hv)), table, table, table],
        out_specs=[rows(heads * LANES), rows(heads * LANES), rows(hv), rows(kv_rank), rows(LANES)],
        out_shape=[
            jax.ShapeDtypeStruct((n, heads * LANES), BF16),
            jax.ShapeDtypeStruct((n, heads * LANES), BF16),
            jax.ShapeDtypeStruct((n, hv), BF16),
            jax.ShapeDtypeStruct((n, kv_rank), F32),
            jax.ShapeDtypeStruct((n, LANES), F32),
        ],
        compiler_params=_cparams(1),
        name="mla_proj",
    )(dqkv, gq.reshape(1, -1), gkv.reshape(1, -1), wuq_p, wuk_p, wuv_p, ones_p, c, sa, sb)


def _flash_kernel(q_ref, k_ref, v_ref, o_ref, *scratch, tile, heads, v_head):
    m_refs, acc_refs = scratch[:heads], scratch[heads:]
    qi = pl.program_id(2)
    for h in range(heads):
        m_refs[h][...] = jnp.full((tile, LANES), -jnp.inf, F32)
        acc_refs[h][...] = jnp.zeros((tile, LANES), F32)
    nchunk = tile // LANES

    def step(ki, masked):
        start = pl.multiple_of(ki * tile, tile)
        scores = []
        for h in range(heads):
            cols = slice(h * LANES, (h + 1) * LANES)
            s = lax.dot_general(q_ref[:, cols], k_ref[pl.ds(start, tile), cols], _NT, preferred_element_type=F32)
            if masked:
                row = lax.broadcasted_iota(jnp.int32, s.shape, 0)
                col = lax.broadcasted_iota(jnp.int32, s.shape, 1)
                s = jnp.where(col <= row, s, jnp.finfo(F32).min)
            scores.append(s)
        for h in range(heads):
            cols = slice(h * LANES, (h + 1) * LANES)
            chunks = [scores[h][:, c * LANES:(c + 1) * LANES] for c in range(nchunk)]
            part = chunks[0]
            for ch in chunks[1:]:
                part = jnp.maximum(part, ch)
            m_old = m_refs[h][...]
            m_new = jnp.maximum(m_old, jnp.max(part, axis=-1, keepdims=True))
            m_refs[h][...] = m_new
            p = jnp.concatenate([jnp.exp2(ch - m_new) for ch in chunks], axis=1).astype(BF16)
            acc_refs[h][...] = (jnp.exp2(m_old - m_new) * acc_refs[h][...]
                                + jnp.dot(p, v_ref[pl.ds(start, tile), cols], preferred_element_type=F32))

    def body(kp, carry):
        step(2 * kp, False)
        step(2 * kp + 1, False)
        return carry

    lax.fori_loop(0, qi // 2, body, 0)

    @pl.when(qi % 2 == 1)
    def _():
        step(qi - 1, False)

    step(qi, True)
    lane = lax.broadcasted_iota(jnp.int32, (tile, LANES), 1)
    for j in range(heads // 2):
        even = acc_refs[2 * j][...]
        odd = acc_refs[2 * j + 1][...]
        num = jnp.where(lane < v_head, even, odd)
        den = jnp.where(lane < v_head, pltpu.roll(even, v_head, 1), pltpu.roll(odd, v_head, 1))
        o_ref[:, j * LANES:(j + 1) * LANES] = (num / den).astype(o_ref.dtype)


def _flash(q_cat, k_cat, v_ext, *, batch, seq, heads, v_head):
    tile = min(ATTN_TILE, seq)
    nq = seq // tile
    hg = ATTN_HEADS
    return pl.pallas_call(
        functools.partial(_flash_kernel, tile=tile, heads=hg, v_head=v_head),
        grid=(batch, heads // hg, nq),
        in_specs=[
            pl.BlockSpec((tile, hg * LANES), lambda b, g, qi: (b * nq + qi, g)),
            pl.BlockSpec((seq, hg * LANES), lambda b, g, qi: (b, g)),
            pl.BlockSpec((seq, hg * LANES), lambda b, g, qi: (b, g)),
        ],
        out_specs=pl.BlockSpec((tile, hg * v_head), lambda b, g, qi: (b * nq + qi, g)),
        out_shape=jax.ShapeDtypeStruct((batch * seq, heads * v_head), BF16),
        scratch_shapes=[pltpu.VMEM((tile, LANES), F32)] * (2 * hg),
        compiler_params=_cparams(3),
        name="mla_flash",
    )(q_cat, k_cat, v_ext)


def _qlat_kernel(q_ref, w_ref, o_ref, *, heads, kv_rank):
    for h in range(heads):
        o_ref[:, h * kv_rank:(h + 1) * kv_rank] = jnp.dot(
            q_ref[:, h * LANES:(h + 1) * LANES], w_ref[h], preferred_element_type=F32).astype(o_ref.dtype)


def _qlat(q_cat, wukt_p, *, row0, nrows, heads, kv_rank):
    tm = min(ROW_TILE, nrows)
    r0 = row0 // tm
    return pl.pallas_call(
        functools.partial(_qlat_kernel, heads=heads, kv_rank=kv_rank),
        grid=(nrows // tm,),
        in_specs=[
            pl.BlockSpec((tm, heads * LANES), lambda i: (r0 + i, 0)),
            pl.BlockSpec(wukt_p.shape, lambda i: (0, 0, 0)),
        ],
        out_specs=pl.BlockSpec((tm, heads * kv_rank), lambda i: (i, 0)),
        out_shape=jax.ShapeDtypeStruct((nrows, heads * kv_rank), BF16),
        compiler_params=_cparams(1),
        name="mla_qlat",
    )(q_cat, wukt_p)


def _olat_kernel(x_ref, w_ref, o_ref, *, pairs, kv_rank):
    for p in range(pairs):
        o_ref[:, p * LANES:(p + 1) * LANES] = jnp.dot(
            x_ref[:, p * 2 * kv_rank:(p + 1) * 2 * kv_rank], w_ref[p], preferred_element_type=F32).astype(o_ref.dtype)


def _olat(o_lat, wuv_pair, *, heads, kv_rank):
    n = o_lat.shape[0]
    tm = min(ROW_TILE, n)
    return pl.pallas_call(
        functools.partial(_olat_kernel, pairs=heads // 2, kv_rank=kv_rank),
        grid=(n // tm,),
        in_specs=[
            pl.BlockSpec((tm, heads * kv_rank), lambda i: (i, 0)),
            pl.BlockSpec(wuv_pair.shape, lambda i: (0, 0, 0)),
        ],
        out_specs=pl.BlockSpec((tm, (heads // 2) * LANES), lambda i: (i, 0)),
        out_shape=jax.ShapeDtypeStruct((n, (heads // 2) * LANES), BF16),
        compiler_params=_cparams(1),
        name="mla_olat",
    )(o_lat, wuv_pair)


def _paged_copies(pt_ref, cache_ckv, cache_kpe, ckv_buf, kpe_buf, sem, b, slot, p, page):
    pg = pt_ref[b, p]
    return (
        pltpu.make_async_copy(cache_ckv.at[0, pg], ckv_buf.at[slot, pl.ds(p * page, page)], sem.at[0, slot]),
        pltpu.make_async_copy(cache_kpe.at[0, pg], kpe_buf.at[slot, :, pl.ds(p * page, page)], sem.at[1, slot]),
    )


def _paged_attn_kernel(pt_ref, qlat_ref, qpe_ref, cnew_ref, pnew_ref, cache_ckv, cache_kpe, o_ref,
                       ckv_buf, kpe_buf, sem, *, n_pages, page, heads):
    b = pl.program_id(0)
    nb = pl.num_programs(0)
    slot = b % 2

    def start_all(bb, sl):
        def body(p, _):
            for cp in _paged_copies(pt_ref, cache_ckv, cache_kpe, ckv_buf, kpe_buf, sem, bb, sl, p, page):
                cp.start()
            return 0
        lax.fori_loop(0, n_pages, body, 0)

    @pl.when(b == 0)
    def _():
        start_all(0, 0)

    @pl.when(b + 1 < nb)
    def _():
        start_all(b + 1, 1 - slot)

    def wait_body(p, _):
        for cp in _paged_copies(pt_ref, cache_ckv, cache_kpe, ckv_buf, kpe_buf, sem, b, slot, p, page):
            cp.wait()
        return 0
    lax.fori_loop(0, n_pages, wait_body, 0)

    qlat = qlat_ref[0]
    qpe = qpe_ref[0]
    ck = ckv_buf[slot].astype(BF16)
    kp = kpe_buf[slot].astype(BF16)
    s_past = (lax.dot_general(qlat, ck, _NT, preferred_element_type=F32)
              + jnp.dot(qpe, kp, preferred_element_type=F32))
    cn = cnew_ref[0].astype(BF16)
    pn = pnew_ref[0].astype(BF16)
    s_new = (lax.dot_general(qlat, cn, _NT, preferred_element_type=F32)
             + lax.dot_general(qpe, pn, _NT, preferred_element_type=F32))
    t_of_row = lax.broadcasted_iota(jnp.int32, s_new.shape, 0) // heads
    j = lax.broadcasted_iota(jnp.int32, s_new.shape, 1)
    s_new = jnp.where(j <= t_of_row, s_new, jnp.finfo(F32).min)
    m = jnp.maximum(jnp.max(s_past, axis=-1, keepdims=True), jnp.max(s_new, axis=-1, keepdims=True))
    p_past = jnp.exp2(s_past - m)
    p_new = jnp.exp2(s_new - m)
    denom = jnp.sum(p_past, axis=-1, keepdims=True) + jnp.sum(p_new, axis=-1, keepdims=True)
    o = (jnp.dot(p_past.astype(BF16), ck, preferred_element_type=F32)
         + jnp.dot(p_new.astype(BF16), cn, preferred_element_type=F32))
    o_ref[0] = (o / denom).astype(o_ref.dtype)


def _paged_attn(page_table, qlat, qpe, ckv_new, kpe_new, cache_ckv, cache_kpe, *, heads):
    nbatch, n_pages = page_table.shape
    page, kv_rank = cache_ckv.shape[2], cache_ckv.shape[3]
    rope = cache_kpe.shape[2]
    rows = qlat.shape[1]
    tpad = ckv_new.shape[1]
    grid_spec = pltpu.PrefetchScalarGridSpec(
        num_scalar_prefetch=1,
        grid=(nbatch,),
        in_specs=[
            pl.BlockSpec((1, rows, kv_rank), lambda b, pt: (b, 0, 0)),
            pl.BlockSpec((1, rows, rope), lambda b, pt: (b, 0, 0)),
            pl.BlockSpec((1, tpad, kv_rank), lambda b, pt: (b, 0, 0)),
            pl.BlockSpec((1, tpad, rope), lambda b, pt: (b, 0, 0)),
            pl.BlockSpec(memory_space=pl.ANY),
            pl.BlockSpec(memory_space=pl.ANY),
        ],
        out_specs=pl.BlockSpec((1, rows, kv_rank), lambda b, pt: (b, 0, 0)),
        scratch_shapes=[
            pltpu.VMEM((2, n_pages * page, kv_rank), F32),
            pltpu.VMEM((2, rope, n_pages * page), F32),
            pltpu.SemaphoreType.DMA((2, 2)),
        ],
    )
    return pl.pallas_call(
        functools.partial(_paged_attn_kernel, n_pages=n_pages, page=page, heads=heads),
        grid_spec=grid_spec,
        out_shape=jax.ShapeDtypeStruct((nbatch, rows, kv_rank), BF16),
        compiler_params=_cparams(1),
        name="mla_paged_attn",
    )(page_table, qlat, qpe, ckv_new, kpe_new, cache_ckv, cache_kpe)


def _rmsnorm_kernel(x_ref, g_ref, op_ref, os_ref, *, prompt_tiles):
    i = pl.program_id(0)
    y = _rms(x_ref[...], g_ref[...])

    @pl.when(i < prompt_tiles)
    def _():
        op_ref[...] = y

    @pl.when(i >= prompt_tiles)
    def _():
        os_ref[...] = y


def _rmsnorm(x, g, n_prompt):
    n, d = x.shape
    tm = ROW_TILE
    pt = n_prompt // tm
    return pl.pallas_call(
        functools.partial(_rmsnorm_kernel, prompt_tiles=pt),
        grid=(n // tm,),
        in_specs=[pl.BlockSpec((tm, d), lambda i: (i, 0)), pl.BlockSpec((1, d), lambda i: (0, 0))],
        out_specs=[
            pl.BlockSpec((tm, d), lambda i: (jnp.minimum(i, pt - 1), 0)),
            pl.BlockSpec((tm, d), lambda i: (jnp.maximum(i - pt, 0), 0)),
        ],
        out_shape=[jax.ShapeDtypeStruct((n_prompt, d), F32), jax.ShapeDtypeStruct((n - n_prompt, d), F32)],
        compiler_params=_cparams(1),
        name="final_norm",
    )(x, g.reshape(1, d))


def _rope_angles(pos, half):
    inv = ROPE_BASE ** (-jnp.arange(half, dtype=F32) / half)
    return pos.astype(F32)[:, None] * inv[None, :]


def _table_positions(seq, dec_seq, past_len, tm):
    return jnp.concatenate([jnp.arange(seq), jnp.tile(past_len + jnp.arange(dec_seq), tm // dec_seq)])


def kernel(x_prompt, x_sample, state_ret, cache_ckv, cache_kpe, page_table, norm_mix, norm_ffn, norm_final, ret_wq, ret_wk, ret_wv, ret_wg, ret_gn, ret_wo, mla_wdq, mla_gq, mla_wuq, mla_wdkv, mla_gkv, mla_wuk, mla_wuv, mla_wo, moe_wr, moe_br, moe_wgu, moe_bgu, moe_wd, moe_bd):
    batch, seq, d = x_prompt.shape
    dec_batch, dec_seq, _ = x_sample.shape
    ret_heads, dk, dv = state_ret.shape[2], state_ret.shape[3], state_ret.shape[4]
    page = cache_ckv.shape[2]
    kv_rank = cache_ckv.shape[3]
    rope_dim = cache_kpe.shape[3]
    past_len = page_table.shape[1] * page
    mla_heads, nope = mla_wuk.shape[2], mla_wuk.shape[3]
    v_head = mla_wuv.shape[3]
    q_rank = mla_wdq.shape[2]
    n_prompt = batch * seq
    n_sample = dec_batch * dec_seq
    assert nope + rope_dim <= LANES and 2 * v_head == LANES and dk == 2 * LANES

    y = jnp.concatenate([x_prompt.reshape(n_prompt, d), x_sample.reshape(n_sample, d)], axis=0)
    n_all = n_prompt + n_sample
    assert n_sample % _proj_tile(n_all) == 0 and seq % _proj_tile(n_all) == 0 and ROW_TILE % dec_seq == 0

    ang = _rope_angles(_table_positions(seq, dec_seq, past_len, _proj_tile(n_all)), dk // 2)
    w_all = jnp.concatenate([ret_wq[0], ret_wk[0] * (dk ** -0.5), ret_wv[0], ret_wg[0]], axis=1).astype(BF16)
    qkvg = _norm_matmul(y, norm_mix[0], w_all, jnp.cos(ang), jnp.sin(ang), n_prompt=n_prompt, seq=seq,
                        tn=ret_heads * dk, n_rope_tiles=2, head_dim=dk, out_dtype=BF16, name="ret_proj")
    lg = jnp.log(1.0 - 2.0 ** (-5.0 - jnp.arange(ret_heads, dtype=F32)))
    gated_p, state_p = _ret_prompt(qkvg, ret_gn[0], lg, batch=batch, seq=seq, heads=ret_heads, dk=dk, dv=dv)
    gated_s, state_s = _ret_sample(qkvg, ret_gn[0], lg, state_ret[0], row0=n_prompt, t_len=dec_seq,
                                   heads=ret_heads, dk=dk, dv=dv)
    y = _matmul_res(gated_p, gated_s, ret_wo[0].astype(BF16), y, name="ret_out")
    y = _moe(y, norm_ffn[0], moe_wr[0], moe_br[0], moe_wgu, moe_bgu[0], moe_wd, moe_bd[0], 0)

    half = rope_dim // 2
    ang = _rope_angles(_table_positions(seq, dec_seq, past_len, ROW_TILE), half)
    cos2 = jnp.concatenate([jnp.cos(ang), jnp.cos(ang)], axis=1)
    sin = jnp.sin(ang)
    ones = jnp.ones((ang.shape[0], LANES), F32)
    zeros = jnp.zeros((ang.shape[0], LANES), F32)
    c_tab = lax.dynamic_update_slice(ones, cos2, (0, nope))
    sa_tab = lax.dynamic_update_slice(zeros, -sin, (0, nope))
    sb_tab = lax.dynamic_update_slice(zeros, sin, (0, nope + half))

    wdkv = mla_wdkv[0]
    kpe_cols = jnp.zeros((d, LANES), F32).at[:, nope:nope + rope_dim].set(wdkv[:, kv_rank:])
    w_down = jnp.concatenate([mla_wdq[0], wdkv[:, :kv_rank], kpe_cols], axis=1).astype(BF16)
    no_rope = jnp.zeros((seq + _proj_tile(n_all), LANES), F32)
    dqkv = _norm_matmul(y, norm_mix[1], w_down, no_rope, no_rope, n_prompt=n_prompt, seq=seq, tn=w_down.shape[1],
                        n_rope_tiles=0, head_dim=0, out_dtype=F32, name="mla_down")

    qk_dim = nope + rope_dim
    wuq_p = jnp.zeros((q_rank, mla_heads, LANES), F32).at[:, :, :qk_dim].set(
        mla_wuq[0].reshape(q_rank, mla_heads, qk_dim)).reshape(q_rank, mla_heads * LANES).astype(BF16)
    wuk_p = jnp.zeros((kv_rank, mla_heads, LANES), F32).at[:, :, :nope].set(mla_wuk[0]).reshape(
        kv_rank, mla_heads * LANES).astype(BF16)
    wuv_pairs = mla_wuv[0].reshape(kv_rank, mla_heads // 2, 2, v_head)
    wuv_p = jnp.zeros((kv_rank, mla_heads // 2, 2, 2, v_head), F32)
    wuv_p = wuv_p.at[:, :, 0, 0].set(wuv_pairs[:, :, 0]).at[:, :, 1, 1].set(wuv_pairs[:, :, 1])
    wuv_p = wuv_p.reshape(kv_rank, mla_heads * LANES).astype(BF16)
    ones_p = jnp.zeros((mla_heads // 2, 2, 2, v_head), F32).at[:, 0, 1].set(1.0).at[:, 1, 0].set(1.0)
    ones_p = ones_p.reshape(1, mla_heads * LANES)
    scale = (float(qk_dim) ** -0.5) * math.log2(math.e)
    q_cat, k_cat, v_ext, ckv_all, kpe_all = _mla_proj(
        dqkv, mla_gq[0], mla_gkv[0], wuq_p, wuk_p, wuv_p, ones_p, c_tab, sa_tab, sb_tab,
        n_prompt=n_prompt, seq=seq, q_rank=q_rank, kv_rank=kv_rank, heads=mla_heads, rope_half=half, scale=scale)
    kpe_rows = kpe_all[:, nope:nope + rope_dim]

    o_p = _flash(q_cat, k_cat, v_ext, batch=batch, seq=seq, heads=mla_heads, v_head=v_head)

    wukt_p = jnp.zeros((mla_heads, LANES, kv_rank), F32).at[:, :nope, :].set(
        jnp.transpose(mla_wuk[0], (1, 2, 0))).astype(BF16)
    q_lat = _qlat(q_cat, wukt_p, row0=n_prompt, nrows=n_sample, heads=mla_heads, kv_rank=kv_rank)
    rows = dec_seq * mla_heads
    q_pe = q_cat[n_prompt:].reshape(n_sample, mla_heads, LANES)[:, :, nope:qk_dim].reshape(dec_batch, rows, rope_dim)
    tpad = BF16_ROWS
    ckv_new = jnp.zeros((dec_batch, tpad, kv_rank), F32).at[:, :dec_seq].set(
        ckv_all[n_prompt:].reshape(dec_batch, dec_seq, kv_rank))
    kpe_new = jnp.zeros((dec_batch, tpad, rope_dim), F32).at[:, :dec_seq].set(
        kpe_rows[n_prompt:].reshape(dec_batch, dec_seq, rope_dim))
    o_lat = _paged_attn(page_table, q_lat.reshape(dec_batch, rows, kv_rank), q_pe, ckv_new, kpe_new,
                        cache_ckv, jnp.swapaxes(cache_kpe, 2, 3), heads=mla_heads)
    wuv_h = jnp.transpose(mla_wuv[0], (1, 0, 2)).reshape(mla_heads // 2, 2, kv_rank, v_head)
    wuv_pair = jnp.zeros((mla_heads // 2, 2, kv_rank, 2, v_head), F32)
    wuv_pair = wuv_pair.at[:, 0, :, 0, :].set(wuv_h[:, 0]).at[:, 1, :, 1, :].set(wuv_h[:, 1])
    wuv_pair = wuv_pair.reshape(mla_heads // 2, 2 * kv_rank, LANES).astype(BF16)
    o_s = _olat(o_lat.reshape(n_sample, mla_heads * kv_rank), wuv_pair, heads=mla_heads, kv_rank=kv_rank)

    y = _matmul_res(o_p, o_s, mla_wo[0].astype(BF16), y, name="mla_out")
    y = _moe(y, norm_ffn[1], moe_wr[1], moe_br[1], moe_wgu, moe_bgu[1], moe_wd, moe_bd[1], 1)

    y_p, y_s = _rmsnorm(y, norm_final, n_prompt)
    return (
        y_p.reshape(batch, seq, d),
        y_s.reshape(dec_batch, dec_seq, d),
        state_p[None],
        state_s[None],
        ckv_all[:n_prompt].reshape(1, batch, seq, kv_rank),
        kpe_rows[:n_prompt].reshape(1, batch, seq, rope_dim),
        ckv_all[n_prompt:].reshape(1, dec_batch, dec_seq, kv_rank),
        kpe_rows[n_prompt:].reshape(1, dec_batch, dec_seq, rope_dim),
    )
```

```python
import functools
import math

import jax
import jax.numpy as jnp
from jax import lax
from jax.experimental import pallas as pl
from jax.experimental.pallas import tpu as pltpu

F32 = jnp.float32
BF16 = jnp.bfloat16

TOP_K = 4
SWIGLU_LIMIT = 7.0
SWIGLU_ALPHA = 1.702
NORM_EPS = 1e-6
GN_EPS = 1e-6
ROPE_BASE = 10000.0
LANES = 128
BF16_ROWS = 16
MXU_DIM = 256

ROW_TILE = 512
PROJ_ROW_TILE = 1024
RET_CHUNK = 256
RET_HEADS_PER_STEP = 2
MOE_BLOCK = 256
COMBINE_TILE = 128
DISPATCH_TILE = 256
ZERO_GROUP = 8
ATTN_TILE = 512
ATTN_HEADS = 4
DMA_UNROLL = 64
VMEM_LIMIT = 56 * 1024 * 1024

_NT = (((1,), (1,)), ((), ()))


def _cparams(n_axes, vmem=VMEM_LIMIT):
    return pltpu.CompilerParams(dimension_semantics=("arbitrary",) * n_axes, vmem_limit_bytes=vmem)


def _rms(x, g):
    return x * lax.rsqrt(jnp.mean(x * x, axis=-1, keepdims=True) + NORM_EPS) * g


def _norm_matmul_kernel(x_ref, g_ref, w_ref, cos_ref, sin_ref, o_ref, h_ref, *, n_rope_tiles, head_dim):
    j = pl.program_id(1)

    @pl.when(j == 0)
    def _():
        h_ref[...] = _rms(x_ref[...], g_ref[...]).astype(BF16)

    acc = jnp.dot(h_ref[...], w_ref[...], preferred_element_type=F32)
    tn = acc.shape[1]

    if n_rope_tiles:
        @pl.when(j < n_rope_tiles)
        def _():
            cos = cos_ref[...]
            sin = sin_ref[...]
            half = head_dim // 2
            for h in range(tn // head_dim):
                x1 = acc[:, h * head_dim:h * head_dim + half]
                x2 = acc[:, h * head_dim + half:(h + 1) * head_dim]
                o_ref[:, h * head_dim:h * head_dim + half] = (x1 * cos - x2 * sin).astype(o_ref.dtype)
                o_ref[:, h * head_dim + half:(h + 1) * head_dim] = (x1 * sin + x2 * cos).astype(o_ref.dtype)

        @pl.when(j >= n_rope_tiles)
        def _():
            o_ref[...] = acc.astype(o_ref.dtype)
    else:
        o_ref[...] = acc.astype(o_ref.dtype)


def _proj_tile(n):
    return PROJ_ROW_TILE if n % PROJ_ROW_TILE == 0 else ROW_TILE


def _table_tile(i, prompt_tiles, seq_tiles):
    return jnp.where(i < prompt_tiles, i % seq_tiles, seq_tiles)


def _norm_matmul(x, g, w, cos, sin, *, n_prompt, seq, tn, n_rope_tiles, head_dim, out_dtype, name):
    n, d = x.shape
    nout = w.shape[1]
    tm = _proj_tile(n)
    pt, st = n_prompt // tm, seq // tm
    return pl.pallas_call(
        functools.partial(_norm_matmul_kernel, n_rope_tiles=n_rope_tiles, head_dim=head_dim),
        grid=(n // tm, nout // tn),
        in_specs=[
            pl.BlockSpec((tm, d), lambda i, j: (i, 0)),
            pl.BlockSpec((1, d), lambda i, j: (0, 0)),
            pl.BlockSpec((d, tn), lambda i, j: (0, j)),
            pl.BlockSpec((tm, cos.shape[1]), lambda i, j: (_table_tile(i, pt, st), 0)),
            pl.BlockSpec((tm, sin.shape[1]), lambda i, j: (_table_tile(i, pt, st), 0)),
        ],
        out_specs=pl.BlockSpec((tm, tn), lambda i, j: (i, j)),
        out_shape=jax.ShapeDtypeStruct((n, nout), out_dtype),
        scratch_shapes=[pltpu.VMEM((tm, d), BF16)],
        compiler_params=_cparams(2),
        name=name,
    )(x, g.reshape(1, d), w, cos, sin)


def _group_norm_gate(o, g, gn):
    mu = jnp.mean(o, axis=-1, keepdims=True)
    oc = o - mu
    var = jnp.mean(oc * oc, axis=-1, keepdims=True)
    on = oc * lax.rsqrt(var + GN_EPS) * gn
    gf = g.astype(F32)
    return (gf / (1.0 + jnp.exp(-gf))) * on


def _retention_step(q, k, v, state, lg):
    L = q.shape[0]
    row = lax.broadcasted_iota(jnp.int32, (L, L), 0)
    col = lax.broadcasted_iota(jnp.int32, (L, L), 1)
    diff = (row - col).astype(F32)
    decay = jnp.where(diff >= 0.0, jnp.exp(jnp.maximum(diff, 0.0) * lg), 0.0)
    scores = lax.dot_general(q, k, _NT, preferred_element_type=F32) * decay
    inner = jnp.dot(scores.astype(BF16), v, preferred_element_type=F32)
    idx = lax.broadcasted_iota(jnp.int32, (L, 1), 0).astype(F32)
    q_decay = jnp.exp((idx + 1.0) * lg)
    cross = jnp.dot(q, state.astype(BF16), preferred_element_type=F32) * q_decay
    k_decay = jnp.exp((L - 1.0 - idx) * lg)
    kd = (k.astype(F32) * k_decay).T.astype(BF16)
    chunk_decay = jnp.exp(jnp.full((1, 1), L, F32) * lg)
    new_state = chunk_decay * state + jnp.dot(kd, v, preferred_element_type=F32)
    return inner + cross, new_state


def _ret_prompt_kernel(lg_ref, q_ref, k_ref, v_ref, g_ref, gn_ref, o_ref, s_ref):
    hg = pl.program_id(1)
    c = pl.program_id(2)
    hp = s_ref.shape[1]
    dk, dv = s_ref.shape[2], s_ref.shape[3]

    @pl.when(c == 0)
    def _():
        s_ref[...] = jnp.zeros_like(s_ref)

    steps = []
    for j in range(hp):
        qk_cols = slice(j * dk, (j + 1) * dk)
        v_cols = slice(j * dv, (j + 1) * dv)
        steps.append(_retention_step(q_ref[:, qk_cols], k_ref[:, qk_cols], v_ref[:, v_cols], s_ref[0, j],
                                     lg_ref[hg * hp + j]))
    for j, (o, new_state) in enumerate(steps):
        v_cols = slice(j * dv, (j + 1) * dv)
        s_ref[0, j] = new_state
        o_ref[:, v_cols] = _group_norm_gate(o, g_ref[:, v_cols], gn_ref[:, v_cols]).astype(o_ref.dtype)


def _ret_prompt(qkvg, gn, lg, *, batch, seq, heads, dk, dv):
    L = min(RET_CHUNK, seq)
    nc = seq // L
    hp = RET_HEADS_PER_STEP
    ng = heads // hp
    kq = ng
    v0 = 2 * heads * dk // (hp * dv)
    g0 = v0 + ng
    grid_spec = pltpu.PrefetchScalarGridSpec(
        num_scalar_prefetch=1,
        grid=(batch, ng, nc),
        in_specs=[
            pl.BlockSpec((L, hp * dk), lambda b, h, c, lg: (b * nc + c, h)),
            pl.BlockSpec((L, hp * dk), lambda b, h, c, lg: (b * nc + c, kq + h)),
            pl.BlockSpec((L, hp * dv), lambda b, h, c, lg: (b * nc + c, v0 + h)),
            pl.BlockSpec((L, hp * dv), lambda b, h, c, lg: (b * nc + c, g0 + h)),
            pl.BlockSpec((1, hp * dv), lambda b, h, c, lg: (0, h)),
        ],
        out_specs=[
            pl.BlockSpec((L, hp * dv), lambda b, h, c, lg: (b * nc + c, h)),
            pl.BlockSpec((1, hp, dk, dv), lambda b, h, c, lg: (b, h, 0, 0)),
        ],
    )
    return pl.pallas_call(
        _ret_prompt_kernel,
        grid_spec=grid_spec,
        out_shape=[
            jax.ShapeDtypeStruct((batch * seq, heads * dv), BF16),
            jax.ShapeDtypeStruct((batch, heads, dk, dv), F32),
        ],
        compiler_params=_cparams(3),
        name="ret_prompt",
    )(lg, qkvg, qkvg, qkvg, qkvg, gn.reshape(1, -1))


def _ret_sample_kernel(lg_ref, q_ref, k_ref, v_ref, g_ref, gn_ref, s_in_ref, o_ref, s_out_ref, *, t_len):
    h = pl.program_id(1)
    lg = lg_ref[h]
    nb = s_in_ref.shape[0]
    n = nb * t_len
    q = q_ref[...]
    k = k_ref[...]
    v = v_ref[...]
    g = g_ref[...]
    gn = gn_ref[...]
    row = lax.broadcasted_iota(jnp.int32, (n, n), 0)
    col = lax.broadcasted_iota(jnp.int32, (n, n), 1)
    diff = (row - col).astype(F32)
    keep = (row // t_len == col // t_len) & (row >= col)
    decay = jnp.where(keep, jnp.exp(jnp.maximum(diff, 0.0) * lg), 0.0)
    scores = lax.dot_general(q, k, _NT, preferred_element_type=F32) * decay
    inner = jnp.dot(scores.astype(BF16), v, preferred_element_type=F32)
    idx = (lax.broadcasted_iota(jnp.int32, (n, 1), 0) % t_len).astype(F32)
    q_decay = jnp.exp((idx + 1.0) * lg)
    kd = k.astype(F32) * jnp.exp((t_len - 1.0 - idx) * lg)
    chunk_decay = jnp.exp(jnp.full((1, 1), t_len, F32) * lg)
    group = BF16_ROWS // t_len
    seq_of_row = lax.broadcasted_iota(jnp.int32, (BF16_ROWS, 1), 0) // t_len
    for p in range(nb // group):
        rows = slice(p * BF16_ROWS, (p + 1) * BF16_ROWS)
        q16, kd16, v16 = q[rows], kd[rows], v[rows]
        cross = jnp.zeros((BF16_ROWS, v.shape[1]), F32)
        for j in range(group):
            b = p * group + j
            state = s_in_ref[b, 0]
            mine = seq_of_row == j
            cross = jnp.where(mine, jnp.dot(q16, state.astype(BF16), preferred_element_type=F32), cross)
            kdb = jnp.where(mine, kd16, 0.0).T.astype(BF16)
            s_out_ref[b, 0] = chunk_decay * state + jnp.dot(kdb, v16, preferred_element_type=F32)
        o = inner[rows] + cross * q_decay[rows]
        o_ref[rows, :] = _group_norm_gate(o, g[rows], gn).astype(o_ref.dtype)


def _ret_sample(qkvg, gn, lg, state, *, row0, t_len, heads, dk, dv):
    nbatch = state.shape[0]
    nb = 8
    rows = nb * t_len
    r0 = row0 // rows
    kq = heads
    v0 = 2 * heads * dk // dv
    g0 = v0 + heads
    grid_spec = pltpu.PrefetchScalarGridSpec(
        num_scalar_prefetch=1,
        grid=(nbatch // nb, heads),
        in_specs=[
            pl.BlockSpec((rows, dk), lambda i, h, lg: (r0 + i, h)),
            pl.BlockSpec((rows, dk), lambda i, h, lg: (r0 + i, kq + h)),
            pl.BlockSpec((rows, dv), lambda i, h, lg: (r0 + i, v0 + h)),
            pl.BlockSpec((rows, dv), lambda i, h, lg: (r0 + i, g0 + h)),
            pl.BlockSpec((1, dv), lambda i, h, lg: (0, h)),
            pl.BlockSpec((nb, 1, dk, dv), lambda i, h, lg: (i, h, 0, 0)),
        ],
        out_specs=[
            pl.BlockSpec((rows, dv), lambda i, h, lg: (i, h)),
            pl.BlockSpec((nb, 1, dk, dv), lambda i, h, lg: (i, h, 0, 0)),
        ],
    )
    return pl.pallas_call(
        functools.partial(_ret_sample_kernel, t_len=t_len),
        grid_spec=grid_spec,
        out_shape=[
            jax.ShapeDtypeStruct((nbatch * t_len, heads * dv), BF16),
            jax.ShapeDtypeStruct(state.shape, F32),
        ],
        compiler_params=_cparams(2),
        name="ret_sample",
    )(lg, qkvg, qkvg, qkvg, qkvg, gn.reshape(1, -1), state)


def _matmul_res_kernel(xp_ref, xs_ref, w_ref, r_ref, o_ref, *, prompt_tiles):
    i = pl.program_id(0)

    @pl.when(i < prompt_tiles)
    def _():
        o_ref[...] = r_ref[...] + jnp.dot(xp_ref[...], w_ref[...], preferred_element_type=F32)

    @pl.when(i >= prompt_tiles)
    def _():
        o_ref[...] = r_ref[...] + jnp.dot(xs_ref[...], w_ref[...], preferred_element_type=F32)


def _matmul_res(x_prompt, x_sample, w, res, *, name):
    n, d = res.shape
    k = w.shape[0]
    tm = ROW_TILE
    pt = x_prompt.shape[0] // tm
    return pl.pallas_call(
        functools.partial(_matmul_res_kernel, prompt_tiles=pt),
        grid=(n // tm,),
        in_specs=[
            pl.BlockSpec((tm, k), lambda i: (jnp.minimum(i, pt - 1), 0)),
            pl.BlockSpec((tm, k), lambda i: (jnp.maximum(i - pt, 0), 0)),
            pl.BlockSpec((k, d), lambda i: (0, 0)),
            pl.BlockSpec((tm, d), lambda i: (i, 0)),
        ],
        out_specs=pl.BlockSpec((tm, d), lambda i: (i, 0)),
        out_shape=jax.ShapeDtypeStruct((n, d), F32),
        compiler_params=_cparams(1),
        name=name,
    )(x_prompt, x_sample, w, res)


def _router_kernel(x_ref, g_ref, wh_ref, wl_ref, b_ref, idx_ref, gate_ref, *, n_experts):
    xn = _rms(x_ref[...], g_ref[...])
    hi = xn.astype(BF16)
    lo = (xn - hi.astype(F32)).astype(BF16)
    logits = (jnp.dot(hi, wh_ref[...], preferred_element_type=F32)
              + jnp.dot(lo, wh_ref[...], preferred_element_type=F32)
              + jnp.dot(hi, wl_ref[...], preferred_element_type=F32)) + b_ref[...]
    lane = lax.broadcasted_iota(jnp.int32, logits.shape, 1).astype(F32)
    neg = jnp.float32(-jnp.inf)
    work = jnp.where(lane < n_experts, logits, neg)
    vals, idxs = [], []
    for _ in range(TOP_K):
        m = jnp.max(work, axis=-1, keepdims=True)
        sel = jnp.min(jnp.where(work == m, lane, float(LANES)), axis=-1, keepdims=True)
        vals.append(m)
        idxs.append(sel)
        work = jnp.where(lane == sel, neg, work)
    es = [jnp.exp(v - vals[0]) for v in vals]
    denom = es[0]
    for e in es[1:]:
        denom = denom + e
    idx_out = jnp.zeros(logits.shape, F32)
    gate_out = jnp.zeros(logits.shape, F32)
    for kk in range(TOP_K):
        idx_out = jnp.where(lane == kk, idxs[kk], idx_out)
        gate_out = jnp.where(lane == kk, es[kk] / denom, gate_out)
    idx_ref[...] = idx_out.astype(jnp.int32)
    gate_ref[...] = gate_out


def _router(y, g, wr, br):
    n, d = y.shape
    e = wr.shape[1]
    tm = ROW_TILE
    wr_pad = jnp.zeros((d, LANES), F32).at[:, :e].set(wr)
    wh = wr_pad.astype(BF16)
    wl = (wr_pad - wh.astype(F32)).astype(BF16)
    b_pad = jnp.zeros((1, LANES), F32).at[0, :e].set(br)
    return pl.pallas_call(
        functools.partial(_router_kernel, n_experts=e),
        grid=(n // tm,),
        in_specs=[
            pl.BlockSpec((tm, d), lambda i: (i, 0)),
            pl.BlockSpec((1, d), lambda i: (0, 0)),
            pl.BlockSpec((d, LANES), lambda i: (0, 0)),
            pl.BlockSpec((d, LANES), lambda i: (0, 0)),
            pl.BlockSpec((1, LANES), lambda i: (0, 0)),
        ],
        out_specs=[
            pl.BlockSpec((tm, LANES), lambda i: (i, 0)),
            pl.BlockSpec((tm, LANES), lambda i: (i, 0)),
        ],
        out_shape=[
            jax.ShapeDtypeStruct((n, LANES), jnp.int32),
            jax.ShapeDtypeStruct((n, LANES), F32),
        ],
        compiler_params=_cparams(1),
        name="moe_router",
    )(y, g.reshape(1, d), wh, wl, b_pad)


def _split_gate_up_kernel(w_ref, p_ref, o_ref):
    half = o_ref.shape[1] // 2
    hp = MXU_DIM // 2
    for c in range(o_ref.shape[1] // MXU_DIM):
        y = jnp.dot(w_ref[:, c * MXU_DIM:(c + 1) * MXU_DIM].astype(BF16), p_ref[...], preferred_element_type=F32)
        o_ref[:, c * hp:(c + 1) * hp] = y[:, :hp].astype(BF16)
        o_ref[:, half + c * hp:half + (c + 1) * hp] = y[:, hp:].astype(BF16)


def _split_gate_up(wgu_all, layer):
    _, e, d, w2 = wgu_all.shape
    rows = e * d
    tm = ROW_TILE
    row0 = layer * rows // tm
    src = lax.broadcasted_iota(jnp.int32, (MXU_DIM, MXU_DIM), 0)
    dst = lax.broadcasted_iota(jnp.int32, (MXU_DIM, MXU_DIM), 1)
    perm = (dst == (src % 2) * (MXU_DIM // 2) + src // 2).astype(BF16)
    out = pl.pallas_call(
        _split_gate_up_kernel,
        grid=(rows // tm,),
        in_specs=[pl.BlockSpec((tm, w2), lambda i: (row0 + i, 0)),
                  pl.BlockSpec((MXU_DIM, MXU_DIM), lambda i: (0, 0))],
        out_specs=pl.BlockSpec((tm, w2), lambda i: (i, 0)),
        out_shape=jax.ShapeDtypeStruct((rows, w2), BF16),
        compiler_params=_cparams(1),
        name="moe_split_gate_up",
    )(wgu_all.reshape(-1, w2), perm)
    return out.reshape(e, d, w2)


def _row_copy(src, dst, sem, src_row, dst_row):
    return pltpu.make_async_copy(src.at[pl.ds(src_row, 1)], dst.at[pl.ds(dst_row, 1)], sem)


def _dispatch_kernel(dest_ref, hi_ref, y_ref, xb_hbm, zblk, sem, *, n_experts):
    i = pl.program_id(0)
    tm = y_ref.shape[0]
    blk = zblk.shape[0]
    n_blocks = xb_hbm.shape[0] // blk

    def zero_block(b, wait):
        cp = pltpu.make_async_copy(zblk, xb_hbm.at[pl.ds(pl.multiple_of(b * blk, blk), blk)], sem.at[1])
        cp.wait() if wait else cp.start()

    @pl.when(i == 0)
    def _():
        zblk[...] = jnp.zeros_like(zblk)
        for g in range(0, n_experts, ZERO_GROUP):
            for wait in (False, True):
                for e in range(g, min(g + ZERO_GROUP, n_experts)):
                    @pl.when(hi_ref[e] > (hi_ref[e - 1] if e else 0))
                    def _():
                        zero_block(hi_ref[e] // blk - 1, wait)

        first_unused = hi_ref[n_experts - 1] // blk

        def tail_group(gi, _):
            for wait in (False, True):
                for j in range(ZERO_GROUP):
                    b = first_unused + gi * ZERO_GROUP + j

                    @pl.when(b < n_blocks)
                    def _():
                        zero_block(b, wait)
            return 0

        lax.fori_loop(0, (n_blocks - first_unused + ZERO_GROUP - 1) // ZERO_GROUP, tail_group, 0)

    def scatter(wait):
        def body(t, _):
            for kk in range(TOP_K):
                cp = _row_copy(y_ref, xb_hbm, sem.at[0], t, dest_ref[(i * tm + t) * TOP_K + kk])
                cp.wait() if wait else cp.start()
            return 0
        lax.fori_loop(0, tm, body, 0, unroll=DMA_UNROLL // TOP_K)

    scatter(False)
    scatter(True)


def _dispatch(y, dest, region_end, n_rows):
    n, d = y.shape
    tm = DISPATCH_TILE
    grid_spec = pltpu.PrefetchScalarGridSpec(
        num_scalar_prefetch=2,
        grid=(n // tm,),
        in_specs=[pl.BlockSpec((tm, d), lambda i, dest, hi: (i, 0))],
        out_specs=pl.BlockSpec(memory_space=pl.ANY),
        scratch_shapes=[pltpu.VMEM((MOE_BLOCK, d), F32), pltpu.SemaphoreType.DMA((2,))],
    )
    return pl.pallas_call(
        functools.partial(_dispatch_kernel, n_experts=region_end.shape[0]),
        grid_spec=grid_spec,
        out_shape=jax.ShapeDtypeStruct((n_rows, d), F32),
        compiler_params=_cparams(1),
        name="moe_dispatch",
    )(dest, region_end, y)


def _expert_kernel(be_ref, nv_ref, x_ref, g_ref, wgu_ref, bgu_ref, wd_ref, bd_ref, o_ref, wd_bf, *, d_expert):
    i = pl.program_id(0)

    @pl.when((i == 0) | (be_ref[i] != be_ref[jnp.maximum(i - 1, 0)]))
    def _():
        wd_bf[...] = wd_ref[0, 0].astype(BF16)

    @pl.when(i < nv_ref[0])
    def _():
        x = _rms(x_ref[...], g_ref[...]).astype(BF16)
        hgu = jnp.dot(x, wgu_ref[0], preferred_element_type=F32) + bgu_ref[0]
        gate = jnp.minimum(hgu[:, :d_expert], SWIGLU_LIMIT)
        up = jnp.clip(hgu[:, d_expert:], -SWIGLU_LIMIT, SWIGLU_LIMIT)
        act = (up + 1.0) * gate * (1.0 / (1.0 + jnp.exp(-(gate * SWIGLU_ALPHA))))
        o_ref[...] = jnp.dot(act.astype(BF16), wd_bf[...], preferred_element_type=F32) + bd_ref[0]

    @pl.when(i >= nv_ref[0])
    def _():
        o_ref[...] = jnp.zeros_like(o_ref)


def _experts(xb, g, block_expert, n_valid, wgu, bgu, wd_all, layer, bd):
    r, d = xb.shape
    de = wd_all.shape[2]
    blk = MOE_BLOCK
    grid_spec = pltpu.PrefetchScalarGridSpec(
        num_scalar_prefetch=2,
        grid=(r // blk,),
        in_specs=[
            pl.BlockSpec((blk, d), lambda i, be, nv: (jnp.minimum(i, nv[0] - 1), 0)),
            pl.BlockSpec((1, d), lambda i, be, nv: (0, 0)),
            pl.BlockSpec((1, d, 2 * de), lambda i, be, nv: (be[i], 0, 0)),
            pl.BlockSpec((1, 1, 2 * de), lambda i, be, nv: (be[i], 0, 0)),
            pl.BlockSpec((1, 1, de, d), lambda i, be, nv: (layer, be[i], 0, 0)),
            pl.BlockSpec((1, 1, d), lambda i, be, nv: (be[i], 0, 0)),
        ],
        out_specs=pl.BlockSpec((blk, d), lambda i, be, nv: (i, 0)),
        scratch_shapes=[pltpu.VMEM((de, d), BF16)],
    )
    return pl.pallas_call(
        functools.partial(_expert_kernel, d_expert=de),
        grid_spec=grid_spec,
        out_shape=jax.ShapeDtypeStruct((r, d), F32),
        compiler_params=_cparams(1),
        name="moe_experts",
    )(block_expert, n_valid, xb, g.reshape(1, d), wgu, bgu, wd_all, bd)


def _combine_kernel(dest_ref, res_ref, gate_ref, yb_hbm, o_ref, buf, sem):
    i = pl.program_id(0)
    nsteps = pl.num_programs(0)
    tm = o_ref.shape[0]
    slot = i % 2

    def gather(tile, sl, wait):
        def body(t, _):
            for kk in range(TOP_K):
                cp = _row_copy(yb_hbm, buf.at[sl, kk], sem.at[sl], dest_ref[(tile * tm + t) * TOP_K + kk], t)
                cp.wait() if wait else cp.start()
            return 0
        lax.fori_loop(0, tm, body, 0, unroll=DMA_UNROLL // TOP_K)

    @pl.when(i == 0)
    def _():
        gather(0, 0, False)

    @pl.when(i + 1 < nsteps)
    def _():
        gather(i + 1, 1 - slot, False)

    gather(i, slot, True)
    acc = res_ref[...]
    gates = gate_ref[...]
    for kk in range(TOP_K):
        acc = acc + gates[:, kk:kk + 1] * buf[slot, kk]
    o_ref[...] = acc


def _combine(res, gate_pad, dest, yb):
    n, d = res.shape
    tm = COMBINE_TILE
    grid_spec = pltpu.PrefetchScalarGridSpec(
        num_scalar_prefetch=1,
        grid=(n // tm,),
        in_specs=[
            pl.BlockSpec((tm, d), lambda i, dest: (i, 0)),
            pl.BlockSpec((tm, LANES), lambda i, dest: (i, 0)),
            pl.BlockSpec(memory_space=pl.ANY),
        ],
        out_specs=pl.BlockSpec((tm, d), lambda i, dest: (i, 0)),
        scratch_shapes=[pltpu.VMEM((2, TOP_K, tm, d), F32), pltpu.SemaphoreType.DMA((2,))],
    )
    return pl.pallas_call(
        _combine_kernel,
        grid_spec=grid_spec,
        out_shape=jax.ShapeDtypeStruct((n, d), F32),
        compiler_params=_cparams(1),
        name="moe_combine",
    )(dest, res, gate_pad, yb)


def _moe(y, g, wr, br, wgu_all, bgu, wd_all, bd, layer):
    n, d = y.shape
    e = wr.shape[1]
    de = wd_all.shape[2]
    idx_pad, gate_pad = _router(y, g, wr, br)

    a = n * TOP_K
    blk = MOE_BLOCK
    flat_e = idx_pad[:, :TOP_K].reshape(a)
    onehot = (flat_e[:, None] == jnp.arange(e, dtype=jnp.int32)[None, :]).astype(jnp.int32)
    csum = jnp.cumsum(onehot, axis=0)
    counts = csum[-1]
    rank = jnp.take_along_axis(csum, flat_e[:, None], axis=1)[:, 0] - 1
    padded = ((counts + blk - 1) // blk) * blk
    ends = jnp.cumsum(padded)
    dest = ((ends - padded)[flat_e] + rank).astype(jnp.int32)
    n_blocks = -(-a // blk) + e
    r = n_blocks * blk
    block_start = jnp.arange(n_blocks, dtype=jnp.int32) * blk
    block_expert = jnp.minimum(jnp.sum((ends[None, :] <= block_start[:, None]).astype(jnp.int32), axis=1), e - 1)
    n_valid = (ends[-1:] // blk).astype(jnp.int32)

    xb = _dispatch(y, dest, ends.astype(jnp.int32), r)
    wgu_b = _split_gate_up(wgu_all, layer)
    bgu_b = jnp.concatenate([bgu[:, 0::2], bgu[:, 1::2]], axis=-1).reshape(e, 1, 2 * de)
    yb = _experts(xb, g, block_expert, n_valid, wgu_b, bgu_b, wd_all, layer, bd.reshape(e, 1, d))
    return _combine(y, gate_pad, dest, yb)


def _rope_lanes(x, c, sa, sb, half):
    n = x.shape[1]
    return x * c + pltpu.roll(x, n - half, 1) * sa + pltpu.roll(x, half, 1) * sb


def _mla_proj_kernel(dqkv_ref, gq_ref, gkv_ref, wuq_ref, wuk_ref, wuv_ref, ones_ref, c_ref, sa_ref, sb_ref,
                     q_ref, k_ref, v_ref, ckv_ref, kpe_ref, *, q_rank, kv_rank, heads, rope_half, scale):
    x = dqkv_ref[...]
    c = c_ref[...]
    sa = sa_ref[...]
    sb = sb_ref[...]
    cq = _rms(x[:, :q_rank], gq_ref[...]).astype(BF16)
    ckv = _rms(x[:, q_rank:q_rank + kv_rank], gkv_ref[...])
    ckv_ref[...] = ckv
    kpe = _rope_lanes(x[:, q_rank + kv_rank:], c, sa, sb, rope_half)
    kpe_ref[...] = kpe
    ckv_b = ckv.astype(BF16)
    q = jnp.dot(cq, wuq_ref[...], preferred_element_type=F32)
    k = jnp.dot(ckv_b, wuk_ref[...], preferred_element_type=F32)
    for h in range(heads):
        cols = slice(h * LANES, (h + 1) * LANES)
        q_ref[:, cols] = (_rope_lanes(q[:, cols], c, sa, sb, rope_half) * scale).astype(BF16)
        k_ref[:, cols] = (k[:, cols] + kpe).astype(BF16)
    v_ref[...] = (jnp.dot(ckv_b, wuv_ref[...], preferred_element_type=F32) + ones_ref[...]).astype(BF16)


def _mla_proj(dqkv, gq, gkv, wuq_p, wuk_p, wuv_p, ones_p, c, sa, sb, *, n_prompt, seq, q_rank, kv_rank, heads,
              rope_half, scale):
    n, w = dqkv.shape
    tm = ROW_TILE
    pt, st = n_prompt // tm, seq // tm
    hv = wuv_p.shape[1]
    full = lambda shape: pl.BlockSpec(shape, lambda i: (0,) * len(shape))
    rows = lambda width: pl.BlockSpec((tm, width), lambda i: (i, 0))
    table = pl.BlockSpec((tm, LANES), lambda i: (_table_tile(i, pt, st), 0))
    return pl.pallas_call(
        functools.partial(_mla_proj_kernel, q_rank=q_rank, kv_rank=kv_rank, heads=heads, rope_half=rope_half,
                          scale=scale),
        grid=(n // tm,),
        in_specs=[rows(w), full((1, q_rank)), full((1, kv_rank)), full(wuq_p.shape), full(wuk_p.shape),
                  full(wuv_p.shape), full((1, hv)), table, table, table],
        out_specs=[rows(heads * LANES), rows(heads * LANES), rows(hv), rows(kv_rank), rows(LANES)],
        out_shape=[
            jax.ShapeDtypeStruct((n, heads * LANES), BF16),
            jax.ShapeDtypeStruct((n, heads * LANES), BF16),
            jax.ShapeDtypeStruct((n, hv), BF16),
            jax.ShapeDtypeStruct((n, kv_rank), F32),
            jax.ShapeDtypeStruct((n, LANES), F32),
        ],
        compiler_params=_cparams(1),
        name="mla_proj",
    )(dqkv, gq.reshape(1, -1), gkv.reshape(1, -1), wuq_p, wuk_p, wuv_p, ones_p, c, sa, sb)


def _flash_kernel(q_ref, k_ref, v_ref, o_ref, *scratch, tile, heads, v_head):
    m_refs, acc_refs = scratch[:heads], scratch[heads:]
    qi = pl.program_id(2)
    for h in range(heads):
        m_refs[h][...] = jnp.full((tile, LANES), -jnp.inf, F32)
        acc_refs[h][...] = jnp.zeros((tile, LANES), F32)
    nchunk = tile // LANES

    def step(ki, masked):
        start = pl.multiple_of(ki * tile, tile)
        scores = []
        for h in range(heads):
            cols = slice(h * LANES, (h + 1) * LANES)
            s = lax.dot_general(q_ref[:, cols], k_ref[pl.ds(start, tile), cols], _NT, preferred_element_type=F32)
            if masked:
                row = lax.broadcasted_iota(jnp.int32, s.shape, 0)
                col = lax.broadcasted_iota(jnp.int32, s.shape, 1)
                s = jnp.where(col <= row, s, jnp.finfo(F32).min)
            scores.append(s)
        for h in range(heads):
            cols = slice(h * LANES, (h + 1) * LANES)
            chunks = [scores[h][:, c * LANES:(c + 1) * LANES] for c in range(nchunk)]
            part = chunks[0]
            for ch in chunks[1:]:
                part = jnp.maximum(part, ch)
            m_old = m_refs[h][...]
            m_new = jnp.maximum(m_old, jnp.max(part, axis=-1, keepdims=True))
            m_refs[h][...] = m_new
            p = jnp.concatenate([jnp.exp2(ch - m_new) for ch in chunks], axis=1).astype(BF16)
            acc_refs[h][...] = (jnp.exp2(m_old - m_new) * acc_refs[h][...]
                                + jnp.dot(p, v_ref[pl.ds(start, tile), cols], preferred_element_type=F32))

    def body(kp, carry):
        step(2 * kp, False)
        step(2 * kp + 1, False)
        return carry

    lax.fori_loop(0, qi // 2, body, 0)

    @pl.when(qi % 2 == 1)
    def _():
        step(qi - 1, False)

    step(qi, True)
    lane = lax.broadcasted_iota(jnp.int32, (tile, LANES), 1)
    for j in range(heads // 2):
        even = acc_refs[2 * j][...]
        odd = acc_refs[2 * j + 1][...]
        num = jnp.where(lane < v_head, even, odd)
        den = jnp.where(lane < v_head, pltpu.roll(even, v_head, 1), pltpu.roll(odd, v_head, 1))
        o_ref[:, j * LANES:(j + 1) * LANES] = (num / den).astype(o_ref.dtype)


def _flash(q_cat, k_cat, v_ext, *, batch, seq, heads, v_head):
    tile = min(ATTN_TILE, seq)
    nq = seq // tile
    hg = ATTN_HEADS
    return pl.pallas_call(
        functools.partial(_flash_kernel, tile=tile, heads=hg, v_head=v_head),
        grid=(batch, heads // hg, nq),
        in_specs=[
            pl.BlockSpec((tile, hg * LANES), lambda b, g, qi: (b * nq + qi, g)),
            pl.BlockSpec((seq, hg * LANES), lambda b, g, qi: (b, g)),
            pl.BlockSpec((seq, hg * LANES), lambda b, g, qi: (b, g)),
        ],
        out_specs=pl.BlockSpec((tile, hg * v_head), lambda b, g, qi: (b * nq + qi, g)),
        out_shape=jax.ShapeDtypeStruct((batch * seq, heads * v_head), BF16),
        scratch_shapes=[pltpu.VMEM((tile, LANES), F32)] * (2 * hg),
        compiler_params=_cparams(3),
        name="mla_flash",
    )(q_cat, k_cat, v_ext)


def _qlat_kernel(q_ref, w_ref, o_ref, *, heads, kv_rank):
    for h in range(heads):
        o_ref[:, h * kv_rank:(h + 1) * kv_rank] = jnp.dot(
            q_ref[:, h * LANES:(h + 1) * LANES], w_ref[h], preferred_element_type=F32).astype(o_ref.dtype)


def _qlat(q_cat, wukt_p, *, row0, nrows, heads, kv_rank):
    tm = min(ROW_TILE, nrows)
    r0 = row0 // tm
    return pl.pallas_call(
        functools.partial(_qlat_kernel, heads=heads, kv_rank=kv_rank),
        grid=(nrows // tm,),
        in_specs=[
            pl.BlockSpec((tm, heads * LANES), lambda i: (r0 + i, 0)),
            pl.BlockSpec(wukt_p.shape, lambda i: (0, 0, 0)),
        ],
        out_specs=pl.BlockSpec((tm, heads * kv_rank), lambda i: (i, 0)),
        out_shape=jax.ShapeDtypeStruct((nrows, heads * kv_rank), BF16),
        compiler_params=_cparams(1),
        name="mla_qlat",
    )(q_cat, wukt_p)


def _olat_kernel(x_ref, w_ref, o_ref, *, pairs, kv_rank):
    for p in range(pairs):
        o_ref[:, p * LANES:(p + 1) * LANES] = jnp.dot(
            x_ref[:, p * 2 * kv_rank:(p + 1) * 2 * kv_rank], w_ref[p], preferred_element_type=F32).astype(o_ref.dtype)


def _olat(o_lat, wuv_pair, *, heads, kv_rank):
    n = o_lat.shape[0]
    tm = min(ROW_TILE, n)
    return pl.pallas_call(
        functools.partial(_olat_kernel, pairs=heads // 2, kv_rank=kv_rank),
        grid=(n // tm,),
        in_specs=[
            pl.BlockSpec((tm, heads * kv_rank), lambda i: (i, 0)),
            pl.BlockSpec(wuv_pair.shape, lambda i: (0, 0, 0)),
        ],
        out_specs=pl.BlockSpec((tm, (heads // 2) * LANES), lambda i: (i, 0)),
        out_shape=jax.ShapeDtypeStruct((n, (heads // 2) * LANES), BF16),
        compiler_params=_cparams(1),
        name="mla_olat",
    )(o_lat, wuv_pair)


def _paged_copies(pt_ref, cache_ckv, cache_kpe, ckv_buf, kpe_buf, sem, b, slot, p, page):
    pg = pt_ref[b, p]
    return (
        pltpu.make_async_copy(cache_ckv.at[0, pg], ckv_buf.at[slot, pl.ds(p * page, page)], sem.at[0, slot]),
        pltpu.make_async_copy(cache_kpe.at[0, pg], kpe_buf.at[slot, :, pl.ds(p * page, page)], sem.at[1, slot]),
    )


def _paged_attn_kernel(pt_ref, qlat_ref, qpe_ref, cnew_ref, pnew_ref, cache_ckv, cache_kpe, o_ref,
                       ckv_buf, kpe_buf, sem, *, n_pages, page, heads):
    b = pl.program_id(0)
    nb = pl.num_programs(0)
    slot = b % 2

    def start_all(bb, sl):
        def body(p, _):
            for cp in _paged_copies(pt_ref, cache_ckv, cache_kpe, ckv_buf, kpe_buf, sem, bb, sl, p, page):
                cp.start()
            return 0
        lax.fori_loop(0, n_pages, body, 0)

    @pl.when(b == 0)
    def _():
        start_all(0, 0)

    @pl.when(b + 1 < nb)
    def _():
        start_all(b + 1, 1 - slot)

    def wait_body(p, _):
        for cp in _paged_copies(pt_ref, cache_ckv, cache_kpe, ckv_buf, kpe_buf, sem, b, slot, p, page):
            cp.wait()
        return 0
    lax.fori_loop(0, n_pages, wait_body, 0)

    qlat = qlat_ref[0]
    qpe = qpe_ref[0]
    ck = ckv_buf[slot].astype(BF16)
    kp = kpe_buf[slot].astype(BF16)
    s_past = (lax.dot_general(qlat, ck, _NT, preferred_element_type=F32)
              + jnp.dot(qpe, kp, preferred_element_type=F32))
    cn = cnew_ref[0].astype(BF16)
    pn = pnew_ref[0].astype(BF16)
    s_new = (lax.dot_general(qlat, cn, _NT, preferred_element_type=F32)
             + lax.dot_general(qpe, pn, _NT, preferred_element_type=F32))
    t_of_row = lax.broadcasted_iota(jnp.int32, s_new.shape, 0) // heads
    j = lax.broadcasted_iota(jnp.int32, s_new.shape, 1)
    s_new = jnp.where(j <= t_of_row, s_new, jnp.finfo(F32).min)
    m = jnp.maximum(jnp.max(s_past, axis=-1, keepdims=True), jnp.max(s_new, axis=-1, keepdims=True))
    p_past = jnp.exp2(s_past - m)
    p_new = jnp.exp2(s_new - m)
    denom = jnp.sum(p_past, axis=-1, keepdims=True) + jnp.sum(p_new, axis=-1, keepdims=True)
    o = (jnp.dot(p_past.astype(BF16), ck, preferred_element_type=F32)
         + jnp.dot(p_new.astype(BF16), cn, preferred_element_type=F32))
    o_ref[0] = (o / denom).astype(o_ref.dtype)


def _paged_attn(page_table, qlat, qpe, ckv_new, kpe_new, cache_ckv, cache_kpe, *, heads):
    nbatch, n_pages = page_table.shape
    page, kv_rank = cache_ckv.shape[2], cache_ckv.shape[3]
    rope = cache_kpe.shape[2]
    rows = qlat.shape[1]
    tpad = ckv_new.shape[1]
    grid_spec = pltpu.PrefetchScalarGridSpec(
        num_scalar_prefetch=1,
        grid=(nbatch,),
        in_specs=[
            pl.BlockSpec((1, rows, kv_rank), lambda b, pt: (b, 0, 0)),
            pl.BlockSpec((1, rows, rope), lambda b, pt: (b, 0, 0)),
            pl.BlockSpec((1, tpad, kv_rank), lambda b, pt: (b, 0, 0)),
            pl.BlockSpec((1, tpad, rope), lambda b, pt: (b, 0, 0)),
            pl.BlockSpec(memory_space=pl.ANY),
            pl.BlockSpec(memory_space=pl.ANY),
        ],
        out_specs=pl.BlockSpec((1, rows, kv_rank), lambda b, pt: (b, 0, 0)),
        scratch_shapes=[
            pltpu.VMEM((2, n_pages * page, kv_rank), F32),
            pltpu.VMEM((2, rope, n_pages * page), F32),
            pltpu.SemaphoreType.DMA((2, 2)),
        ],
    )
    return pl.pallas_call(
        functools.partial(_paged_attn_kernel, n_pages=n_pages, page=page, heads=heads),
        grid_spec=grid_spec,
        out_shape=jax.ShapeDtypeStruct((nbatch, rows, kv_rank), BF16),
        compiler_params=_cparams(1),
        name="mla_paged_attn",
    )(page_table, qlat, qpe, ckv_new, kpe_new, cache_ckv, cache_kpe)


def _rmsnorm_kernel(x_ref, g_ref, op_ref, os_ref, *, prompt_tiles):
    i = pl.program_id(0)
    y = _rms(x_ref[...], g_ref[...])

    @pl.when(i < prompt_tiles)
    def _():
        op_ref[...] = y

    @pl.when(i >= prompt_tiles)
    def _():
        os_ref[...] = y


def _rmsnorm(x, g, n_prompt):
    n, d = x.shape
    tm = ROW_TILE
    pt = n_prompt // tm
    return pl.pallas_call(
        functools.partial(_rmsnorm_kernel, prompt_tiles=pt),
        grid=(n // tm,),
        in_specs=[pl.BlockSpec((tm, d), lambda i: (i, 0)), pl.BlockSpec((1, d), lambda i: (0, 0))],
        out_specs=[
            pl.BlockSpec((tm, d), lambda i: (jnp.minimum(i, pt - 1), 0)),
            pl.BlockSpec((tm, d), lambda i: (jnp.maximum(i - pt, 0), 0)),
        ],
        out_shape=[jax.ShapeDtypeStruct((n_prompt, d), F32), jax.ShapeDtypeStruct((n - n_prompt, d), F32)],
        compiler_params=_cparams(1),
        name="final_norm",
    )(x, g.reshape(1, d))


def _rope_angles(pos, half):
    inv = ROPE_BASE ** (-jnp.arange(half, dtype=F32) / half)
    return pos.astype(F32)[:, None] * inv[None, :]


def _table_positions(seq, dec_seq, past_len, tm):
    return jnp.concatenate([jnp.arange(seq), jnp.tile(past_len + jnp.arange(dec_seq), tm // dec_seq)])


def kernel(x_prompt, x_sample, state_ret, cache_ckv, cache_kpe, page_table, norm_mix, norm_ffn, norm_final, ret_wq, ret_wk, ret_wv, ret_wg, ret_gn, ret_wo, mla_wdq, mla_gq, mla_wuq, mla_wdkv, mla_gkv, mla_wuk, mla_wuv, mla_wo, moe_wr, moe_br, moe_wgu, moe_bgu, moe_wd, moe_bd):
    batch, seq, d = x_prompt.shape
    dec_batch, dec_seq, _ = x_sample.shape
    ret_heads, dk, dv = state_ret.shape[2], state_ret.shape[3], state_ret.shape[4]
    page = cache_ckv.shape[2]
    kv_rank = cache_ckv.shape[3]
    rope_dim = cache_kpe.shape[3]
    past_len = page_table.shape[1] * page
    mla_heads, nope = mla_wuk.shape[2], mla_wuk.shape[3]
    v_head = mla_wuv.shape[3]
    q_rank = mla_wdq.shape[2]
    n_prompt = batch * seq
    n_sample = dec_batch * dec_seq
    assert nope + rope_dim <= LANES and 2 * v_head == LANES and dk == 2 * LANES

    y = jnp.concatenate([x_prompt.reshape(n_prompt, d), x_sample.reshape(n_sample, d)], axis=0)
    n_all = n_prompt + n_sample
    assert n_sample % _proj_tile(n_all) == 0 and seq % _proj_tile(n_all) == 0 and ROW_TILE % dec_seq == 0

    ang = _rope_angles(_table_positions(seq, dec_seq, past_len, _proj_tile(n_all)), dk // 2)
    w_all = jnp.concatenate([ret_wq[0], ret_wk[0] * (dk ** -0.5), ret_wv[0], ret_wg[0]], axis=1).astype(BF16)
    qkvg = _norm_matmul(y, norm_mix[0], w_all, jnp.cos(ang), jnp.sin(ang), n_prompt=n_prompt, seq=seq,
                        tn=ret_heads * dk, n_rope_tiles=2, head_dim=dk, out_dtype=BF16, name="ret_proj")
    lg = jnp.log(1.0 - 2.0 ** (-5.0 - jnp.arange(ret_heads, dtype=F32)))
    gated_p, state_p = _ret_prompt(qkvg, ret_gn[0], lg, batch=batch, seq=seq, heads=ret_heads, dk=dk, dv=dv)
    gated_s, state_s = _ret_sample(qkvg, ret_gn[0], lg, state_ret[0], row0=n_prompt, t_len=dec_seq,
                                   heads=ret_heads, dk=dk, dv=dv)
    y = _matmul_res(gated_p, gated_s, ret_wo[0].astype(BF16), y, name="ret_out")
    y = _moe(y, norm_ffn[0], moe_wr[0], moe_br[0], moe_wgu, moe_bgu[0], moe_wd, moe_bd[0], 0)

    half = rope_dim // 2
    ang = _rope_angles(_table_positions(seq, dec_seq, past_len, ROW_TILE), half)
    cos2 = jnp.concatenate([jnp.cos(ang), jnp.cos(ang)], axis=1)
    sin = jnp.sin(ang)
    ones = jnp.ones((ang.shape[0], LANES), F32)
    zeros = jnp.zeros((ang.shape[0], LANES), F32)
    c_tab = lax.dynamic_update_slice(ones, cos2, (0, nope))
    sa_tab = lax.dynamic_update_slice(zeros, -sin, (0, nope))
    sb_tab = lax.dynamic_update_slice(zeros, sin, (0, nope + half))

    wdkv = mla_wdkv[0]
    kpe_cols = jnp.zeros((d, LANES), F32).at[:, nope:nope + rope_dim].set(wdkv[:, kv_rank:])
    w_down = jnp.concatenate([mla_wdq[0], wdkv[:, :kv_rank], kpe_cols], axis=1).astype(BF16)
    no_rope = jnp.zeros((seq + _proj_tile(n_all), LANES), F32)
    dqkv = _norm_matmul(y, norm_mix[1], w_down, no_rope, no_rope, n_prompt=n_prompt, seq=seq, tn=w_down.shape[1],
                        n_rope_tiles=0, head_dim=0, out_dtype=F32, name="mla_down")

    qk_dim = nope + rope_dim
    wuq_p = jnp.zeros((q_rank, mla_heads, LANES), F32).at[:, :, :qk_dim].set(
        mla_wuq[0].reshape(q_rank, mla_heads, qk_dim)).reshape(q_rank, mla_heads * LANES).astype(BF16)
    wuk_p = jnp.zeros((kv_rank, mla_heads, LANES), F32).at[:, :, :nope].set(mla_wuk[0]).reshape(
        kv_rank, mla_heads * LANES).astype(BF16)
    wuv_pairs = mla_wuv[0].reshape(kv_rank, mla_heads // 2, 2, v_head)
    wuv_p = jnp.zeros((kv_rank, mla_heads // 2, 2, 2, v_head), F32)
    wuv_p = wuv_p.at[:, :, 0, 0].set(wuv_pairs[:, :, 0]).at[:, :, 1, 1].set(wuv_pairs[:, :, 1])
    wuv_p = wuv_p.reshape(kv_rank, mla_heads * LANES).astype(BF16)
    ones_p = jnp.zeros((mla_heads // 2, 2, 2, v_head), F32).at[:, 0, 1].set(1.0).at[:, 1, 0].set(1.0)
    ones_p = ones_p.reshape(1, mla_heads * LANES)
    scale = (float(qk_dim) ** -0.5) * math.log2(math.e)
    q_cat, k_cat, v_ext, ckv_all, kpe_all = _mla_proj(
        dqkv, mla_gq[0], mla_gkv[0], wuq_p, wuk_p, wuv_p, ones_p, c_tab, sa_tab, sb_tab,
        n_prompt=n_prompt, seq=seq, q_rank=q_rank, kv_rank=kv_rank, heads=mla_heads, rope_half=half, scale=scale)
    kpe_rows = kpe_all[:, nope:nope + rope_dim]

    o_p = _flash(q_cat, k_cat, v_ext, batch=batch, seq=seq, heads=mla_heads, v_head=v_head)

    wukt_p = jnp.zeros((mla_heads, LANES, kv_rank), F32).at[:, :nope, :].set(
        jnp.transpose(mla_wuk[0], (1, 2, 0))).astype(BF16)
    q_lat = _qlat(q_cat, wukt_p, row0=n_prompt, nrows=n_sample, heads=mla_heads, kv_rank=kv_rank)
    rows = dec_seq * mla_heads
    q_pe = q_cat[n_prompt:].reshape(n_sample, mla_heads, LANES)[:, :, nope:qk_dim].reshape(dec_batch, rows, rope_dim)
    tpad = BF16_ROWS
    ckv_new = jnp.zeros((dec_batch, tpad, kv_rank), F32).at[:, :dec_seq].set(
        ckv_all[n_prompt:].reshape(dec_batch, dec_seq, kv_rank))
    kpe_new = jnp.zeros((dec_batch, tpad, rope_dim), F32).at[:, :dec_seq].set(
        kpe_rows[n_prompt:].reshape(dec_batch, dec_seq, rope_dim))
    o_lat = _paged_attn(page_table, q_lat.reshape(dec_batch, rows, kv_rank), q_pe, ckv_new, kpe_new,
                        cache_ckv, jnp.swapaxes(cache_kpe, 2, 3), heads=mla_heads)
    wuv_h = jnp.transpose(mla_wuv[0], (1, 0, 2)).reshape(mla_heads // 2, 2, kv_rank, v_head)
    wuv_pair = jnp.zeros((mla_heads // 2, 2, kv_rank, 2, v_head), F32)
    wuv_pair = wuv_pair.at[:, 0, :, 0, :].set(wuv_h[:, 0]).at[:, 1, :, 1, :].set(wuv_h[:, 1])
    wuv_pair = wuv_pair.reshape(mla_heads // 2, 2 * kv_rank, LANES).astype(BF16)
    o_s = _olat(o_lat.reshape(n_sample, mla_heads * kv_rank), wuv_pair, heads=mla_heads, kv_rank=kv_rank)

    y = _matmul_res(o_p, o_s, mla_wo[0].astype(BF16), y, name="mla_out")
    y = _moe(y, norm_ffn[1], moe_wr[1], moe_br[1], moe_wgu, moe_bgu[1], moe_wd, moe_bd[1], 1)

    y_p, y_s = _rmsnorm(y, norm_final, n_prompt)
    return (
        y_p.reshape(batch, seq, d),
        y_s.reshape(dec_batch, dec_seq, d),
        state_p[None],
        state_s[None],
        ckv_all[:n_prompt].reshape(1, batch, seq, kv_rank),
        kpe_rows[:n_prompt].reshape(1, batch, seq, rope_dim),
        ckv_all[n_prompt:].reshape(1, dec_batch, dec_seq, kv_rank),
        kpe_rows[n_prompt:].reshape(1, dec_batch, dec_seq, rope_dim),
    )
```

```python
import functools
import math

import jax
import jax.numpy as jnp
from jax import lax
from jax.experimental import pallas as pl
from jax.experimental.pallas import tpu as pltpu

F32 = jnp.float32
BF16 = jnp.bfloat16

TOP_K = 4
SWIGLU_LIMIT = 7.0
SWIGLU_ALPHA = 1.702
NORM_EPS = 1e-6
GN_EPS = 1e-6
ROPE_BASE = 10000.0
LANES = 128
BF16_ROWS = 16
MXU_DIM = 256

ROW_TILE = 512
PROJ_ROW_TILE = 1024
RET_CHUNK = 256
RET_HEADS_PER_STEP = 2
MOE_BLOCK = 512
EXPERT_ROW_GROUPS = 2
COMBINE_TILE = 128
DISPATCH_TILE = 256
ZERO_GROUP = 4
ATTN_TILE = 512
ATTN_HEADS = 4
DMA_UNROLL = 64
VMEM_LIMIT = 56 * 1024 * 1024

_NT = (((1,), (1,)), ((), ()))


def _cparams(n_axes, vmem=VMEM_LIMIT):
    return pltpu.CompilerParams(dimension_semantics=("arbitrary",) * n_axes, vmem_limit_bytes=vmem)


def _rms(x, g):
    return x * lax.rsqrt(jnp.mean(x * x, axis=-1, keepdims=True) + NORM_EPS) * g


def _norm_matmul_kernel(x_ref, g_ref, w_ref, cos_ref, sin_ref, o_ref, h_ref, *, n_rope_tiles, head_dim):
    j = pl.program_id(1)

    @pl.when(j == 0)
    def _():
        h_ref[...] = _rms(x_ref[...], g_ref[...]).astype(BF16)

    acc = jnp.dot(h_ref[...], w_ref[...], preferred_element_type=F32)
    tn = acc.shape[1]

    if n_rope_tiles:
        @pl.when(j < n_rope_tiles)
        def _():
            cos = cos_ref[...]
            sin = sin_ref[...]
            half = head_dim // 2
            for h in range(tn // head_dim):
                x1 = acc[:, h * head_dim:h * head_dim + half]
                x2 = acc[:, h * head_dim + half:(h + 1) * head_dim]
                o_ref[:, h * head_dim:h * head_dim + half] = (x1 * cos - x2 * sin).astype(o_ref.dtype)
                o_ref[:, h * head_dim + half:(h + 1) * head_dim] = (x1 * sin + x2 * cos).astype(o_ref.dtype)

        @pl.when(j >= n_rope_tiles)
        def _():
            o_ref[...] = acc.astype(o_ref.dtype)
    else:
        o_ref[...] = acc.astype(o_ref.dtype)


def _proj_tile(n):
    return PROJ_ROW_TILE if n % PROJ_ROW_TILE == 0 else ROW_TILE


def _table_tile(i, prompt_tiles, seq_tiles):
    return jnp.where(i < prompt_tiles, i % seq_tiles, seq_tiles)


def _norm_matmul(x, g, w, cos, sin, *, n_prompt, seq, tn, n_rope_tiles, head_dim, out_dtype, name):
    n, d = x.shape
    nout = w.shape[1]
    tm = _proj_tile(n)
    pt, st = n_prompt // tm, seq // tm
    return pl.pallas_call(
        functools.partial(_norm_matmul_kernel, n_rope_tiles=n_rope_tiles, head_dim=head_dim),
        grid=(n // tm, nout // tn),
        in_specs=[
            pl.BlockSpec((tm, d), lambda i, j: (i, 0)),
            pl.BlockSpec((1, d), lambda i, j: (0, 0)),
            pl.BlockSpec((d, tn), lambda i, j: (0, j)),
            pl.BlockSpec((tm, cos.shape[1]), lambda i, j: (_table_tile(i, pt, st), 0)),
            pl.BlockSpec((tm, sin.shape[1]), lambda i, j: (_table_tile(i, pt, st), 0)),
        ],
        out_specs=pl.BlockSpec((tm, tn), lambda i, j: (i, j)),
        out_shape=jax.ShapeDtypeStruct((n, nout), out_dtype),
        scratch_shapes=[pltpu.VMEM((tm, d), BF16)],
        compiler_params=_cparams(2),
        name=name,
    )(x, g.reshape(1, d), w, cos, sin)


def _group_norm_gate(o, g, gn):
    mu = jnp.mean(o, axis=-1, keepdims=True)
    oc = o - mu
    var = jnp.mean(oc * oc, axis=-1, keepdims=True)
    on = oc * lax.rsqrt(var + GN_EPS) * gn
    gf = g.astype(F32)
    return (gf / (1.0 + jnp.exp(-gf))) * on


def _retention_step(q, k, v, state, lg):
    L = q.shape[0]
    row = lax.broadcasted_iota(jnp.int32, (L, L), 0)
    col = lax.broadcasted_iota(jnp.int32, (L, L), 1)
    diff = (row - col).astype(F32)
    decay = jnp.where(diff >= 0.0, jnp.exp(jnp.maximum(diff, 0.0) * lg), 0.0)
    scores = lax.dot_general(q, k, _NT, preferred_element_type=F32) * decay
    inner = jnp.dot(scores.astype(BF16), v, preferred_element_type=F32)
    idx = lax.broadcasted_iota(jnp.int32, (L, 1), 0).astype(F32)
    q_decay = jnp.exp((idx + 1.0) * lg)
    cross = jnp.dot(q, state.astype(BF16), preferred_element_type=F32) * q_decay
    k_decay = jnp.exp((L - 1.0 - idx) * lg)
    kd = (k.astype(F32) * k_decay).T.astype(BF16)
    chunk_decay = jnp.exp(jnp.full((1, 1), L, F32) * lg)
    new_state = chunk_decay * state + jnp.dot(kd, v, preferred_element_type=F32)
    return inner + cross, new_state


def _ret_prompt_kernel(lg_ref, q_ref, k_ref, v_ref, g_ref, gn_ref, o_ref, s_ref):
    hg = pl.program_id(1)
    c = pl.program_id(2)
    hp = s_ref.shape[1]
    dk, dv = s_ref.shape[2], s_ref.shape[3]

    @pl.when(c == 0)
    def _():
        s_ref[...] = jnp.zeros_like(s_ref)

    steps = []
    for j in range(hp):
        qk_cols = slice(j * dk, (j + 1) * dk)
        v_cols = slice(j * dv, (j + 1) * dv)
        steps.append(_retention_step(q_ref[:, qk_cols], k_ref[:, qk_cols], v_ref[:, v_cols], s_ref[0, j],
                                     lg_ref[hg * hp + j]))
    for j, (o, new_state) in enumerate(steps):
        v_cols = slice(j * dv, (j + 1) * dv)
        s_ref[0, j] = new_state
        o_ref[:, v_cols] = _group_norm_gate(o, g_ref[:, v_cols], gn_ref[:, v_cols]).astype(o_ref.dtype)


def _ret_prompt(qkvg, gn, lg, *, batch, seq, heads, dk, dv):
    L = min(RET_CHUNK, seq)
    nc = seq // L
    hp = RET_HEADS_PER_STEP
    ng = heads // hp
    kq = ng
    v0 = 2 * heads * dk // (hp * dv)
    g0 = v0 + ng
    grid_spec = pltpu.PrefetchScalarGridSpec(
        num_scalar_prefetch=1,
        grid=(batch, ng, nc),
        in_specs=[
            pl.BlockSpec((L, hp * dk), lambda b, h, c, lg: (b * nc + c, h)),
            pl.BlockSpec((L, hp * dk), lambda b, h, c, lg: (b * nc + c, kq + h)),
            pl.BlockSpec((L, hp * dv), lambda b, h, c, lg: (b * nc + c, v0 + h)),
            pl.BlockSpec((L, hp * dv), lambda b, h, c, lg: (b * nc + c, g0 + h)),
            pl.BlockSpec((1, hp * dv), lambda b, h, c, lg: (0, h)),
        ],
        out_specs=[
            pl.BlockSpec((L, hp * dv), lambda b, h, c, lg: (b * nc + c, h)),
            pl.BlockSpec((1, hp, dk, dv), lambda b, h, c, lg: (b, h, 0, 0)),
        ],
    )
    return pl.pallas_call(
        _ret_prompt_kernel,
        grid_spec=grid_spec,
        out_shape=[
            jax.ShapeDtypeStruct((batch * seq, heads * dv), BF16),
            jax.ShapeDtypeStruct((batch, heads, dk, dv), F32),
        ],
        compiler_params=_cparams(3),
        name="ret_prompt",
    )(lg, qkvg, qkvg, qkvg, qkvg, gn.reshape(1, -1))


def _ret_sample_kernel(lg_ref, q_ref, k_ref, v_ref, g_ref, gn_ref, s_in_ref, o_ref, s_out_ref, *, t_len):
    h = pl.program_id(1)
    lg = lg_ref[h]
    nb = s_in_ref.shape[0]
    n = nb * t_len
    q = q_ref[...]
    k = k_ref[...]
    v = v_ref[...]
    g = g_ref[...]
    gn = gn_ref[...]
    row = lax.broadcasted_iota(jnp.int32, (n, n), 0)
    col = lax.broadcasted_iota(jnp.int32, (n, n), 1)
    diff = (row - col).astype(F32)
    keep = (row // t_len == col // t_len) & (row >= col)
    decay = jnp.where(keep, jnp.exp(jnp.maximum(diff, 0.0) * lg), 0.0)
    scores = lax.dot_general(q, k, _NT, preferred_element_type=F32) * decay
    inner = jnp.dot(scores.astype(BF16), v, preferred_element_type=F32)
    idx = (lax.broadcasted_iota(jnp.int32, (n, 1), 0) % t_len).astype(F32)
    q_decay = jnp.exp((idx + 1.0) * lg)
    kd = k.astype(F32) * jnp.exp((t_len - 1.0 - idx) * lg)
    chunk_decay = jnp.exp(jnp.full((1, 1), t_len, F32) * lg)
    group = BF16_ROWS // t_len
    seq_of_row = lax.broadcasted_iota(jnp.int32, (BF16_ROWS, 1), 0) // t_len
    for p in range(nb // group):
        rows = slice(p * BF16_ROWS, (p + 1) * BF16_ROWS)
        q16, kd16, v16 = q[rows], kd[rows], v[rows]
        cross = jnp.zeros((BF16_ROWS, v.shape[1]), F32)
        for j in range(group):
            b = p * group + j
            state = s_in_ref[b, 0]
            mine = seq_of_row == j
            cross = jnp.where(mine, jnp.dot(q16, state.astype(BF16), preferred_element_type=F32), cross)
            kdb = jnp.where(mine, kd16, 0.0).T.astype(BF16)
            s_out_ref[b, 0] = chunk_decay * state + jnp.dot(kdb, v16, preferred_element_type=F32)
        o = inner[rows] + cross * q_decay[rows]
        o_ref[rows, :] = _group_norm_gate(o, g[rows], gn).astype(o_ref.dtype)


def _ret_sample(qkvg, gn, lg, state, *, row0, t_len, heads, dk, dv):
    nbatch = state.shape[0]
    nb = 8
    rows = nb * t_len
    r0 = row0 // rows
    kq = heads
    v0 = 2 * heads * dk // dv
    g0 = v0 + heads
    grid_spec = pltpu.PrefetchScalarGridSpec(
        num_scalar_prefetch=1,
        grid=(nbatch // nb, heads),
        in_specs=[
            pl.BlockSpec((rows, dk), lambda i, h, lg: (r0 + i, h)),
            pl.BlockSpec((rows, dk), lambda i, h, lg: (r0 + i, kq + h)),
            pl.BlockSpec((rows, dv), lambda i, h, lg: (r0 + i, v0 + h)),
            pl.BlockSpec((rows, dv), lambda i, h, lg: (r0 + i, g0 + h)),
            pl.BlockSpec((1, dv), lambda i, h, lg: (0, h)),
            pl.BlockSpec((nb, 1, dk, dv), lambda i, h, lg: (i, h, 0, 0)),
        ],
        out_specs=[
            pl.BlockSpec((rows, dv), lambda i, h, lg: (i, h)),
            pl.BlockSpec((nb, 1, dk, dv), lambda i, h, lg: (i, h, 0, 0)),
        ],
    )
    return pl.pallas_call(
        functools.partial(_ret_sample_kernel, t_len=t_len),
        grid_spec=grid_spec,
        out_shape=[
            jax.ShapeDtypeStruct((nbatch * t_len, heads * dv), BF16),
            jax.ShapeDtypeStruct(state.shape, F32),
        ],
        compiler_params=_cparams(2),
        name="ret_sample",
    )(lg, qkvg, qkvg, qkvg, qkvg, gn.reshape(1, -1), state)


def _matmul_res_kernel(xp_ref, xs_ref, w_ref, r_ref, o_ref, *, prompt_tiles):
    i = pl.program_id(0)

    @pl.when(i < prompt_tiles)
    def _():
        o_ref[...] = r_ref[...] + jnp.dot(xp_ref[...], w_ref[...], preferred_element_type=F32)

    @pl.when(i >= prompt_tiles)
    def _():
        o_ref[...] = r_ref[...] + jnp.dot(xs_ref[...], w_ref[...], preferred_element_type=F32)


def _matmul_res(x_prompt, x_sample, w, res, *, name):
    n, d = res.shape
    k = w.shape[0]
    tm = ROW_TILE
    pt = x_prompt.shape[0] // tm
    return pl.pallas_call(
        functools.partial(_matmul_res_kernel, prompt_tiles=pt),
        grid=(n // tm,),
        in_specs=[
            pl.BlockSpec((tm, k), lambda i: (jnp.minimum(i, pt - 1), 0)),
            pl.BlockSpec((tm, k), lambda i: (jnp.maximum(i - pt, 0), 0)),
            pl.BlockSpec((k, d), lambda i: (0, 0)),
            pl.BlockSpec((tm, d), lambda i: (i, 0)),
        ],
        out_specs=pl.BlockSpec((tm, d), lambda i: (i, 0)),
        out_shape=jax.ShapeDtypeStruct((n, d), F32),
        compiler_params=_cparams(1),
        name=name,
    )(x_prompt, x_sample, w, res)


def _router_kernel(x_ref, g_ref, wh_ref, wl_ref, b_ref, idx_ref, gate_ref, *, n_experts):
    xn = _rms(x_ref[...], g_ref[...])
    hi = xn.astype(BF16)
    lo = (xn - hi.astype(F32)).astype(BF16)
    logits = (jnp.dot(hi, wh_ref[...], preferred_element_type=F32)
              + jnp.dot(lo, wh_ref[...], preferred_element_type=F32)
              + jnp.dot(hi, wl_ref[...], preferred_element_type=F32)) + b_ref[...]
    lane = lax.broadcasted_iota(jnp.int32, logits.shape, 1).astype(F32)
    neg = jnp.float32(-jnp.inf)
    work = jnp.where(lane < n_experts, logits, neg)
    vals, idxs = [], []
    for _ in range(TOP_K):
        m = jnp.max(work, axis=-1, keepdims=True)
        sel = jnp.min(jnp.where(work == m, lane, float(LANES)), axis=-1, keepdims=True)
        vals.append(m)
        idxs.append(sel)
        work = jnp.where(lane == sel, neg, work)
    es = [jnp.exp(v - vals[0]) for v in vals]
    denom = es[0]
    for e in es[1:]:
        denom = denom + e
    idx_out = jnp.zeros(logits.shape, F32)
    gate_out = jnp.zeros(logits.shape, F32)
    for kk in range(TOP_K):
        idx_out = jnp.where(lane == kk, idxs[kk], idx_out)
        gate_out = jnp.where(lane == kk, es[kk] / denom, gate_out)
    idx_ref[...] = idx_out.astype(jnp.int32)
    gate_ref[...] = gate_out


def _router(y, g, wr, br):
    n, d = y.shape
    e = wr.shape[1]
    tm = ROW_TILE
    wr_pad = jnp.zeros((d, LANES), F32).at[:, :e].set(wr)
    wh = wr_pad.astype(BF16)
    wl = (wr_pad - wh.astype(F32)).astype(BF16)
    b_pad = jnp.zeros((1, LANES), F32).at[0, :e].set(br)
    return pl.pallas_call(
        functools.partial(_router_kernel, n_experts=e),
        grid=(n // tm,),
        in_specs=[
            pl.BlockSpec((tm, d), lambda i: (i, 0)),
            pl.BlockSpec((1, d), lambda i: (0, 0)),
            pl.BlockSpec((d, LANES), lambda i: (0, 0)),
            pl.BlockSpec((d, LANES), lambda i: (0, 0)),
            pl.BlockSpec((1, LANES), lambda i: (0, 0)),
        ],
        out_specs=[
            pl.BlockSpec((tm, LANES), lambda i: (i, 0)),
            pl.BlockSpec((tm, LANES), lambda i: (i, 0)),
        ],
        out_shape=[
            jax.ShapeDtypeStruct((n, LANES), jnp.int32),
            jax.ShapeDtypeStruct((n, LANES), F32),
        ],
        compiler_params=_cparams(1),
        name="moe_router",
    )(y, g.reshape(1, d), wh, wl, b_pad)


def _split_gate_up_kernel(w_ref, p_ref, o_ref):
    half = o_ref.shape[1] // 2
    hp = MXU_DIM // 2
    for c in range(o_ref.shape[1] // MXU_DIM):
        y = jnp.dot(w_ref[:, c * MXU_DIM:(c + 1) * MXU_DIM].astype(BF16), p_ref[...], preferred_element_type=F32)
        o_ref[:, c * hp:(c + 1) * hp] = y[:, :hp].astype(BF16)
        o_ref[:, half + c * hp:half + (c + 1) * hp] = y[:, hp:].astype(BF16)


def _split_gate_up(wgu_all, layer):
    _, e, d, w2 = wgu_all.shape
    rows = e * d
    tm = ROW_TILE
    row0 = layer * rows // tm
    src = lax.broadcasted_iota(jnp.int32, (MXU_DIM, MXU_DIM), 0)
    dst = lax.broadcasted_iota(jnp.int32, (MXU_DIM, MXU_DIM), 1)
    perm = (dst == (src % 2) * (MXU_DIM // 2) + src // 2).astype(BF16)
    out = pl.pallas_call(
        _split_gate_up_kernel,
        grid=(rows // tm,),
        in_specs=[pl.BlockSpec((tm, w2), lambda i: (row0 + i, 0)),
                  pl.BlockSpec((MXU_DIM, MXU_DIM), lambda i: (0, 0))],
        out_specs=pl.BlockSpec((tm, w2), lambda i: (i, 0)),
        out_shape=jax.ShapeDtypeStruct((rows, w2), BF16),
        compiler_params=_cparams(1),
        name="moe_split_gate_up",
    )(wgu_all.reshape(-1, w2), perm)
    return out.reshape(e, d, w2)


def _row_copy(src, dst, sem, src_row, dst_row):
    return pltpu.make_async_copy(src.at[pl.ds(src_row, 1)], dst.at[pl.ds(dst_row, 1)], sem)


def _dispatch_kernel(dest_ref, hi_ref, y_ref, xb_hbm, zblk, sem, *, n_experts):
    i = pl.program_id(0)
    tm = y_ref.shape[0]
    blk = zblk.shape[0]
    n_blocks = xb_hbm.shape[0] // blk

    def zero_block(b, wait):
        cp = pltpu.make_async_copy(zblk, xb_hbm.at[pl.ds(pl.multiple_of(b * blk, blk), blk)], sem.at[1])
        cp.wait() if wait else cp.start()

    @pl.when(i == 0)
    def _():
        zblk[...] = jnp.zeros_like(zblk)
        for g in range(0, n_experts, ZERO_GROUP):
            for wait in (False, True):
                for e in range(g, min(g + ZERO_GROUP, n_experts)):
                    @pl.when(hi_ref[e] > (hi_ref[e - 1] if e else 0))
                    def _():
                        zero_block(hi_ref[e] // blk - 1, wait)

        first_unused = hi_ref[n_experts - 1] // blk

        def tail_group(gi, _):
            for wait in (False, True):
                for j in range(ZERO_GROUP):
                    b = first_unused + gi * ZERO_GROUP + j

                    @pl.when(b < n_blocks)
                    def _():
                        zero_block(b, wait)
            return 0

        lax.fori_loop(0, (n_blocks - first_unused + ZERO_GROUP - 1) // ZERO_GROUP, tail_group, 0)

    def scatter(wait):
        def body(t, _):
            for kk in range(TOP_K):
                cp = _row_copy(y_ref, xb_hbm, sem.at[0], t, dest_ref[(i * tm + t) * TOP_K + kk])
                cp.wait() if wait else cp.start()
            return 0
        lax.fori_loop(0, tm, body, 0, unroll=DMA_UNROLL // TOP_K)

    scatter(False)
    scatter(True)


def _dispatch(y, dest, region_end, n_rows):
    n, d = y.shape
    tm = DISPATCH_TILE
    grid_spec = pltpu.PrefetchScalarGridSpec(
        num_scalar_prefetch=2,
        grid=(n // tm,),
        in_specs=[pl.BlockSpec((tm, d), lambda i, dest, hi: (i, 0))],
        out_specs=pl.BlockSpec(memory_space=pl.ANY),
        scratch_shapes=[pltpu.VMEM((MOE_BLOCK, d), F32), pltpu.SemaphoreType.DMA((2,))],
    )
    return pl.pallas_call(
        functools.partial(_dispatch_kernel, n_experts=region_end.shape[0]),
        grid_spec=grid_spec,
        out_shape=jax.ShapeDtypeStruct((n_rows, d), F32),
        compiler_params=_cparams(1),
        name="moe_dispatch",
    )(dest, region_end, y)


def _expert_kernel(be_ref, nv_ref, x_ref, g_ref, wgu_ref, bgu_ref, wd_ref, bd_ref, o_ref, wd_bf, *, d_expert):
    i = pl.program_id(0)

    @pl.when((i == 0) | (be_ref[i] != be_ref[jnp.maximum(i - 1, 0)]))
    def _():
        wd_bf[...] = wd_ref[0, 0].astype(BF16)

    @pl.when(i < nv_ref[0])
    def _():
        rows = o_ref.shape[0] // EXPERT_ROW_GROUPS
        groups = [slice(r * rows, (r + 1) * rows) for r in range(EXPERT_ROW_GROUPS)]
        hgus = [jnp.dot(_rms(x_ref[grp, :], g_ref[...]).astype(BF16), wgu_ref[0], preferred_element_type=F32)
                + bgu_ref[0] for grp in groups]
        for grp, hgu in zip(groups, hgus):
            gate = jnp.minimum(hgu[:, :d_expert], SWIGLU_LIMIT)
            up = jnp.clip(hgu[:, d_expert:], -SWIGLU_LIMIT, SWIGLU_LIMIT)
            act = (up + 1.0) * gate * (1.0 / (1.0 + jnp.exp(-(gate * SWIGLU_ALPHA))))
            o_ref[grp, :] = jnp.dot(act.astype(BF16), wd_bf[...], preferred_element_type=F32) + bd_ref[0]

    @pl.when(i >= nv_ref[0])
    def _():
        o_ref[...] = jnp.zeros_like(o_ref)


def _experts(xb, g, block_expert, n_valid, wgu, bgu, wd_all, layer, bd):
    r, d = xb.shape
    de = wd_all.shape[2]
    blk = MOE_BLOCK
    grid_spec = pltpu.PrefetchScalarGridSpec(
        num_scalar_prefetch=2,
        grid=(r // blk,),
        in_specs=[
            pl.BlockSpec((blk, d), lambda i, be, nv: (jnp.minimum(i, nv[0] - 1), 0)),
            pl.BlockSpec((1, d), lambda i, be, nv: (0, 0)),
            pl.BlockSpec((1, d, 2 * de), lambda i, be, nv: (be[i], 0, 0)),
            pl.BlockSpec((1, 1, 2 * de), lambda i, be, nv: (be[i], 0, 0)),
            pl.BlockSpec((1, 1, de, d), lambda i, be, nv: (layer, be[i], 0, 0)),
            pl.BlockSpec((1, 1, d), lambda i, be, nv: (be[i], 0, 0)),
        ],
        out_specs=pl.BlockSpec((blk, d), lambda i, be, nv: (i, 0)),
        scratch_shapes=[pltpu.VMEM((de, d), BF16)],
    )
    return pl.pallas_call(
        functools.partial(_expert_kernel, d_expert=de),
        grid_spec=grid_spec,
        out_shape=jax.ShapeDtypeStruct((r, d), F32),
        compiler_params=_cparams(1),
        name="moe_experts",
    )(block_expert, n_valid, xb, g.reshape(1, d), wgu, bgu, wd_all, bd)


def _combine_kernel(dest_ref, res_ref, gate_ref, yb_hbm, o_ref, buf, sem):
    i = pl.program_id(0)
    nsteps = pl.num_programs(0)
    tm = o_ref.shape[0]
    slot = i % 2

    def gather(tile, sl, wait):
        def body(t, _):
            for kk in range(TOP_K):
                cp = _row_copy(yb_hbm, buf.at[sl, kk], sem.at[sl], dest_ref[(tile * tm + t) * TOP_K + kk], t)
                cp.wait() if wait else cp.start()
            return 0
        lax.fori_loop(0, tm, body, 0, unroll=DMA_UNROLL // TOP_K)

    @pl.when(i == 0)
    def _():
        gather(0, 0, False)

    @pl.when(i + 1 < nsteps)
    def _():
        gather(i + 1, 1 - slot, False)

    gather(i, slot, True)
    acc = res_ref[...]
    gates = gate_ref[...]
    for kk in range(TOP_K):
        acc = acc + gates[:, kk:kk + 1] * buf[slot, kk]
    o_ref[...] = acc


def _combine(res, gate_pad, dest, yb):
    n, d = res.shape
    tm = COMBINE_TILE
    grid_spec = pltpu.PrefetchScalarGridSpec(
        num_scalar_prefetch=1,
        grid=(n // tm,),
        in_specs=[
            pl.BlockSpec((tm, d), lambda i, dest: (i, 0)),
            pl.BlockSpec((tm, LANES), lambda i, dest: (i, 0)),
            pl.BlockSpec(memory_space=pl.ANY),
        ],
        out_specs=pl.BlockSpec((tm, d), lambda i, dest: (i, 0)),
        scratch_shapes=[pltpu.VMEM((2, TOP_K, tm, d), F32), pltpu.SemaphoreType.DMA((2,))],
    )
    return pl.pallas_call(
        _combine_kernel,
        grid_spec=grid_spec,
        out_shape=jax.ShapeDtypeStruct((n, d), F32),
        compiler_params=_cparams(1),
        name="moe_combine",
    )(dest, res, gate_pad, yb)


def _moe(y, g, wr, br, wgu_all, bgu, wd_all, bd, layer):
    n, d = y.shape
    e = wr.shape[1]
    de = wd_all.shape[2]
    idx_pad, gate_pad = _router(y, g, wr, br)

    a = n * TOP_K
    blk = MOE_BLOCK
    flat_e = idx_pad[:, :TOP_K].reshape(a)
    onehot = (flat_e[:, None] == jnp.arange(e, dtype=jnp.int32)[None, :]).astype(jnp.int32)
    csum = jnp.cumsum(onehot, axis=0)
    counts = csum[-1]
    rank = jnp.take_along_axis(csum, flat_e[:, None], axis=1)[:, 0] - 1
    padded = ((counts + blk - 1) // blk) * blk
    ends = jnp.cumsum(padded)
    dest = ((ends - padded)[flat_e] + rank).astype(jnp.int32)
    n_blocks = -(-a // blk) + e
    r = n_blocks * blk
    block_start = jnp.arange(n_blocks, dtype=jnp.int32) * blk
    block_expert = jnp.minimum(jnp.sum((ends[None, :] <= block_start[:, None]).astype(jnp.int32), axis=1), e - 1)
    n_valid = (ends[-1:] // blk).astype(jnp.int32)

    xb = _dispatch(y, dest, ends.astype(jnp.int32), r)
    wgu_b = _split_gate_up(wgu_all, layer)
    bgu_b = jnp.concatenate([bgu[:, 0::2], bgu[:, 1::2]], axis=-1).reshape(e, 1, 2 * de)
    yb = _experts(xb, g, block_expert, n_valid, wgu_b, bgu_b, wd_all, layer, bd.reshape(e, 1, d))
    return _combine(y, gate_pad, dest, yb)


def _rope_lanes(x, c, sa, sb, half):
    n = x.shape[1]
    return x * c + pltpu.roll(x, n - half, 1) * sa + pltpu.roll(x, half, 1) * sb


def _mla_proj_kernel(dqkv_ref, gq_ref, gkv_ref, wuq_ref, wuk_ref, wuv_ref, ones_ref, c_ref, sa_ref, sb_ref,
                     q_ref, k_ref, v_ref, ckv_ref, kpe_ref, *, q_rank, kv_rank, heads, rope_half, scale):
    x = dqkv_ref[...]
    c = c_ref[...]
    sa = sa_ref[...]
    sb = sb_ref[...]
    cq = _rms(x[:, :q_rank], gq_ref[...]).astype(BF16)
    ckv = _rms(x[:, q_rank:q_rank + kv_rank], gkv_ref[...])
    ckv_ref[...] = ckv
    kpe = _rope_lanes(x[:, q_rank + kv_rank:], c, sa, sb, rope_half)
    kpe_ref[...] = kpe
    ckv_b = ckv.astype(BF16)
    q = jnp.dot(cq, wuq_ref[...], preferred_element_type=F32)
    k = jnp.dot(ckv_b, wuk_ref[...], preferred_element_type=F32)
    for h in range(heads):
        cols = slice(h * LANES, (h + 1) * LANES)
        q_ref[:, cols] = (_rope_lanes(q[:, cols], c, sa, sb, rope_half) * scale).astype(BF16)
        k_ref[:, cols] = (k[:, cols] + kpe).astype(BF16)
    v_ref[...] = (jnp.dot(ckv_b, wuv_ref[...], preferred_element_type=F32) + ones_ref[...]).astype(BF16)


def _mla_proj(dqkv, gq, gkv, wuq_p, wuk_p, wuv_p, ones_p, c, sa, sb, *, n_prompt, seq, q_rank, kv_rank, heads,
              rope_half, scale):
    n, w = dqkv.shape
    tm = ROW_TILE
    pt, st = n_prompt // tm, seq // tm
    hv = wuv_p.shape[1]
    full = lambda shape: pl.BlockSpec(shape, lambda i: (0,) * len(shape))
    rows = lambda width: pl.BlockSpec((tm, width), lambda i: (i, 0))
    table = pl.BlockSpec((tm, LANES), lambda i: (_table_tile(i, pt, st), 0))
    return pl.pallas_call(
        functools.partial(_mla_proj_kernel, q_rank=q_rank, kv_rank=kv_rank, heads=heads, rope_half=rope_half,
                          scale=scale),
        grid=(n // tm,),
        in_specs=[rows(w), full((1, q_rank)), full((1, kv_rank)), full(wuq_p.shape), full(wuk_p.shape),
                  full(wuv_p.shape), full((1, hv)), table, table, table],
        out_specs=[rows(heads * LANES), rows(heads * LANES), rows(hv), rows(kv_rank), rows(LANES)],
        out_shape=[
            jax.ShapeDtypeStruct((n, heads * LANES), BF16),
            jax.ShapeDtypeStruct((n, heads * LANES), BF16),
            jax.ShapeDtypeStruct((n, hv), BF16),
            jax.ShapeDtypeStruct((n, kv_rank), F32),
            jax.ShapeDtypeStruct((n, LANES), F32),
        ],
        compiler_params=_cparams(1),
        name="mla_proj",
    )(dqkv, gq.reshape(1, -1), gkv.reshape(1, -1), wuq_p, wuk_p, wuv_p, ones_p, c, sa, sb)


def _flash_kernel(q_ref, k_ref, v_ref, o_ref, *scratch, tile, heads, v_head):
    m_refs, acc_refs = scratch[:heads], scratch[heads:]
    qi = pl.program_id(2)
    for h in range(heads):
        m_refs[h][...] = jnp.full((tile, LANES), -jnp.inf, F32)
        acc_refs[h][...] = jnp.zeros((tile, LANES), F32)
    nchunk = tile // LANES

    def step(ki, masked):
        start = pl.multiple_of(ki * tile, tile)
        scores = []
        for h in range(heads):
            cols = slice(h * LANES, (h + 1) * LANES)
            s = lax.dot_general(q_ref[:, cols], k_ref[pl.ds(start, tile), cols], _NT, preferred_element_type=F32)
            if masked:
                row = lax.broadcasted_iota(jnp.int32, s.shape, 0)
                col = lax.broadcasted_iota(jnp.int32, s.shape, 1)
                s = jnp.where(col <= row, s, jnp.finfo(F32).min)
            scores.append(s)
        for h in range(heads):
            cols = slice(h * LANES, (h + 1) * LANES)
            chunks = [scores[h][:, c * LANES:(c + 1) * LANES] for c in range(nchunk)]
            part = chunks[0]
            for ch in chunks[1:]:
                part = jnp.maximum(part, ch)
            m_old = m_refs[h][...]
            m_new = jnp.maximum(m_old, jnp.max(part, axis=-1, keepdims=True))
            m_refs[h][...] = m_new
            p = jnp.concatenate([jnp.exp2(ch - m_new) for ch in chunks], axis=1).astype(BF16)
            acc_refs[h][...] = (jnp.exp2(m_old - m_new) * acc_refs[h][...]
                                + jnp.dot(p, v_ref[pl.ds(start, tile), cols], preferred_element_type=F32))

    def body(kp, carry):
        step(2 * kp, False)
        step(2 * kp + 1, False)
        return carry

    lax.fori_loop(0, qi // 2, body, 0)

    @pl.when(qi % 2 == 1)
    def _():
        step(qi - 1, False)

    step(qi, True)
    lane = lax.broadcasted_iota(jnp.int32, (tile, LANES), 1)
    for j in range(heads // 2):
        even = acc_refs[2 * j][...]
        odd = acc_refs[2 * j + 1][...]
        num = jnp.where(lane < v_head, even, odd)
        den = jnp.where(lane < v_head, pltpu.roll(even, v_head, 1), pltpu.roll(odd, v_head, 1))
        o_ref[:, j * LANES:(j + 1) * LANES] = (num / den).astype(o_ref.dtype)


def _flash(q_cat, k_cat, v_ext, *, batch, seq, heads, v_head):
    tile = min(ATTN_TILE, seq)
    nq = seq // tile
    hg = ATTN_HEADS
    return pl.pallas_call(
        functools.partial(_flash_kernel, tile=tile, heads=hg, v_head=v_head),
        grid=(batch, heads // hg, nq),
        in_specs=[
            pl.BlockSpec((tile, hg * LANES), lambda b, g, qi: (b * nq + qi, g)),
            pl.BlockSpec((seq, hg * LANES), lambda b, g, qi: (b, g)),
            pl.BlockSpec((seq, hg * LANES), lambda b, g, qi: (b, g)),
        ],
        out_specs=pl.BlockSpec((tile, hg * v_head), lambda b, g, qi: (b * nq + qi, g)),
        out_shape=jax.ShapeDtypeStruct((batch * seq, heads * v_head), BF16),
        scratch_shapes=[pltpu.VMEM((tile, LANES), F32)] * (2 * hg),
        compiler_params=_cparams(3),
        name="mla_flash",
    )(q_cat, k_cat, v_ext)


def _qlat_kernel(q_ref, w_ref, o_ref, *, heads, kv_rank):
    for h in range(heads):
        o_ref[:, h * kv_rank:(h + 1) * kv_rank] = jnp.dot(
            q_ref[:, h * LANES:(h + 1) * LANES], w_ref[h], preferred_element_type=F32).astype(o_ref.dtype)


def _qlat(q_cat, wukt_p, *, row0, nrows, heads, kv_rank):
    tm = min(ROW_TILE, nrows)
    r0 = row0 // tm
    return pl.pallas_call(
        functools.partial(_qlat_kernel, heads=heads, kv_rank=kv_rank),
        grid=(nrows // tm,),
        in_specs=[
            pl.BlockSpec((tm, heads * LANES), lambda i: (r0 + i, 0)),
            pl.BlockSpec(wukt_p.shape, lambda i: (0, 0, 0)),
        ],
        out_specs=pl.BlockSpec((tm, heads * kv_rank), lambda i: (i, 0)),
        out_shape=jax.ShapeDtypeStruct((nrows, heads * kv_rank), BF16),
        compiler_params=_cparams(1),
        name="mla_qlat",
    )(q_cat, wukt_p)


def _olat_kernel(x_ref, w_ref, o_ref, *, pairs, kv_rank):
    for p in range(pairs):
        o_ref[:, p * LANES:(p + 1) * LANES] = jnp.dot(
            x_ref[:, p * 2 * kv_rank:(p + 1) * 2 * kv_rank], w_ref[p], preferred_element_type=F32).astype(o_ref.dtype)


def _olat(o_lat, wuv_pair, *, heads, kv_rank):
    n = o_lat.shape[0]
    tm = min(ROW_TILE, n)
    return pl.pallas_call(
        functools.partial(_olat_kernel, pairs=heads // 2, kv_rank=kv_rank),
        grid=(n // tm,),
        in_specs=[
            pl.BlockSpec((tm, heads * kv_rank), lambda i: (i, 0)),
            pl.BlockSpec(wuv_pair.shape, lambda i: (0, 0, 0)),
        ],
        out_specs=pl.BlockSpec((tm, (heads // 2) * LANES), lambda i: (i, 0)),
        out_shape=jax.ShapeDtypeStruct((n, (heads // 2) * LANES), BF16),
        compiler_params=_cparams(1),
        name="mla_olat",
    )(o_lat, wuv_pair)


def _paged_copies(pt_ref, cache_ckv, cache_kpe, ckv_buf, kpe_buf, sem, b, slot, p, page):
    pg = pt_ref[b, p]
    return (
        pltpu.make_async_copy(cache_ckv.at[0, pg], ckv_buf.at[slot, pl.ds(p * page, page)], sem.at[0, slot]),
        pltpu.make_async_copy(cache_kpe.at[0, pg], kpe_buf.at[slot, :, pl.ds(p * page, page)], sem.at[1, slot]),
    )


def _paged_attn_kernel(pt_ref, qlat_ref, qpe_ref, cnew_ref, pnew_ref, cache_ckv, cache_kpe, o_ref,
                       ckv_buf, kpe_buf, sem, *, n_pages, page, heads):
    b = pl.program_id(0)
    nb = pl.num_programs(0)
    slot = b % 2

    def start_all(bb, sl):
        def body(p, _):
            for cp in _paged_copies(pt_ref, cache_ckv, cache_kpe, ckv_buf, kpe_buf, sem, bb, sl, p, page):
                cp.start()
            return 0
        lax.fori_loop(0, n_pages, body, 0)

    @pl.when(b == 0)
    def _():
        start_all(0, 0)

    @pl.when(b + 1 < nb)
    def _():
        start_all(b + 1, 1 - slot)

    def wait_body(p, _):
        for cp in _paged_copies(pt_ref, cache_ckv, cache_kpe, ckv_buf, kpe_buf, sem, b, slot, p, page):
            cp.wait()
        return 0
    lax.fori_loop(0, n_pages, wait_body, 0)

    qlat = qlat_ref[0]
    qpe = qpe_ref[0]
    ck = ckv_buf[slot].astype(BF16)
    kp = kpe_buf[slot].astype(BF16)
    s_past = (lax.dot_general(qlat, ck, _NT, preferred_element_type=F32)
              + jnp.dot(qpe, kp, preferred_element_type=F32))
    cn = cnew_ref[0].astype(BF16)
    pn = pnew_ref[0].astype(BF16)
    s_new = (lax.dot_general(qlat, cn, _NT, preferred_element_type=F32)
             + lax.dot_general(qpe, pn, _NT, preferred_element_type=F32))
    t_of_row = lax.broadcasted_iota(jnp.int32, s_new.shape, 0) // heads
    j = lax.broadcasted_iota(jnp.int32, s_new.shape, 1)
    s_new = jnp.where(j <= t_of_row, s_new, jnp.finfo(F32).min)
    m = jnp.maximum(jnp.max(s_past, axis=-1, keepdims=True), jnp.max(s_new, axis=-1, keepdims=True))
    p_past = jnp.exp2(s_past - m)
    p_new = jnp.exp2(s_new - m)
    denom = jnp.sum(p_past, axis=-1, keepdims=True) + jnp.sum(p_new, axis=-1, keepdims=True)
    o = (jnp.dot(p_past.astype(BF16), ck, preferred_element_type=F32)
         + jnp.dot(p_new.astype(BF16), cn, preferred_element_type=F32))
    o_ref[0] = (o / denom).astype(o_ref.dtype)


def _paged_attn(page_table, qlat, qpe, ckv_new, kpe_new, cache_ckv, cache_kpe, *, heads):
    nbatch, n_pages = page_table.shape
    page, kv_rank = cache_ckv.shape[2], cache_ckv.shape[3]
    rope = cache_kpe.shape[2]
    rows = qlat.shape[1]
    tpad = ckv_new.shape[1]
    grid_spec = pltpu.PrefetchScalarGridSpec(
        num_scalar_prefetch=1,
        grid=(nbatch,),
        in_specs=[
            pl.BlockSpec((1, rows, kv_rank), lambda b, pt: (b, 0, 0)),
            pl.BlockSpec((1, rows, rope), lambda b, pt: (b, 0, 0)),
            pl.BlockSpec((1, tpad, kv_rank), lambda b, pt: (b, 0, 0)),
            pl.BlockSpec((1, tpad, rope), lambda b, pt: (b, 0, 0)),
            pl.BlockSpec(memory_space=pl.ANY),
            pl.BlockSpec(memory_space=pl.ANY),
        ],
        out_specs=pl.BlockSpec((1, rows, kv_rank), lambda b, pt: (b, 0, 0)),
        scratch_shapes=[
            pltpu.VMEM((2, n_pages * page, kv_rank), F32),
            pltpu.VMEM((2, rope, n_pages * page), F32),
            pltpu.SemaphoreType.DMA((2, 2)),
        ],
    )
    return pl.pallas_call(
        functools.partial(_paged_attn_kernel, n_pages=n_pages, page=page, heads=heads),
        grid_spec=grid_spec,
        out_shape=jax.ShapeDtypeStruct((nbatch, rows, kv_rank), BF16),
        compiler_params=_cparams(1),
        name="mla_paged_attn",
    )(page_table, qlat, qpe, ckv_new, kpe_new, cache_ckv, cache_kpe)


def _rmsnorm_kernel(x_ref, g_ref, op_ref, os_ref, *, prompt_tiles):
    i = pl.program_id(0)
    y = _rms(x_ref[...], g_ref[...])

    @pl.when(i < prompt_tiles)
    def _():
        op_ref[...] = y

    @pl.when(i >= prompt_tiles)
    def _():
        os_ref[...] = y


def _rmsnorm(x, g, n_prompt):
    n, d = x.shape
    tm = ROW_TILE
    pt = n_prompt // tm
    return pl.pallas_call(
        functools.partial(_rmsnorm_kernel, prompt_tiles=pt),
        grid=(n // tm,),
        in_specs=[pl.BlockSpec((tm, d), lambda i: (i, 0)), pl.BlockSpec((1, d), lambda i: (0, 0))],
        out_specs=[
            pl.BlockSpec((tm, d), lambda i: (jnp.minimum(i, pt - 1), 0)),
            pl.BlockSpec((tm, d), lambda i: (jnp.maximum(i - pt, 0), 0)),
        ],
        out_shape=[jax.ShapeDtypeStruct((n_prompt, d), F32), jax.ShapeDtypeStruct((n - n_prompt, d), F32)],
        compiler_params=_cparams(1),
        name="final_norm",
    )(x, g.reshape(1, d))


def _rope_angles(pos, half):
    inv = ROPE_BASE ** (-jnp.arange(half, dtype=F32) / half)
    return pos.astype(F32)[:, None] * inv[None, :]


def _table_positions(seq, dec_seq, past_len, tm):
    return jnp.concatenate([jnp.arange(seq), jnp.tile(past_len + jnp.arange(dec_seq), tm // dec_seq)])


def kernel(x_prompt, x_sample, state_ret, cache_ckv, cache_kpe, page_table, norm_mix, norm_ffn, norm_final, ret_wq, ret_wk, ret_wv, ret_wg, ret_gn, ret_wo, mla_wdq, mla_gq, mla_wuq, mla_wdkv, mla_gkv, mla_wuk, mla_wuv, mla_wo, moe_wr, moe_br, moe_wgu, moe_bgu, moe_wd, moe_bd):
    batch, seq, d = x_prompt.shape
    dec_batch, dec_seq, _ = x_sample.shape
    ret_heads, dk, dv = state_ret.shape[2], state_ret.shape[3], state_ret.shape[4]
    page = cache_ckv.shape[2]
    kv_rank = cache_ckv.shape[3]
    rope_dim = cache_kpe.shape[3]
    past_len = page_table.shape[1] * page
    mla_heads, nope = mla_wuk.shape[2], mla_wuk.shape[3]
    v_head = mla_wuv.shape[3]
    q_rank = mla_wdq.shape[2]
    n_prompt = batch * seq
    n_sample = dec_batch * dec_seq
    assert nope + rope_dim <= LANES and 2 * v_head == LANES and dk == 2 * LANES

    y = jnp.concatenate([x_prompt.reshape(n_prompt, d), x_sample.reshape(n_sample, d)], axis=0)
    n_all = n_prompt + n_sample
    assert n_sample % _proj_tile(n_all) == 0 and seq % _proj_tile(n_all) == 0 and ROW_TILE % dec_seq == 0

    ang = _rope_angles(_table_positions(seq, dec_seq, past_len, _proj_tile(n_all)), dk // 2)
    w_all = jnp.concatenate([ret_wq[0], ret_wk[0] * (dk ** -0.5), ret_wv[0], ret_wg[0]], axis=1).astype(BF16)
    qkvg = _norm_matmul(y, norm_mix[0], w_all, jnp.cos(ang), jnp.sin(ang), n_prompt=n_prompt, seq=seq,
                        tn=ret_heads * dk, n_rope_tiles=2, head_dim=dk, out_dtype=BF16, name="ret_proj")
    lg = jnp.log(1.0 - 2.0 ** (-5.0 - jnp.arange(ret_heads, dtype=F32)))
    gated_p, state_p = _ret_prompt(qkvg, ret_gn[0], lg, batch=batch, seq=seq, heads=ret_heads, dk=dk, dv=dv)
    gated_s, state_s = _ret_sample(qkvg, ret_gn[0], lg, state_ret[0], row0=n_prompt, t_len=dec_seq,
                                   heads=ret_heads, dk=dk, dv=dv)
    y = _matmul_res(gated_p, gated_s, ret_wo[0].astype(BF16), y, name="ret_out")
    y = _moe(y, norm_ffn[0], moe_wr[0], moe_br[0], moe_wgu, moe_bgu[0], moe_wd, moe_bd[0], 0)

    half = rope_dim // 2
    ang = _rope_angles(_table_positions(seq, dec_seq, past_len, ROW_TILE), half)
    cos2 = jnp.concatenate([jnp.cos(ang), jnp.cos(ang)], axis=1)
    sin = jnp.sin(ang)
    ones = jnp.ones((ang.shape[0], LANES), F32)
    zeros = jnp.zeros((ang.shape[0], LANES), F32)
    c_tab = lax.dynamic_update_slice(ones, cos2, (0, nope))
    sa_tab = lax.dynamic_update_slice(zeros, -sin, (0, nope))
    sb_tab = lax.dynamic_update_slice(zeros, sin, (0, nope + half))

    wdkv = mla_wdkv[0]
    kpe_cols = jnp.zeros((d, LANES), F32).at[:, nope:nope + rope_dim].set(wdkv[:, kv_rank:])
    w_down = jnp.concatenate([mla_wdq[0], wdkv[:, :kv_rank], kpe_cols], axis=1).astype(BF16)
    no_rope = jnp.zeros((seq + _proj_tile(n_all), LANES), F32)
    dqkv = _norm_matmul(y, norm_mix[1], w_down, no_rope, no_rope, n_prompt=n_prompt, seq=seq, tn=w_down.shape[1],
                        n_rope_tiles=0, head_dim=0, out_dtype=F32, name="mla_down")

    qk_dim = nope + rope_dim
    wuq_p = jnp.zeros((q_rank, mla_heads, LANES), F32).at[:, :, :qk_dim].set(
        mla_wuq[0].reshape(q_rank, mla_heads, qk_dim)).reshape(q_rank, mla_heads * LANES).astype(BF16)
    wuk_p = jnp.zeros((kv_rank, mla_heads, LANES), F32).at[:, :, :nope].set(mla_wuk[0]).reshape(
        kv_rank, mla_heads * LANES).astype(BF16)
    wuv_pairs = mla_wuv[0].reshape(kv_rank, mla_heads // 2, 2, v_head)
    wuv_p = jnp.zeros((kv_rank, mla_heads // 2, 2, 2, v_head), F32)
    wuv_p = wuv_p.at[:, :, 0, 0].set(wuv_pairs[:, :, 0]).at[:, :, 1, 1].set(wuv_pairs[:, :, 1])
    wuv_p = wuv_p.reshape(kv_rank, mla_heads * LANES).astype(BF16)
    ones_p = jnp.zeros((mla_heads // 2, 2, 2, v_head), F32).at[:, 0, 1].set(1.0).at[:, 1, 0].set(1.0)
    ones_p = ones_p.reshape(1, mla_heads * LANES)
    scale = (float(qk_dim) ** -0.5) * math.log2(math.e)
    q_cat, k_cat, v_ext, ckv_all, kpe_all = _mla_proj(
        dqkv, mla_gq[0], mla_gkv[0], wuq_p, wuk_p, wuv_p, ones_p, c_tab, sa_tab, sb_tab,
        n_prompt=n_prompt, seq=seq, q_rank=q_rank, kv_rank=kv_rank, heads=mla_heads, rope_half=half, scale=scale)
    kpe_rows = kpe_all[:, nope:nope + rope_dim]

    o_p = _flash(q_cat, k_cat, v_ext, batch=batch, seq=seq, heads=mla_heads, v_head=v_head)

    wukt_p = jnp.zeros((mla_heads, LANES, kv_rank), F32).at[:, :nope, :].set(
        jnp.transpose(mla_wuk[0], (1, 2, 0))).astype(BF16)
    q_lat = _qlat(q_cat, wukt_p, row0=n_prompt, nrows=n_sample, heads=mla_heads, kv_rank=kv_rank)
    rows = dec_seq * mla_heads
    q_pe = q_cat[n_prompt:].reshape(n_sample, mla_heads, LANES)[:, :, nope:qk_dim].reshape(dec_batch, rows, rope_dim)
    tpad = BF16_ROWS
    ckv_new = jnp.zeros((dec_batch, tpad, kv_rank), F32).at[:, :dec_seq].set(
        ckv_all[n_prompt:].reshape(dec_batch, dec_seq, kv_rank))
    kpe_new = jnp.zeros((dec_batch, tpad, rope_dim), F32).at[:, :dec_seq].set(
        kpe_rows[n_prompt:].reshape(dec_batch, dec_seq, rope_dim))
    o_lat = _paged_attn(page_table, q_lat.reshape(dec_batch, rows, kv_rank), q_pe, ckv_new, kpe_new,
                        cache_ckv, jnp.swapaxes(cache_kpe, 2, 3), heads=mla_heads)
    wuv_h = jnp.transpose(mla_wuv[0], (1, 0, 2)).reshape(mla_heads // 2, 2, kv_rank, v_head)
    wuv_pair = jnp.zeros((mla_heads // 2, 2, kv_rank, 2, v_head), F32)
    wuv_pair = wuv_pair.at[:, 0, :, 0, :].set(wuv_h[:, 0]).at[:, 1, :, 1, :].set(wuv_h[:, 1])
    wuv_pair = wuv_pair.reshape(mla_heads // 2, 2 * kv_rank, LANES).astype(BF16)
    o_s = _olat(o_lat.reshape(n_sample, mla_heads * kv_rank), wuv_pair, heads=mla_heads, kv_rank=kv_rank)

    y = _matmul_res(o_p, o_s, mla_wo[0].astype(BF16), y, name="mla_out")
    y = _moe(y, norm_ffn[1], moe_wr[1], moe_br[1], moe_wgu, moe_bgu[1], moe_wd, moe_bd[1], 1)

    y_p, y_s = _rmsnorm(y, norm_final, n_prompt)
    return (
        y_p.reshape(batch, seq, d),
        y_s.reshape(dec_batch, dec_seq, d),
        state_p[None],
        state_s[None],
        ckv_all[:n_prompt].reshape(1, batch, seq, kv_rank),
        kpe_rows[:n_prompt].reshape(1, batch, seq, rope_dim),
        ckv_all[n_prompt:].reshape(1, dec_batch, dec_seq, kv_rank),
        kpe_rows[n_prompt:].reshape(1, dec_batch, dec_seq, rope_dim),
    )
```

```python
import functools
import math

import jax
import jax.numpy as jnp
from jax import lax
from jax.experimental import pallas as pl
from jax.experimental.pallas import tpu as pltpu

F32 = jnp.float32
BF16 = jnp.bfloat16

TOP_K = 4
SWIGLU_LIMIT = 7.0
SWIGLU_ALPHA = 1.702
NORM_EPS = 1e-6
GN_EPS = 1e-6
ROPE_BASE = 10000.0
LANES = 128
BF16_ROWS = 16
MXU_DIM = 256

ROW_TILE = 512
PROJ_ROW_TILE = 1024
PROJ_ROW_GROUPS = 2
RET_CHUNK = 256
RET_HEADS_PER_STEP = 2
MOE_BLOCK = 512
EXPERT_ROW_GROUPS = 2
COMBINE_TILE = 128
DISPATCH_TILE = 256
ZERO_GROUP = 4
ATTN_TILE = 512
ATTN_HEADS = 4
DMA_UNROLL = 64
VMEM_LIMIT = 56 * 1024 * 1024

_NT = (((1,), (1,)), ((), ()))


def _cparams(n_axes, vmem=VMEM_LIMIT):
    return pltpu.CompilerParams(dimension_semantics=("arbitrary",) * n_axes, vmem_limit_bytes=vmem)


def _rms(x, g):
    return x * lax.rsqrt(jnp.mean(x * x, axis=-1, keepdims=True) + NORM_EPS) * g


def _norm_matmul_kernel(x_ref, g_ref, w_ref, cos_ref, sin_ref, o_ref, h_ref, *, n_rope_tiles, head_dim):
    j = pl.program_id(1)

    @pl.when(j == 0)
    def _():
        h_ref[...] = _rms(x_ref[...], g_ref[...]).astype(BF16)

    rows = o_ref.shape[0] // PROJ_ROW_GROUPS
    groups = [slice(r * rows, (r + 1) * rows) for r in range(PROJ_ROW_GROUPS)]
    accs = [jnp.dot(h_ref[grp, :], w_ref[...], preferred_element_type=F32) for grp in groups]
    tn = o_ref.shape[1]
    for grp, acc in zip(groups, accs):
        if n_rope_tiles:
            roped = (j < n_rope_tiles).astype(F32)
            cos = roped * cos_ref[grp, :] + (1.0 - roped)
            sin = roped * sin_ref[grp, :]
            half = head_dim // 2
            for h in range(tn // head_dim):
                x1 = acc[:, h * head_dim:h * head_dim + half]
                x2 = acc[:, h * head_dim + half:(h + 1) * head_dim]
                o_ref[grp, h * head_dim:h * head_dim + half] = (x1 * cos - x2 * sin).astype(o_ref.dtype)
                o_ref[grp, h * head_dim + half:(h + 1) * head_dim] = (x1 * sin + x2 * cos).astype(o_ref.dtype)
        else:
            o_ref[grp, :] = acc.astype(o_ref.dtype)


def _proj_tile(n):
    return PROJ_ROW_TILE if n % PROJ_ROW_TILE == 0 else ROW_TILE


def _table_tile(i, prompt_tiles, seq_tiles):
    return jnp.where(i < prompt_tiles, i % seq_tiles, seq_tiles)


def _norm_matmul(x, g, w, cos, sin, *, n_prompt, seq, tn, n_rope_tiles, head_dim, out_dtype, name):
    n, d = x.shape
    nout = w.shape[1]
    tm = _proj_tile(n)
    pt, st = n_prompt // tm, seq // tm
    return pl.pallas_call(
        functools.partial(_norm_matmul_kernel, n_rope_tiles=n_rope_tiles, head_dim=head_dim),
        grid=(n // tm, nout // tn),
        in_specs=[
            pl.BlockSpec((tm, d), lambda i, j: (i, 0)),
            pl.BlockSpec((1, d), lambda i, j: (0, 0)),
            pl.BlockSpec((d, tn), lambda i, j: (0, j)),
            pl.BlockSpec((tm, cos.shape[1]), lambda i, j: (_table_tile(i, pt, st), 0)),
            pl.BlockSpec((tm, sin.shape[1]), lambda i, j: (_table_tile(i, pt, st), 0)),
        ],
        out_specs=pl.BlockSpec((tm, tn), lambda i, j: (i, j)),
        out_shape=jax.ShapeDtypeStruct((n, nout), out_dtype),
        scratch_shapes=[pltpu.VMEM((tm, d), BF16)],
        compiler_params=_cparams(2),
        name=name,
    )(x, g.reshape(1, d), w, cos, sin)


def _group_norm_gate(o, g, gn):
    mu = jnp.mean(o, axis=-1, keepdims=True)
    oc = o - mu
    var = jnp.mean(oc * oc, axis=-1, keepdims=True)
    on = oc * lax.rsqrt(var + GN_EPS) * gn
    gf = g.astype(F32)
    return (gf / (1.0 + jnp.exp(-gf))) * on


def _retention_step(q, k, v, state, lg):
    L = q.shape[0]
    row = lax.broadcasted_iota(jnp.int32, (L, L), 0)
    col = lax.broadcasted_iota(jnp.int32, (L, L), 1)
    diff = (row - col).astype(F32)
    decay = jnp.where(diff >= 0.0, jnp.exp(jnp.maximum(diff, 0.0) * lg), 0.0)
    scores = lax.dot_general(q, k, _NT, preferred_element_type=F32) * decay
    inner = jnp.dot(scores.astype(BF16), v, preferred_element_type=F32)
    idx = lax.broadcasted_iota(jnp.int32, (L, 1), 0).astype(F32)
    q_decay = jnp.exp((idx + 1.0) * lg)
    cross = jnp.dot(q, state.astype(BF16), preferred_element_type=F32) * q_decay
    k_decay = jnp.exp((L - 1.0 - idx) * lg)
    kd = (k.astype(F32) * k_decay).T.astype(BF16)
    chunk_decay = jnp.exp(jnp.full((1, 1), L, F32) * lg)
    new_state = chunk_decay * state + jnp.dot(kd, v, preferred_element_type=F32)
    return inner + cross, new_state


def _ret_prompt_kernel(lg_ref, q_ref, k_ref, v_ref, g_ref, gn_ref, o_ref, s_ref):
    hg = pl.program_id(1)
    c = pl.program_id(2)
    hp = s_ref.shape[1]
    dk, dv = s_ref.shape[2], s_ref.shape[3]

    @pl.when(c == 0)
    def _():
        s_ref[...] = jnp.zeros_like(s_ref)

    steps = []
    for j in range(hp):
        qk_cols = slice(j * dk, (j + 1) * dk)
        v_cols = slice(j * dv, (j + 1) * dv)
        steps.append(_retention_step(q_ref[:, qk_cols], k_ref[:, qk_cols], v_ref[:, v_cols], s_ref[0, j],
                                     lg_ref[hg * hp + j]))
    for j, (o, new_state) in enumerate(steps):
        v_cols = slice(j * dv, (j + 1) * dv)
        s_ref[0, j] = new_state
        o_ref[:, v_cols] = _group_norm_gate(o, g_ref[:, v_cols], gn_ref[:, v_cols]).astype(o_ref.dtype)


def _ret_prompt(qkvg, gn, lg, *, batch, seq, heads, dk, dv):
    L = min(RET_CHUNK, seq)
    nc = seq // L
    hp = RET_HEADS_PER_STEP
    ng = heads // hp
    kq = ng
    v0 = 2 * heads * dk // (hp * dv)
    g0 = v0 + ng
    grid_spec = pltpu.PrefetchScalarGridSpec(
        num_scalar_prefetch=1,
        grid=(batch, ng, nc),
        in_specs=[
            pl.BlockSpec((L, hp * dk), lambda b, h, c, lg: (b * nc + c, h)),
            pl.BlockSpec((L, hp * dk), lambda b, h, c, lg: (b * nc + c, kq + h)),
            pl.BlockSpec((L, hp * dv), lambda b, h, c, lg: (b * nc + c, v0 + h)),
            pl.BlockSpec((L, hp * dv), lambda b, h, c, lg: (b * nc + c, g0 + h)),
            pl.BlockSpec((1, hp * dv), lambda b, h, c, lg: (0, h)),
        ],
        out_specs=[
            pl.BlockSpec((L, hp * dv), lambda b, h, c, lg: (b * nc + c, h)),
            pl.BlockSpec((1, hp, dk, dv), lambda b, h, c, lg: (b, h, 0, 0)),
        ],
    )
    return pl.pallas_call(
        _ret_prompt_kernel,
        grid_spec=grid_spec,
        out_shape=[
            jax.ShapeDtypeStruct((batch * seq, heads * dv), BF16),
            jax.ShapeDtypeStruct((batch, heads, dk, dv), F32),
        ],
        compiler_params=_cparams(3),
        name="ret_prompt",
    )(lg, qkvg, qkvg, qkvg, qkvg, gn.reshape(1, -1))


def _ret_sample_kernel(lg_ref, q_ref, k_ref, v_ref, g_ref, gn_ref, s_in_ref, o_ref, s_out_ref, *, t_len):
    h = pl.program_id(1)
    lg = lg_ref[h]
    nb = s_in_ref.shape[0]
    n = nb * t_len
    q = q_ref[...]
    k = k_ref[...]
    v = v_ref[...]
    g = g_ref[...]
    gn = gn_ref[...]
    row = lax.broadcasted_iota(jnp.int32, (n, n), 0)
    col = lax.broadcasted_iota(jnp.int32, (n, n), 1)
    diff = (row - col).astype(F32)
    keep = (row // t_len == col // t_len) & (row >= col)
    decay = jnp.where(keep, jnp.exp(jnp.maximum(diff, 0.0) * lg), 0.0)
    scores = lax.dot_general(q, k, _NT, preferred_element_type=F32) * decay
    inner = jnp.dot(scores.astype(BF16), v, preferred_element_type=F32)
    idx = (lax.broadcasted_iota(jnp.int32, (n, 1), 0) % t_len).astype(F32)
    q_decay = jnp.exp((idx + 1.0) * lg)
    kd = k.astype(F32) * jnp.exp((t_len - 1.0 - idx) * lg)
    chunk_decay = jnp.exp(jnp.full((1, 1), t_len, F32) * lg)
    group = BF16_ROWS // t_len
    seq_of_row = lax.broadcasted_iota(jnp.int32, (BF16_ROWS, 1), 0) // t_len
    for p in range(nb // group):
        rows = slice(p * BF16_ROWS, (p + 1) * BF16_ROWS)
        q16, kd16, v16 = q[rows], kd[rows], v[rows]
        cross = jnp.zeros((BF16_ROWS, v.shape[1]), F32)
        for j in range(group):
            b = p * group + j
            state = s_in_ref[b, 0]
            mine = seq_of_row == j
            cross = jnp.where(mine, jnp.dot(q16, state.astype(BF16), preferred_element_type=F32), cross)
            kdb = jnp.where(mine, kd16, 0.0).T.astype(BF16)
            s_out_ref[b, 0] = chunk_decay * state + jnp.dot(kdb, v16, preferred_element_type=F32)
        o = inner[rows] + cross * q_decay[rows]
        o_ref[rows, :] = _group_norm_gate(o, g[rows], gn).astype(o_ref.dtype)


def _ret_sample(qkvg, gn, lg, state, *, row0, t_len, heads, dk, dv):
    nbatch = state.shape[0]
    nb = 8
    rows = nb * t_len
    r0 = row0 // rows
    kq = heads
    v0 = 2 * heads * dk // dv
    g0 = v0 + heads
    grid_spec = pltpu.PrefetchScalarGridSpec(
        num_scalar_prefetch=1,
        grid=(nbatch // nb, heads),
        in_specs=[
            pl.BlockSpec((rows, dk), lambda i, h, lg: (r0 + i, h)),
            pl.BlockSpec((rows, dk), lambda i, h, lg: (r0 + i, kq + h)),
            pl.BlockSpec((rows, dv), lambda i, h, lg: (r0 + i, v0 + h)),
            pl.BlockSpec((rows, dv), lambda i, h, lg: (r0 + i, g0 + h)),
            pl.BlockSpec((1, dv), lambda i, h, lg: (0, h)),
            pl.BlockSpec((nb, 1, dk, dv), lambda i, h, lg: (i, h, 0, 0)),
        ],
        out_specs=[
            pl.BlockSpec((rows, dv), lambda i, h, lg: (i, h)),
            pl.BlockSpec((nb, 1, dk, dv), lambda i, h, lg: (i, h, 0, 0)),
        ],
    )
    return pl.pallas_call(
        functools.partial(_ret_sample_kernel, t_len=t_len),
        grid_spec=grid_spec,
        out_shape=[
            jax.ShapeDtypeStruct((nbatch * t_len, heads * dv), BF16),
            jax.ShapeDtypeStruct(state.shape, F32),
        ],
        compiler_params=_cparams(2),
        name="ret_sample",
    )(lg, qkvg, qkvg, qkvg, qkvg, gn.reshape(1, -1), state)


def _matmul_res_kernel(xp_ref, xs_ref, w_ref, r_ref, o_ref, *, prompt_tiles):
    i = pl.program_id(0)

    @pl.when(i < prompt_tiles)
    def _():
        o_ref[...] = r_ref[...] + jnp.dot(xp_ref[...], w_ref[...], preferred_element_type=F32)

    @pl.when(i >= prompt_tiles)
    def _():
        o_ref[...] = r_ref[...] + jnp.dot(xs_ref[...], w_ref[...], preferred_element_type=F32)


def _matmul_res(x_prompt, x_sample, w, res, *, name):
    n, d = res.shape
    k = w.shape[0]
    tm = ROW_TILE
    pt = x_prompt.shape[0] // tm
    return pl.pallas_call(
        functools.partial(_matmul_res_kernel, prompt_tiles=pt),
        grid=(n // tm,),
        in_specs=[
            pl.BlockSpec((tm, k), lambda i: (jnp.minimum(i, pt - 1), 0)),
            pl.BlockSpec((tm, k), lambda i: (jnp.maximum(i - pt, 0), 0)),
            pl.BlockSpec((k, d), lambda i: (0, 0)),
            pl.BlockSpec((tm, d), lambda i: (i, 0)),
        ],
        out_specs=pl.BlockSpec((tm, d), lambda i: (i, 0)),
        out_shape=jax.ShapeDtypeStruct((n, d), F32),
        compiler_params=_cparams(1),
        name=name,
    )(x_prompt, x_sample, w, res)


def _router_kernel(x_ref, g_ref, wh_ref, wl_ref, b_ref, idx_ref, gate_ref, *, n_experts):
    xn = _rms(x_ref[...], g_ref[...])
    hi = xn.astype(BF16)
    lo = (xn - hi.astype(F32)).astype(BF16)
    logits = (jnp.dot(hi, wh_ref[...], preferred_element_type=F32)
              + jnp.dot(lo, wh_ref[...], preferred_element_type=F32)
              + jnp.dot(hi, wl_ref[...], preferred_element_type=F32)) + b_ref[...]
    lane = lax.broadcasted_iota(jnp.int32, logits.shape, 1).astype(F32)
    neg = jnp.float32(-jnp.inf)
    work = jnp.where(lane < n_experts, logits, neg)
    vals, idxs = [], []
    for _ in range(TOP_K):
        m = jnp.max(work, axis=-1, keepdims=True)
        sel = jnp.min(jnp.where(work == m, lane, float(LANES)), axis=-1, keepdims=True)
        vals.append(m)
        idxs.append(sel)
        work = jnp.where(lane == sel, neg, work)
    es = [jnp.exp(v - vals[0]) for v in vals]
    denom = es[0]
    for e in es[1:]:
        denom = denom + e
    idx_out = jnp.zeros(logits.shape, F32)
    gate_out = jnp.zeros(logits.shape, F32)
    for kk in range(TOP_K):
        idx_out = jnp.where(lane == kk, idxs[kk], idx_out)
        gate_out = jnp.where(lane == kk, es[kk] / denom, gate_out)
    idx_ref[...] = idx_out.astype(jnp.int32)
    gate_ref[...] = gate_out


def _router(y, g, wr, br):
    n, d = y.shape
    e = wr.shape[1]
    tm = ROW_TILE
    wr_pad = jnp.zeros((d, LANES), F32).at[:, :e].set(wr)
    wh = wr_pad.astype(BF16)
    wl = (wr_pad - wh.astype(F32)).astype(BF16)
    b_pad = jnp.zeros((1, LANES), F32).at[0, :e].set(br)
    return pl.pallas_call(
        functools.partial(_router_kernel, n_experts=e),
        grid=(n // tm,),
        in_specs=[
            pl.BlockSpec((tm, d), lambda i: (i, 0)),
            pl.BlockSpec((1, d), lambda i: (0, 0)),
            pl.BlockSpec((d, LANES), lambda i: (0, 0)),
            pl.BlockSpec((d, LANES), lambda i: (0, 0)),
            pl.BlockSpec((1, LANES), lambda i: (0, 0)),
        ],
        out_specs=[
            pl.BlockSpec((tm, LANES), lambda i: (i, 0)),
            pl.BlockSpec((tm, LANES), lambda i: (i, 0)),
        ],
        out_shape=[
            jax.ShapeDtypeStruct((n, LANES), jnp.int32),
            jax.ShapeDtypeStruct((n, LANES), F32),
        ],
        compiler_params=_cparams(1),
        name="moe_router",
    )(y, g.reshape(1, d), wh, wl, b_pad)


def _split_gate_up_kernel(w_ref, p_ref, o_ref):
    half = o_ref.shape[1] // 2
    hp = MXU_DIM // 2
    for c in range(o_ref.shape[1] // MXU_DIM):
        y = jnp.dot(w_ref[:, c * MXU_DIM:(c + 1) * MXU_DIM].astype(BF16), p_ref[...], preferred_element_type=F32)
        o_ref[:, c * hp:(c + 1) * hp] = y[:, :hp].astype(BF16)
        o_ref[:, half + c * hp:half + (c + 1) * hp] = y[:, hp:].astype(BF16)


def _split_gate_up(wgu_all, layer):
    _, e, d, w2 = wgu_all.shape
    rows = e * d
    tm = ROW_TILE
    row0 = layer * rows // tm
    src = lax.broadcasted_iota(jnp.int32, (MXU_DIM, MXU_DIM), 0)
    dst = lax.broadcasted_iota(jnp.int32, (MXU_DIM, MXU_DIM), 1)
    perm = (dst == (src % 2) * (MXU_DIM // 2) + src // 2).astype(BF16)
    out = pl.pallas_call(
        _split_gate_up_kernel,
        grid=(rows // tm,),
        in_specs=[pl.BlockSpec((tm, w2), lambda i: (row0 + i, 0)),
                  pl.BlockSpec((MXU_DIM, MXU_DIM), lambda i: (0, 0))],
        out_specs=pl.BlockSpec((tm, w2), lambda i: (i, 0)),
        out_shape=jax.ShapeDtypeStruct((rows, w2), BF16),
        compiler_params=_cparams(1),
        name="moe_split_gate_up",
    )(wgu_all.reshape(-1, w2), perm)
    return out.reshape(e, d, w2)


def _row_copy(src, dst, sem, src_row, dst_row):
    return pltpu.make_async_copy(src.at[pl.ds(src_row, 1)], dst.at[pl.ds(dst_row, 1)], sem)


def _dispatch_kernel(dest_ref, hi_ref, y_ref, xb_hbm, zblk, sem, *, n_experts):
    i = pl.program_id(0)
    tm = y_ref.shape[0]
    blk = zblk.shape[0]
    n_blocks = xb_hbm.shape[0] // blk

    def zero_block(b, wait):
        cp = pltpu.make_async_copy(zblk, xb_hbm.at[pl.ds(pl.multiple_of(b * blk, blk), blk)], sem.at[1])
        cp.wait() if wait else cp.start()

    @pl.when(i == 0)
    def _():
        zblk[...] = jnp.zeros_like(zblk)
        for g in range(0, n_experts, ZERO_GROUP):
            for wait in (False, True):
                for e in range(g, min(g + ZERO_GROUP, n_experts)):
                    @pl.when(hi_ref[e] > (hi_ref[e - 1] if e else 0))
                    def _():
                        zero_block(hi_ref[e] // blk - 1, wait)

        first_unused = hi_ref[n_experts - 1] // blk

        def tail_group(gi, _):
            for wait in (False, True):
                for j in range(ZERO_GROUP):
                    b = first_unused + gi * ZERO_GROUP + j

                    @pl.when(b < n_blocks)
                    def _():
                        zero_block(b, wait)
            return 0

        lax.fori_loop(0, (n_blocks - first_unused + ZERO_GROUP - 1) // ZERO_GROUP, tail_group, 0)

    def scatter(wait):
        def body(t, _):
            for kk in range(TOP_K):
                cp = _row_copy(y_ref, xb_hbm, sem.at[0], t, dest_ref[(i * tm + t) * TOP_K + kk])
                cp.wait() if wait else cp.start()
            return 0
        lax.fori_loop(0, tm, body, 0, unroll=DMA_UNROLL // TOP_K)

    scatter(False)
    scatter(True)


def _dispatch(y, dest, region_end, n_rows):
    n, d = y.shape
    tm = DISPATCH_TILE
    grid_spec = pltpu.PrefetchScalarGridSpec(
        num_scalar_prefetch=2,
        grid=(n // tm,),
        in_specs=[pl.BlockSpec((tm, d), lambda i, dest, hi: (i, 0))],
        out_specs=pl.BlockSpec(memory_space=pl.ANY),
        scratch_shapes=[pltpu.VMEM((MOE_BLOCK, d), F32), pltpu.SemaphoreType.DMA((2,))],
    )
    return pl.pallas_call(
        functools.partial(_dispatch_kernel, n_experts=region_end.shape[0]),
        grid_spec=grid_spec,
        out_shape=jax.ShapeDtypeStruct((n_rows, d), F32),
        compiler_params=_cparams(1),
        name="moe_dispatch",
    )(dest, region_end, y)


def _expert_kernel(be_ref, nv_ref, x_ref, g_ref, wgu_ref, bgu_ref, wd_ref, bd_ref, o_ref, wd_bf, *, d_expert):
    i = pl.program_id(0)

    @pl.when((i == 0) | (be_ref[i] != be_ref[jnp.maximum(i - 1, 0)]))
    def _():
        wd_bf[...] = wd_ref[0, 0].astype(BF16)

    @pl.when(i < nv_ref[0])
    def _():
        rows = o_ref.shape[0] // EXPERT_ROW_GROUPS
        groups = [slice(r * rows, (r + 1) * rows) for r in range(EXPERT_ROW_GROUPS)]
        hgus = [jnp.dot(_rms(x_ref[grp, :], g_ref[...]).astype(BF16), wgu_ref[0], preferred_element_type=F32)
                + bgu_ref[0] for grp in groups]
        for grp, hgu in zip(groups, hgus):
            gate = jnp.minimum(hgu[:, :d_expert], SWIGLU_LIMIT)
            up = jnp.clip(hgu[:, d_expert:], -SWIGLU_LIMIT, SWIGLU_LIMIT)
            act = (up + 1.0) * gate * (1.0 / (1.0 + jnp.exp(-(gate * SWIGLU_ALPHA))))
            o_ref[grp, :] = jnp.dot(act.astype(BF16), wd_bf[...], preferred_element_type=F32) + bd_ref[0]

    @pl.when(i >= nv_ref[0])
    def _():
        o_ref[...] = jnp.zeros_like(o_ref)


def _experts(xb, g, block_expert, n_valid, wgu, bgu, wd_all, layer, bd):
    r, d = xb.shape
    de = wd_all.shape[2]
    blk = MOE_BLOCK
    grid_spec = pltpu.PrefetchScalarGridSpec(
        num_scalar_prefetch=2,
        grid=(r // blk,),
        in_specs=[
            pl.BlockSpec((blk, d), lambda i, be, nv: (jnp.minimum(i, nv[0] - 1), 0)),
            pl.BlockSpec((1, d), lambda i, be, nv: (0, 0)),
            pl.BlockSpec((1, d, 2 * de), lambda i, be, nv: (be[i], 0, 0)),
            pl.BlockSpec((1, 1, 2 * de), lambda i, be, nv: (be[i], 0, 0)),
            pl.BlockSpec((1, 1, de, d), lambda i, be, nv: (layer, be[i], 0, 0)),
            pl.BlockSpec((1, 1, d), lambda i, be, nv: (be[i], 0, 0)),
        ],
        out_specs=pl.BlockSpec((blk, d), lambda i, be, nv: (i, 0)),
        scratch_shapes=[pltpu.VMEM((de, d), BF16)],
    )
    return pl.pallas_call(
        functools.partial(_expert_kernel, d_expert=de),
        grid_spec=grid_spec,
        out_shape=jax.ShapeDtypeStruct((r, d), F32),
        compiler_params=_cparams(1),
        name="moe_experts",
    )(block_expert, n_valid, xb, g.reshape(1, d), wgu, bgu, wd_all, bd)


def _combine_kernel(dest_ref, res_ref, gate_ref, yb_hbm, o_ref, buf, sem):
    i = pl.program_id(0)
    nsteps = pl.num_programs(0)
    tm = o_ref.shape[0]
    slot = i % 2

    def gather(tile, sl, wait):
        def body(t, _):
            for kk in range(TOP_K):
                cp = _row_copy(yb_hbm, buf.at[sl, kk], sem.at[sl], dest_ref[(tile * tm + t) * TOP_K + kk], t)
                cp.wait() if wait else cp.start()
            return 0
        lax.fori_loop(0, tm, body, 0, unroll=DMA_UNROLL // TOP_K)

    @pl.when(i == 0)
    def _():
        gather(0, 0, False)

    @pl.when(i + 1 < nsteps)
    def _():
        gather(i + 1, 1 - slot, False)

    gather(i, slot, True)
    acc = res_ref[...]
    gates = gate_ref[...]
    for kk in range(TOP_K):
        acc = acc + gates[:, kk:kk + 1] * buf[slot, kk]
    o_ref[...] = acc


def _combine(res, gate_pad, dest, yb):
    n, d = res.shape
    tm = COMBINE_TILE
    grid_spec = pltpu.PrefetchScalarGridSpec(
        num_scalar_prefetch=1,
        grid=(n // tm,),
        in_specs=[
            pl.BlockSpec((tm, d), lambda i, dest: (i, 0)),
            pl.BlockSpec((tm, LANES), lambda i, dest: (i, 0)),
            pl.BlockSpec(memory_space=pl.ANY),
        ],
        out_specs=pl.BlockSpec((tm, d), lambda i, dest: (i, 0)),
        scratch_shapes=[pltpu.VMEM((2, TOP_K, tm, d), F32), pltpu.SemaphoreType.DMA((2,))],
    )
    return pl.pallas_call(
        _combine_kernel,
        grid_spec=grid_spec,
        out_shape=jax.ShapeDtypeStruct((n, d), F32),
        compiler_params=_cparams(1),
        name="moe_combine",
    )(dest, res, gate_pad, yb)


def _moe(y, g, wr, br, wgu_all, bgu, wd_all, bd, layer):
    n, d = y.shape
    e = wr.shape[1]
    de = wd_all.shape[2]
    idx_pad, gate_pad = _router(y, g, wr, br)

    a = n * TOP_K
    blk = MOE_BLOCK
    flat_e = idx_pad[:, :TOP_K].reshape(a)
    onehot = (flat_e[:, None] == jnp.arange(e, dtype=jnp.int32)[None, :]).astype(jnp.int32)
    csum = jnp.cumsum(onehot, axis=0)
    counts = csum[-1]
    rank = jnp.take_along_axis(csum, flat_e[:, None], axis=1)[:, 0] - 1
    padded = ((counts + blk - 1) // blk) * blk
    ends = jnp.cumsum(padded)
    dest = ((ends - padded)[flat_e] + rank).astype(jnp.int32)
    n_blocks = -(-a // blk) + e
    r = n_blocks * blk
    block_start = jnp.arange(n_blocks, dtype=jnp.int32) * blk
    block_expert = jnp.minimum(jnp.sum((ends[None, :] <= block_start[:, None]).astype(jnp.int32), axis=1), e - 1)
    n_valid = (ends[-1:] // blk).astype(jnp.int32)

    xb = _dispatch(y, dest, ends.astype(jnp.int32), r)
    wgu_b = _split_gate_up(wgu_all, layer)
    bgu_b = jnp.concatenate([bgu[:, 0::2], bgu[:, 1::2]], axis=-1).reshape(e, 1, 2 * de)
    yb = _experts(xb, g, block_expert, n_valid, wgu_b, bgu_b, wd_all, layer, bd.reshape(e, 1, d))
    return _combine(y, gate_pad, dest, yb)


def _rope_lanes(x, c, sa, sb, half):
    n = x.shape[1]
    return x * c + pltpu.roll(x, n - half, 1) * sa + pltpu.roll(x, half, 1) * sb


def _mla_proj_kernel(dqkv_ref, gq_ref, gkv_ref, wuq_ref, wuk_ref, wuv_ref, ones_ref, c_ref, sa_ref, sb_ref,
                     q_ref, k_ref, v_ref, ckv_ref, kpe_ref, *, q_rank, kv_rank, heads, rope_half, scale):
    x = dqkv_ref[...]
    c = c_ref[...]
    sa = sa_ref[...]
    sb = sb_ref[...]
    cq = _rms(x[:, :q_rank], gq_ref[...]).astype(BF16)
    ckv = _rms(x[:, q_rank:q_rank + kv_rank], gkv_ref[...])
    ckv_ref[...] = ckv
    kpe = _rope_lanes(x[:, q_rank + kv_rank:], c, sa, sb, rope_half)
    kpe_ref[...] = kpe
    ckv_b = ckv.astype(BF16)
    q = jnp.dot(cq, wuq_ref[...], preferred_element_type=F32)
    k = jnp.dot(ckv_b, wuk_ref[...], preferred_element_type=F32)
    for h in range(heads):
        cols = slice(h * LANES, (h + 1) * LANES)
        q_ref[:, cols] = (_rope_lanes(q[:, cols], c, sa, sb, rope_half) * scale).astype(BF16)
        k_ref[:, cols] = (k[:, cols] + kpe).astype(BF16)
    v_ref[...] = (jnp.dot(ckv_b, wuv_ref[...], preferred_element_type=F32) + ones_ref[...]).astype(BF16)


def _mla_proj(dqkv, gq, gkv, wuq_p, wuk_p, wuv_p, ones_p, c, sa, sb, *, n_prompt, seq, q_rank, kv_rank, heads,
              rope_half, scale):
    n, w = dqkv.shape
    tm = ROW_TILE
    pt, st = n_prompt // tm, seq // tm
    hv = wuv_p.shape[1]
    full = lambda shape: pl.BlockSpec(shape, lambda i: (0,) * len(shape))
    rows = lambda width: pl.BlockSpec((tm, width), lambda i: (i, 0))
    table = pl.BlockSpec((tm, LANES), lambda i: (_table_tile(i, pt, st), 0))
    return pl.pallas_call(
        functools.partial(_mla_proj_kernel, q_rank=q_rank, kv_rank=kv_rank, heads=heads, rope_half=rope_half,
                          scale=scale),
        grid=(n // tm,),
        in_specs=[rows(w), full((1, q_rank)), full((1, kv_rank)), full(wuq_p.shape), full(wuk_p.shape),
                  full(wuv_p.shape), full((1, hv)), table, table, table],
        out_specs=[rows(heads * LANES), rows(heads * LANES), rows(hv), rows(kv_rank), rows(LANES)],
        out_shape=[
            jax.ShapeDtypeStruct((n, heads * LANES), BF16),
            jax.ShapeDtypeStruct((n, heads * LANES), BF16),
            jax.ShapeDtypeStruct((n, hv), BF16),
            jax.ShapeDtypeStruct((n, kv_rank), F32),
            jax.ShapeDtypeStruct((n, LANES), F32),
        ],
        compiler_params=_cparams(1),
        name="mla_proj",
    )(dqkv, gq.reshape(1, -1), gkv.reshape(1, -1), wuq_p, wuk_p, wuv_p, ones_p, c, sa, sb)


def _flash_kernel(q_ref, k_ref, v_ref, o_ref, *scratch, tile, heads, v_head):
    m_refs, acc_refs = scratch[:heads], scratch[heads:]
    qi = pl.program_id(2)
    for h in range(heads):
        m_refs[h][...] = jnp.full((tile, LANES), -jnp.inf, F32)
        acc_refs[h][...] = jnp.zeros((tile, LANES), F32)
    nchunk = tile // LANES

    def step(ki, masked):
        start = pl.multiple_of(ki * tile, tile)
        scores = []
        for h in range(heads):
            cols = slice(h * LANES, (h + 1) * LANES)
            s = lax.dot_general(q_ref[:, cols], k_ref[pl.ds(start, tile), cols], _NT, preferred_element_type=F32)
            if masked:
                row = lax.broadcasted_iota(jnp.int32, s.shape, 0)
                col = lax.broadcasted_iota(jnp.int32, s.shape, 1)
                s = jnp.where(col <= row, s, jnp.finfo(F32).min)
            scores.append(s)
        for h in range(heads):
            cols = slice(h * LANES, (h + 1) * LANES)
            chunks = [scores[h][:, c * LANES:(c + 1) * LANES] for c in range(nchunk)]
            part = chunks[0]
            for ch in chunks[1:]:
                part = jnp.maximum(part, ch)
            m_old = m_refs[h][...]
            m_new = jnp.maximum(m_old, jnp.max(part, axis=-1, keepdims=True))
            m_refs[h][...] = m_new
            p = jnp.concatenate([jnp.exp2(ch - m_new) for ch in chunks], axis=1).astype(BF16)
            acc_refs[h][...] = (jnp.exp2(m_old - m_new) * acc_refs[h][...]
                                + jnp.dot(p, v_ref[pl.ds(start, tile), cols], preferred_element_type=F32))

    def body(kp, carry):
        step(2 * kp, False)
        step(2 * kp + 1, False)
        return carry

    lax.fori_loop(0, qi // 2, body, 0)

    @pl.when(qi % 2 == 1)
    def _():
        step(qi - 1, False)

    step(qi, True)
    lane = lax.broadcasted_iota(jnp.int32, (tile, LANES), 1)
    for j in range(heads // 2):
        even = acc_refs[2 * j][...]
        odd = acc_refs[2 * j + 1][...]
        num = jnp.where(lane < v_head, even, odd)
        den = jnp.where(lane < v_head, pltpu.roll(even, v_head, 1), pltpu.roll(odd, v_head, 1))
        o_ref[:, j * LANES:(j + 1) * LANES] = (num / den).astype(o_ref.dtype)


def _flash(q_cat, k_cat, v_ext, *, batch, seq, heads, v_head):
    tile = min(ATTN_TILE, seq)
    nq = seq // tile
    hg = ATTN_HEADS
    return pl.pallas_call(
        functools.partial(_flash_kernel, tile=tile, heads=hg, v_head=v_head),
        grid=(batch, heads // hg, nq),
        in_specs=[
            pl.BlockSpec((tile, hg * LANES), lambda b, g, qi: (b * nq + qi, g)),
            pl.BlockSpec((seq, hg * LANES), lambda b, g, qi: (b, g)),
            pl.BlockSpec((seq, hg * LANES), lambda b, g, qi: (b, g)),
        ],
        out_specs=pl.BlockSpec((tile, hg * v_head), lambda b, g, qi: (b * nq + qi, g)),
        out_shape=jax.ShapeDtypeStruct((batch * seq, heads * v_head), BF16),
        scratch_shapes=[pltpu.VMEM((tile, LANES), F32)] * (2 * hg),
        compiler_params=_cparams(3),
        name="mla_flash",
    )(q_cat, k_cat, v_ext)


def _qlat_kernel(q_ref, w_ref, o_ref, *, heads, kv_rank):
    for h in range(heads):
        o_ref[:, h * kv_rank:(h + 1) * kv_rank] = jnp.dot(
            q_ref[:, h * LANES:(h + 1) * LANES], w_ref[h], preferred_element_type=F32).astype(o_ref.dtype)


def _qlat(q_cat, wukt_p, *, row0, nrows, heads, kv_rank):
    tm = min(ROW_TILE, nrows)
    r0 = row0 // tm
    return pl.pallas_call(
        functools.partial(_qlat_kernel, heads=heads, kv_rank=kv_rank),
        grid=(nrows // tm,),
        in_specs=[
            pl.BlockSpec((tm, heads * LANES), lambda i: (r0 + i, 0)),
            pl.BlockSpec(wukt_p.shape, lambda i: (0, 0, 0)),
        ],
        out_specs=pl.BlockSpec((tm, heads * kv_rank), lambda i: (i, 0)),
        out_shape=jax.ShapeDtypeStruct((nrows, heads * kv_rank), BF16),
        compiler_params=_cparams(1),
        name="mla_qlat",
    )(q_cat, wukt_p)


def _olat_kernel(x_ref, w_ref, o_ref, *, pairs, kv_rank):
    for p in range(pairs):
        o_ref[:, p * LANES:(p + 1) * LANES] = jnp.dot(
            x_ref[:, p * 2 * kv_rank:(p + 1) * 2 * kv_rank], w_ref[p], preferred_element_type=F32).astype(o_ref.dtype)


def _olat(o_lat, wuv_pair, *, heads, kv_rank):
    n = o_lat.shape[0]
    tm = min(ROW_TILE, n)
    return pl.pallas_call(
        functools.partial(_olat_kernel, pairs=heads // 2, kv_rank=kv_rank),
        grid=(n // tm,),
        in_specs=[
            pl.BlockSpec((tm, heads * kv_rank), lambda i: (i, 0)),
            pl.BlockSpec(wuv_pair.shape, lambda i: (0, 0, 0)),
        ],
        out_specs=pl.BlockSpec((tm, (heads // 2) * LANES), lambda i: (i, 0)),
        out_shape=jax.ShapeDtypeStruct((n, (heads // 2) * LANES), BF16),
        compiler_params=_cparams(1),
        name="mla_olat",
    )(o_lat, wuv_pair)


def _paged_copies(pt_ref, cache_ckv, cache_kpe, ckv_buf, kpe_buf, sem, b, slot, p, page):
    pg = pt_ref[b, p]
    return (
        pltpu.make_async_copy(cache_ckv.at[0, pg], ckv_buf.at[slot, pl.ds(p * page, page)], sem.at[0, slot]),
        pltpu.make_async_copy(cache_kpe.at[0, pg], kpe_buf.at[slot, :, pl.ds(p * page, page)], sem.at[1, slot]),
    )


def _paged_attn_kernel(pt_ref, qlat_ref, qpe_ref, cnew_ref, pnew_ref, cache_ckv, cache_kpe, o_ref,
                       ckv_buf, kpe_buf, sem, *, n_pages, page, heads):
    b = pl.program_id(0)
    nb = pl.num_programs(0)
    slot = b % 2

    def start_all(bb, sl):
        def body(p, _):
            for cp in _paged_copies(pt_ref, cache_ckv, cache_kpe, ckv_buf, kpe_buf, sem, bb, sl, p, page):
                cp.start()
            return 0
        lax.fori_loop(0, n_pages, body, 0)

    @pl.when(b == 0)
    def _():
        start_all(0, 0)

    @pl.when(b + 1 < nb)
    def _():
        start_all(b + 1, 1 - slot)

    def wait_body(p, _):
        for cp in _paged_copies(pt_ref, cache_ckv, cache_kpe, ckv_buf, kpe_buf, sem, b, slot, p, page):
            cp.wait()
        return 0
    lax.fori_loop(0, n_pages, wait_body, 0)

    qlat = qlat_ref[0]
    qpe = qpe_ref[0]
    ck = ckv_buf[slot].astype(BF16)
    kp = kpe_buf[slot].astype(BF16)
    s_past = (lax.dot_general(qlat, ck, _NT, preferred_element_type=F32)
              + jnp.dot(qpe, kp, preferred_element_type=F32))
    cn = cnew_ref[0].astype(BF16)
    pn = pnew_ref[0].astype(BF16)
    s_new = (lax.dot_general(qlat, cn, _NT, preferred_element_type=F32)
             + lax.dot_general(qpe, pn, _NT, preferred_element_type=F32))
    t_of_row = lax.broadcasted_iota(jnp.int32, s_new.shape, 0) // heads
    j = lax.broadcasted_iota(jnp.int32, s_new.shape, 1)
    s_new = jnp.where(j <= t_of_row, s_new, jnp.finfo(F32).min)
    m = jnp.maximum(jnp.max(s_past, axis=-1, keepdims=True), jnp.max(s_new, axis=-1, keepdims=True))
    p_past = jnp.exp2(s_past - m)
    p_new = jnp.exp2(s_new - m)
    denom = jnp.sum(p_past, axis=-1, keepdims=True) + jnp.sum(p_new, axis=-1, keepdims=True)
    o = (jnp.dot(p_past.astype(BF16), ck, preferred_element_type=F32)
         + jnp.dot(p_new.astype(BF16), cn, preferred_element_type=F32))
    o_ref[0] = (o / denom).astype(o_ref.dtype)


def _paged_attn(page_table, qlat, qpe, ckv_new, kpe_new, cache_ckv, cache_kpe, *, heads):
    nbatch, n_pages = page_table.shape
    page, kv_rank = cache_ckv.shape[2], cache_ckv.shape[3]
    rope = cache_kpe.shape[2]
    rows = qlat.shape[1]
    tpad = ckv_new.shape[1]
    grid_spec = pltpu.PrefetchScalarGridSpec(
        num_scalar_prefetch=1,
        grid=(nbatch,),
        in_specs=[
            pl.BlockSpec((1, rows, kv_rank), lambda b, pt: (b, 0, 0)),
            pl.BlockSpec((1, rows, rope), lambda b, pt: (b, 0, 0)),
            pl.BlockSpec((1, tpad, kv_rank), lambda b, pt: (b, 0, 0)),
            pl.BlockSpec((1, tpad, rope), lambda b, pt: (b, 0, 0)),
            pl.BlockSpec(memory_space=pl.ANY),
            pl.BlockSpec(memory_space=pl.ANY),
        ],
        out_specs=pl.BlockSpec((1, rows, kv_rank), lambda b, pt: (b, 0, 0)),
        scratch_shapes=[
            pltpu.VMEM((2, n_pages * page, kv_rank), F32),
            pltpu.VMEM((2, rope, n_pages * page), F32),
            pltpu.SemaphoreType.DMA((2, 2)),
        ],
    )
    return pl.pallas_call(
        functools.partial(_paged_attn_kernel, n_pages=n_pages, page=page, heads=heads),
        grid_spec=grid_spec,
        out_shape=jax.ShapeDtypeStruct((nbatch, rows, kv_rank), BF16),
        compiler_params=_cparams(1),
        name="mla_paged_attn",
    )(page_table, qlat, qpe, ckv_new, kpe_new, cache_ckv, cache_kpe)


def _rmsnorm_kernel(x_ref, g_ref, op_ref, os_ref, *, prompt_tiles):
    i = pl.program_id(0)
    y = _rms(x_ref[...], g_ref[...])

    @pl.when(i < prompt_tiles)
    def _():
        op_ref[...] = y

    @pl.when(i >= prompt_tiles)
    def _():
        os_ref[...] = y


def _rmsnorm(x, g, n_prompt):
    n, d = x.shape
    tm = ROW_TILE
    pt = n_prompt // tm
    return pl.pallas_call(
        functools.partial(_rmsnorm_kernel, prompt_tiles=pt),
        grid=(n // tm,),
        in_specs=[pl.BlockSpec((tm, d), lambda i: (i, 0)), pl.BlockSpec((1, d), lambda i: (0, 0))],
        out_specs=[
            pl.BlockSpec((tm, d), lambda i: (jnp.minimum(i, pt - 1), 0)),
            pl.BlockSpec((tm, d), lambda i: (jnp.maximum(i - pt, 0), 0)),
        ],
        out_shape=[jax.ShapeDtypeStruct((n_prompt, d), F32), jax.ShapeDtypeStruct((n - n_prompt, d), F32)],
        compiler_params=_cparams(1),
        name="final_norm",
    )(x, g.reshape(1, d))


def _rope_angles(pos, half):
    inv = ROPE_BASE ** (-jnp.arange(half, dtype=F32) / half)
    return pos.astype(F32)[:, None] * inv[None, :]


def _table_positions(seq, dec_seq, past_len, tm):
    return jnp.concatenate([jnp.arange(seq), jnp.tile(past_len + jnp.arange(dec_seq), tm // dec_seq)])


def kernel(x_prompt, x_sample, state_ret, cache_ckv, cache_kpe, page_table, norm_mix, norm_ffn, norm_final, ret_wq, ret_wk, ret_wv, ret_wg, ret_gn, ret_wo, mla_wdq, mla_gq, mla_wuq, mla_wdkv, mla_gkv, mla_wuk, mla_wuv, mla_wo, moe_wr, moe_br, moe_wgu, moe_bgu, moe_wd, moe_bd):
    batch, seq, d = x_prompt.shape
    dec_batch, dec_seq, _ = x_sample.shape
    ret_heads, dk, dv = state_ret.shape[2], state_ret.shape[3], state_ret.shape[4]
    page = cache_ckv.shape[2]
    kv_rank = cache_ckv.shape[3]
    rope_dim = cache_kpe.shape[3]
    past_len = page_table.shape[1] * page
    mla_heads, nope = mla_wuk.shape[2], mla_wuk.shape[3]
    v_head = mla_wuv.shape[3]
    q_rank = mla_wdq.shape[2]
    n_prompt = batch * seq
    n_sample = dec_batch * dec_seq
    assert nope + rope_dim <= LANES and 2 * v_head == LANES and dk == 2 * LANES

    y = jnp.concatenate([x_prompt.reshape(n_prompt, d), x_sample.reshape(n_sample, d)], axis=0)
    n_all = n_prompt + n_sample
    assert n_sample % _proj_tile(n_all) == 0 and seq % _proj_tile(n_all) == 0 and ROW_TILE % dec_seq == 0

    ang = _rope_angles(_table_positions(seq, dec_seq, past_len, _proj_tile(n_all)), dk // 2)
    w_all = jnp.concatenate([ret_wq[0], ret_wk[0] * (dk ** -0.5), ret_wv[0], ret_wg[0]], axis=1).astype(BF16)
    qkvg = _norm_matmul(y, norm_mix[0], w_all, jnp.cos(ang), jnp.sin(ang), n_prompt=n_prompt, seq=seq,
                        tn=ret_heads * dk, n_rope_tiles=2, head_dim=dk, out_dtype=BF16, name="ret_proj")
    lg = jnp.log(1.0 - 2.0 ** (-5.0 - jnp.arange(ret_heads, dtype=F32)))
    gated_p, state_p = _ret_prompt(qkvg, ret_gn[0], lg, batch=batch, seq=seq, heads=ret_heads, dk=dk, dv=dv)
    gated_s, state_s = _ret_sample(qkvg, ret_gn[0], lg, state_ret[0], row0=n_prompt, t_len=dec_seq,
                                   heads=ret_heads, dk=dk, dv=dv)
    y = _matmul_res(gated_p, gated_s, ret_wo[0].astype(BF16), y, name="ret_out")
    y = _moe(y, norm_ffn[0], moe_wr[0], moe_br[0], moe_wgu, moe_bgu[0], moe_wd, moe_bd[0], 0)

    half = rope_dim // 2
    ang = _rope_angles(_table_positions(seq, dec_seq, past_len, ROW_TILE), half)
    cos2 = jnp.concatenate([jnp.cos(ang), jnp.cos(ang)], axis=1)
    sin = jnp.sin(ang)
    ones = jnp.ones((ang.shape[0], LANES), F32)
    zeros = jnp.zeros((ang.shape[0], LANES), F32)
    c_tab = lax.dynamic_update_slice(ones, cos2, (0, nope))
    sa_tab = lax.dynamic_update_slice(zeros, -sin, (0, nope))
    sb_tab = lax.dynamic_update_slice(zeros, sin, (0, nope + half))

    wdkv = mla_wdkv[0]
    kpe_cols = jnp.zeros((d, LANES), F32).at[:, nope:nope + rope_dim].set(wdkv[:, kv_rank:])
    w_down = jnp.concatenate([mla_wdq[0], wdkv[:, :kv_rank], kpe_cols], axis=1).astype(BF16)
    no_rope = jnp.zeros((seq + _proj_tile(n_all), LANES), F32)
    dqkv = _norm_matmul(y, norm_mix[1], w_down, no_rope, no_rope, n_prompt=n_prompt, seq=seq, tn=w_down.shape[1],
                        n_rope_tiles=0, head_dim=0, out_dtype=F32, name="mla_down")

    qk_dim = nope + rope_dim
    wuq_p = jnp.zeros((q_rank, mla_heads, LANES), F32).at[:, :, :qk_dim].set(
        mla_wuq[0].reshape(q_rank, mla_heads, qk_dim)).reshape(q_rank, mla_heads * LANES).astype(BF16)
    wuk_p = jnp.zeros((kv_rank, mla_heads, LANES), F32).at[:, :, :nope].set(mla_wuk[0]).reshape(
        kv_rank, mla_heads * LANES).astype(BF16)
    wuv_pairs = mla_wuv[0].reshape(kv_rank, mla_heads // 2, 2, v_head)
    wuv_p = jnp.zeros((kv_rank, mla_heads // 2, 2, 2, v_head), F32)
    wuv_p = wuv_p.at[:, :, 0, 0].set(wuv_pairs[:, :, 0]).at[:, :, 1, 1].set(wuv_pairs[:, :, 1])
    wuv_p = wuv_p.reshape(kv_rank, mla_heads * LANES).astype(BF16)
    ones_p = jnp.zeros((mla_heads // 2, 2, 2, v_head), F32).at[:, 0, 1].set(1.0).at[:, 1, 0].set(1.0)
    ones_p = ones_p.reshape(1, mla_heads * LANES)
    scale = (float(qk_dim) ** -0.5) * math.log2(math.e)
    q_cat, k_cat, v_ext, ckv_all, kpe_all = _mla_proj(
        dqkv, mla_gq[0], mla_gkv[0], wuq_p, wuk_p, wuv_p, ones_p, c_tab, sa_tab, sb_tab,
        n_prompt=n_prompt, seq=seq, q_rank=q_rank, kv_rank=kv_rank, heads=mla_heads, rope_half=half, scale=scale)
    kpe_rows = kpe_all[:, nope:nope + rope_dim]

    o_p = _flash(q_cat, k_cat, v_ext, batch=batch, seq=seq, heads=mla_heads, v_head=v_head)

    wukt_p = jnp.zeros((mla_heads, LANES, kv_rank), F32).at[:, :nope, :].set(
        jnp.transpose(mla_wuk[0], (1, 2, 0))).astype(BF16)
    q_lat = _qlat(q_cat, wukt_p, row0=n_prompt, nrows=n_sample, heads=mla_heads, kv_rank=kv_rank)
    rows = dec_seq * mla_heads
    q_pe = q_cat[n_prompt:].reshape(n_sample, mla_heads, LANES)[:, :, nope:qk_dim].reshape(dec_batch, rows, rope_dim)
    tpad = BF16_ROWS
    ckv_new = jnp.zeros((dec_batch, tpad, kv_rank), F32).at[:, :dec_seq].set(
        ckv_all[n_prompt:].reshape(dec_batch, dec_seq, kv_rank))
    kpe_new = jnp.zeros((dec_batch, tpad, rope_dim), F32).at[:, :dec_seq].set(
        kpe_rows[n_prompt:].reshape(dec_batch, dec_seq, rope_dim))
    o_lat = _paged_attn(page_table, q_lat.reshape(dec_batch, rows, kv_rank), q_pe, ckv_new, kpe_new,
                        cache_ckv, jnp.swapaxes(cache_kpe, 2, 3), heads=mla_heads)
    wuv_h = jnp.transpose(mla_wuv[0], (1, 0, 2)).reshape(mla_heads // 2, 2, kv_rank, v_head)
    wuv_pair = jnp.zeros((mla_heads // 2, 2, kv_rank, 2, v_head), F32)
    wuv_pair = wuv_pair.at[:, 0, :, 0, :].set(wuv_h[:, 0]).at[:, 1, :, 1, :].set(wuv_h[:, 1])
    wuv_pair = wuv_pair.reshape(mla_heads // 2, 2 * kv_rank, LANES).astype(BF16)
    o_s = _olat(o_lat.reshape(n_sample, mla_heads * kv_rank), wuv_pair, heads=mla_heads, kv_rank=kv_rank)

    y = _matmul_res(o_p, o_s, mla_wo[0].astype(BF16), y, name="mla_out")
    y = _moe(y, norm_ffn[1], moe_wr[1], moe_br[1], moe_wgu, moe_bgu[1], moe_wd, moe_bd[1], 1)

    y_p, y_s = _rmsnorm(y, norm_final, n_prompt)
    return (
        y_p.reshape(batch, seq, d),
        y_s.reshape(dec_batch, dec_seq, d),
        state_p[None],
        state_s[None],
        ckv_all[:n_prompt].reshape(1, batch, seq, kv_rank),
        kpe_rows[:n_prompt].reshape(1, batch, seq, rope_dim),
        ckv_all[n_prompt:].reshape(1, dec_batch, dec_seq, kv_rank),
        kpe_rows[n_prompt:].reshape(1, dec_batch, dec_seq, rope_dim),
    )
```

```python
import functools
import math

import jax
import jax.numpy as jnp
from jax import lax
from jax.experimental import pallas as pl
from jax.experimental.pallas import tpu as pltpu

F32 = jnp.float32
BF16 = jnp.bfloat16

TOP_K = 4
SWIGLU_LIMIT = 7.0
SWIGLU_ALPHA = 1.702
NORM_EPS = 1e-6
GN_EPS = 1e-6
ROPE_BASE = 10000.0
LANES = 128
BF16_ROWS = 16
MXU_DIM = 256

ROW_TILE = 512
PROJ_ROW_TILE = 1024
PROJ_ROW_GROUPS = 2
RET_CHUNK = 256
RET_HEADS_PER_STEP = 2
MOE_BLOCK = 512
EXPERT_ROW_GROUPS = 2
COMBINE_TILE = 128
DISPATCH_TILE = 512
ZERO_GROUP = 4
ATTN_TILE = 512
ATTN_HEADS = 4
DMA_UNROLL = 64
VMEM_LIMIT = 56 * 1024 * 1024

_NT = (((1,), (1,)), ((), ()))


def _cparams(n_axes, vmem=VMEM_LIMIT):
    return pltpu.CompilerParams(dimension_semantics=("arbitrary",) * n_axes, vmem_limit_bytes=vmem)


def _rms(x, g):
    return x * lax.rsqrt(jnp.mean(x * x, axis=-1, keepdims=True) + NORM_EPS) * g


def _norm_matmul_kernel(x_ref, g_ref, w_ref, cos_ref, sin_ref, o_ref, h_ref, *, n_rope_tiles, head_dim):
    j = pl.program_id(1)

    @pl.when(j == 0)
    def _():
        h_ref[...] = _rms(x_ref[...], g_ref[...]).astype(BF16)

    rows = o_ref.shape[0] // PROJ_ROW_GROUPS
    groups = [slice(r * rows, (r + 1) * rows) for r in range(PROJ_ROW_GROUPS)]
    accs = [jnp.dot(h_ref[grp, :], w_ref[...], preferred_element_type=F32) for grp in groups]
    tn = o_ref.shape[1]
    for grp, acc in zip(groups, accs):
        if n_rope_tiles:
            roped = (j < n_rope_tiles).astype(F32)
            cos = roped * cos_ref[grp, :] + (1.0 - roped)
            sin = roped * sin_ref[grp, :]
            half = head_dim // 2
            for h in range(tn // head_dim):
                x1 = acc[:, h * head_dim:h * head_dim + half]
                x2 = acc[:, h * head_dim + half:(h + 1) * head_dim]
                o_ref[grp, h * head_dim:h * head_dim + half] = (x1 * cos - x2 * sin).astype(o_ref.dtype)
                o_ref[grp, h * head_dim + half:(h + 1) * head_dim] = (x1 * sin + x2 * cos).astype(o_ref.dtype)
        else:
            o_ref[grp, :] = acc.astype(o_ref.dtype)


def _proj_tile(n):
    return PROJ_ROW_TILE if n % PROJ_ROW_TILE == 0 else ROW_TILE


def _table_tile(i, prompt_tiles, seq_tiles):
    return jnp.where(i < prompt_tiles, i % seq_tiles, seq_tiles)


def _norm_matmul(x, g, w, cos, sin, *, n_prompt, seq, tn, n_rope_tiles, head_dim, out_dtype, name):
    n, d = x.shape
    nout = w.shape[1]
    tm = _proj_tile(n)
    pt, st = n_prompt // tm, seq // tm
    return pl.pallas_call(
        functools.partial(_norm_matmul_kernel, n_rope_tiles=n_rope_tiles, head_dim=head_dim),
        grid=(n // tm, nout // tn),
        in_specs=[
            pl.BlockSpec((tm, d), lambda i, j: (i, 0)),
            pl.BlockSpec((1, d), lambda i, j: (0, 0)),
            pl.BlockSpec((d, tn), lambda i, j: (0, j)),
            pl.BlockSpec((tm, cos.shape[1]), lambda i, j: (_table_tile(i, pt, st), 0)),
            pl.BlockSpec((tm, sin.shape[1]), lambda i, j: (_table_tile(i, pt, st), 0)),
        ],
        out_specs=pl.BlockSpec((tm, tn), lambda i, j: (i, j)),
        out_shape=jax.ShapeDtypeStruct((n, nout), out_dtype),
        scratch_shapes=[pltpu.VMEM((tm, d), BF16)],
        compiler_params=_cparams(2),
        name=name,
    )(x, g.reshape(1, d), w, cos, sin)


def _group_norm_gate(o, g, gn):
    mu = jnp.mean(o, axis=-1, keepdims=True)
    oc = o - mu
    var = jnp.mean(oc * oc, axis=-1, keepdims=True)
    on = oc * lax.rsqrt(var + GN_EPS) * gn
    gf = g.astype(F32)
    return (gf / (1.0 + jnp.exp(-gf))) * on


def _retention_step(q, k, v, state, lg):
    L = q.shape[0]
    row = lax.broadcasted_iota(jnp.int32, (L, L), 0)
    col = lax.broadcasted_iota(jnp.int32, (L, L), 1)
    diff = (row - col).astype(F32)
    decay = jnp.where(diff >= 0.0, jnp.exp(jnp.maximum(diff, 0.0) * lg), 0.0)
    scores = lax.dot_general(q, k, _NT, preferred_element_type=F32) * decay
    inner = jnp.dot(scores.astype(BF16), v, preferred_element_type=F32)
    idx = lax.broadcasted_iota(jnp.int32, (L, 1), 0).astype(F32)
    q_decay = jnp.exp((idx + 1.0) * lg)
    cross = jnp.dot(q, state.astype(BF16), preferred_element_type=F32) * q_decay
    k_decay = jnp.exp((L - 1.0 - idx) * lg)
    kd = (k.astype(F32) * k_decay).T.astype(BF16)
    chunk_decay = jnp.exp(jnp.full((1, 1), L, F32) * lg)
    new_state = chunk_decay * state + jnp.dot(kd, v, preferred_element_type=F32)
    return inner + cross, new_state


def _ret_prompt_kernel(lg_ref, q_ref, k_ref, v_ref, g_ref, gn_ref, o_ref, s_ref):
    hg = pl.program_id(1)
    c = pl.program_id(2)
    hp = s_ref.shape[1]
    dk, dv = s_ref.shape[2], s_ref.shape[3]

    @pl.when(c == 0)
    def _():
        s_ref[...] = jnp.zeros_like(s_ref)

    steps = []
    for j in range(hp):
        qk_cols = slice(j * dk, (j + 1) * dk)
        v_cols = slice(j * dv, (j + 1) * dv)
        steps.append(_retention_step(q_ref[:, qk_cols], k_ref[:, qk_cols], v_ref[:, v_cols], s_ref[0, j],
                                     lg_ref[hg * hp + j]))
    for j, (o, new_state) in enumerate(steps):
        v_cols = slice(j * dv, (j + 1) * dv)
        s_ref[0, j] = new_state
        o_ref[:, v_cols] = _group_norm_gate(o, g_ref[:, v_cols], gn_ref[:, v_cols]).astype(o_ref.dtype)


def _ret_prompt(qkvg, gn, lg, *, batch, seq, heads, dk, dv):
    L = min(RET_CHUNK, seq)
    nc = seq // L
    hp = RET_HEADS_PER_STEP
    ng = heads // hp
    kq = ng
    v0 = 2 * heads * dk // (hp * dv)
    g0 = v0 + ng
    grid_spec = pltpu.PrefetchScalarGridSpec(
        num_scalar_prefetch=1,
        grid=(batch, ng, nc),
        in_specs=[
            pl.BlockSpec((L, hp * dk), lambda b, h, c, lg: (b * nc + c, h)),
            pl.BlockSpec((L, hp * dk), lambda b, h, c, lg: (b * nc + c, kq + h)),
            pl.BlockSpec((L, hp * dv), lambda b, h, c, lg: (b * nc + c, v0 + h)),
            pl.BlockSpec((L, hp * dv), lambda b, h, c, lg: (b * nc + c, g0 + h)),
            pl.BlockSpec((1, hp * dv), lambda b, h, c, lg: (0, h)),
        ],
        out_specs=[
            pl.BlockSpec((L, hp * dv), lambda b, h, c, lg: (b * nc + c, h)),
            pl.BlockSpec((1, hp, dk, dv), lambda b, h, c, lg: (b, h, 0, 0)),
        ],
    )
    return pl.pallas_call(
        _ret_prompt_kernel,
        grid_spec=grid_spec,
        out_shape=[
            jax.ShapeDtypeStruct((batch * seq, heads * dv), BF16),
            jax.ShapeDtypeStruct((batch, heads, dk, dv), F32),
        ],
        compiler_params=_cparams(3),
        name="ret_prompt",
    )(lg, qkvg, qkvg, qkvg, qkvg, gn.reshape(1, -1))


def _ret_sample_kernel(lg_ref, q_ref, k_ref, v_ref, g_ref, gn_ref, s_in_ref, o_ref, s_out_ref, *, t_len):
    h = pl.program_id(1)
    lg = lg_ref[h]
    nb = s_in_ref.shape[0]
    n = nb * t_len
    q = q_ref[...]
    k = k_ref[...]
    v = v_ref[...]
    g = g_ref[...]
    gn = gn_ref[...]
    row = lax.broadcasted_iota(jnp.int32, (n, n), 0)
    col = lax.broadcasted_iota(jnp.int32, (n, n), 1)
    diff = (row - col).astype(F32)
    keep = (row // t_len == col // t_len) & (row >= col)
    decay = jnp.where(keep, jnp.exp(jnp.maximum(diff, 0.0) * lg), 0.0)
    scores = lax.dot_general(q, k, _NT, preferred_element_type=F32) * decay
    inner = jnp.dot(scores.astype(BF16), v, preferred_element_type=F32)
    idx = (lax.broadcasted_iota(jnp.int32, (n, 1), 0) % t_len).astype(F32)
    q_decay = jnp.exp((idx + 1.0) * lg)
    kd = k.astype(F32) * jnp.exp((t_len - 1.0 - idx) * lg)
    chunk_decay = jnp.exp(jnp.full((1, 1), t_len, F32) * lg)
    group = BF16_ROWS // t_len
    seq_of_row = lax.broadcasted_iota(jnp.int32, (BF16_ROWS, 1), 0) // t_len
    for p in range(nb // group):
        rows = slice(p * BF16_ROWS, (p + 1) * BF16_ROWS)
        q16, kd16, v16 = q[rows], kd[rows], v[rows]
        cross = jnp.zeros((BF16_ROWS, v.shape[1]), F32)
        for j in range(group):
            b = p * group + j
            state = s_in_ref[b, 0]
            mine = seq_of_row == j
            cross = jnp.where(mine, jnp.dot(q16, state.astype(BF16), preferred_element_type=F32), cross)
            kdb = jnp.where(mine, kd16, 0.0).T.astype(BF16)
            s_out_ref[b, 0] = chunk_decay * state + jnp.dot(kdb, v16, preferred_element_type=F32)
        o = inner[rows] + cross * q_decay[rows]
        o_ref[rows, :] = _group_norm_gate(o, g[rows], gn).astype(o_ref.dtype)


def _ret_sample(qkvg, gn, lg, state, *, row0, t_len, heads, dk, dv):
    nbatch = state.shape[0]
    nb = 8
    rows = nb * t_len
    r0 = row0 // rows
    kq = heads
    v0 = 2 * heads * dk // dv
    g0 = v0 + heads
    grid_spec = pltpu.PrefetchScalarGridSpec(
        num_scalar_prefetch=1,
        grid=(nbatch // nb, heads),
        in_specs=[
            pl.BlockSpec((rows, dk), lambda i, h, lg: (r0 + i, h)),
            pl.BlockSpec((rows, dk), lambda i, h, lg: (r0 + i, kq + h)),
            pl.BlockSpec((rows, dv), lambda i, h, lg: (r0 + i, v0 + h)),
            pl.BlockSpec((rows, dv), lambda i, h, lg: (r0 + i, g0 + h)),
            pl.BlockSpec((1, dv), lambda i, h, lg: (0, h)),
            pl.BlockSpec((nb, 1, dk, dv), lambda i, h, lg: (i, h, 0, 0)),
        ],
        out_specs=[
            pl.BlockSpec((rows, dv), lambda i, h, lg: (i, h)),
            pl.BlockSpec((nb, 1, dk, dv), lambda i, h, lg: (i, h, 0, 0)),
        ],
    )
    return pl.pallas_call(
        functools.partial(_ret_sample_kernel, t_len=t_len),
        grid_spec=grid_spec,
        out_shape=[
            jax.ShapeDtypeStruct((nbatch * t_len, heads * dv), BF16),
            jax.ShapeDtypeStruct(state.shape, F32),
        ],
        compiler_params=_cparams(2),
        name="ret_sample",
    )(lg, qkvg, qkvg, qkvg, qkvg, gn.reshape(1, -1), state)


def _matmul_res_kernel(xp_ref, xs_ref, w_ref, r_ref, o_ref, *, prompt_tiles):
    i = pl.program_id(0)

    @pl.when(i < prompt_tiles)
    def _():
        o_ref[...] = r_ref[...] + jnp.dot(xp_ref[...], w_ref[...], preferred_element_type=F32)

    @pl.when(i >= prompt_tiles)
    def _():
        o_ref[...] = r_ref[...] + jnp.dot(xs_ref[...], w_ref[...], preferred_element_type=F32)


def _matmul_res(x_prompt, x_sample, w, res, *, name):
    n, d = res.shape
    k = w.shape[0]
    tm = _proj_tile(n)
    pt = x_prompt.shape[0] // tm
    return pl.pallas_call(
        functools.partial(_matmul_res_kernel, prompt_tiles=pt),
        grid=(n // tm,),
        in_specs=[
            pl.BlockSpec((tm, k), lambda i: (jnp.minimum(i, pt - 1), 0)),
            pl.BlockSpec((tm, k), lambda i: (jnp.maximum(i - pt, 0), 0)),
            pl.BlockSpec((k, d), lambda i: (0, 0)),
            pl.BlockSpec((tm, d), lambda i: (i, 0)),
        ],
        out_specs=pl.BlockSpec((tm, d), lambda i: (i, 0)),
        out_shape=jax.ShapeDtypeStruct((n, d), F32),
        compiler_params=_cparams(1),
        name=name,
    )(x_prompt, x_sample, w, res)


def _router_kernel(x_ref, g_ref, wh_ref, wl_ref, b_ref, idx_ref, gate_ref, *, n_experts):
    xn = _rms(x_ref[...], g_ref[...])
    hi = xn.astype(BF16)
    lo = (xn - hi.astype(F32)).astype(BF16)
    logits = (jnp.dot(hi, wh_ref[...], preferred_element_type=F32)
              + jnp.dot(lo, wh_ref[...], preferred_element_type=F32)
              + jnp.dot(hi, wl_ref[...], preferred_element_type=F32)) + b_ref[...]
    lane = lax.broadcasted_iota(jnp.int32, logits.shape, 1).astype(F32)
    neg = jnp.float32(-jnp.inf)
    work = jnp.where(lane < n_experts, logits, neg)
    vals, idxs = [], []
    for _ in range(TOP_K):
        m = jnp.max(work, axis=-1, keepdims=True)
        sel = jnp.min(jnp.where(work == m, lane, float(LANES)), axis=-1, keepdims=True)
        vals.append(m)
        idxs.append(sel)
        work = jnp.where(lane == sel, neg, work)
    es = [jnp.exp(v - vals[0]) for v in vals]
    denom = es[0]
    for e in es[1:]:
        denom = denom + e
    idx_out = jnp.zeros(logits.shape, F32)
    gate_out = jnp.zeros(logits.shape, F32)
    for kk in range(TOP_K):
        idx_out = jnp.where(lane == kk, idxs[kk], idx_out)
        gate_out = jnp.where(lane == kk, es[kk] / denom, gate_out)
    idx_ref[...] = idx_out.astype(jnp.int32)
    gate_ref[...] = gate_out


def _router(y, g, wr, br):
    n, d = y.shape
    e = wr.shape[1]
    tm = ROW_TILE
    wr_pad = jnp.zeros((d, LANES), F32).at[:, :e].set(wr)
    wh = wr_pad.astype(BF16)
    wl = (wr_pad - wh.astype(F32)).astype(BF16)
    b_pad = jnp.zeros((1, LANES), F32).at[0, :e].set(br)
    return pl.pallas_call(
        functools.partial(_router_kernel, n_experts=e),
        grid=(n // tm,),
        in_specs=[
            pl.BlockSpec((tm, d), lambda i: (i, 0)),
            pl.BlockSpec((1, d), lambda i: (0, 0)),
            pl.BlockSpec((d, LANES), lambda i: (0, 0)),
            pl.BlockSpec((d, LANES), lambda i: (0, 0)),
            pl.BlockSpec((1, LANES), lambda i: (0, 0)),
        ],
        out_specs=[
            pl.BlockSpec((tm, LANES), lambda i: (i, 0)),
            pl.BlockSpec((tm, LANES), lambda i: (i, 0)),
        ],
        out_shape=[
            jax.ShapeDtypeStruct((n, LANES), jnp.int32),
            jax.ShapeDtypeStruct((n, LANES), F32),
        ],
        compiler_params=_cparams(1),
        name="moe_router",
    )(y, g.reshape(1, d), wh, wl, b_pad)


def _split_gate_up_kernel(w_ref, p_ref, o_ref):
    half = o_ref.shape[1] // 2
    hp = MXU_DIM // 2
    for c in range(o_ref.shape[1] // MXU_DIM):
        y = jnp.dot(w_ref[:, c * MXU_DIM:(c + 1) * MXU_DIM].astype(BF16), p_ref[...], preferred_element_type=F32)
        o_ref[:, c * hp:(c + 1) * hp] = y[:, :hp].astype(BF16)
        o_ref[:, half + c * hp:half + (c + 1) * hp] = y[:, hp:].astype(BF16)


def _split_gate_up(wgu_all, layer):
    _, e, d, w2 = wgu_all.shape
    rows = e * d
    tm = ROW_TILE
    row0 = layer * rows // tm
    src = lax.broadcasted_iota(jnp.int32, (MXU_DIM, MXU_DIM), 0)
    dst = lax.broadcasted_iota(jnp.int32, (MXU_DIM, MXU_DIM), 1)
    perm = (dst == (src % 2) * (MXU_DIM // 2) + src // 2).astype(BF16)
    out = pl.pallas_call(
        _split_gate_up_kernel,
        grid=(rows // tm,),
        in_specs=[pl.BlockSpec((tm, w2), lambda i: (row0 + i, 0)),
                  pl.BlockSpec((MXU_DIM, MXU_DIM), lambda i: (0, 0))],
        out_specs=pl.BlockSpec((tm, w2), lambda i: (i, 0)),
        out_shape=jax.ShapeDtypeStruct((rows, w2), BF16),
        compiler_params=_cparams(1),
        name="moe_split_gate_up",
    )(wgu_all.reshape(-1, w2), perm)
    return out.reshape(e, d, w2)


def _row_copy(src, dst, sem, src_row, dst_row):
    return pltpu.make_async_copy(src.at[pl.ds(src_row, 1)], dst.at[pl.ds(dst_row, 1)], sem)


def _dispatch_kernel(dest_ref, hi_ref, y_ref, xb_hbm, zblk, sem, *, n_experts):
    i = pl.program_id(0)
    tm = y_ref.shape[0]
    blk = zblk.shape[0]
    n_blocks = xb_hbm.shape[0] // blk

    def zero_block(b, wait):
        cp = pltpu.make_async_copy(zblk, xb_hbm.at[pl.ds(pl.multiple_of(b * blk, blk), blk)], sem.at[1])
        cp.wait() if wait else cp.start()

    @pl.when(i == 0)
    def _():
        zblk[...] = jnp.zeros_like(zblk)
        for g in range(0, n_experts, ZERO_GROUP):
            for wait in (False, True):
                for e in range(g, min(g + ZERO_GROUP, n_experts)):
                    @pl.when(hi_ref[e] > (hi_ref[e - 1] if e else 0))
                    def _():
                        zero_block(hi_ref[e] // blk - 1, wait)

        first_unused = hi_ref[n_experts - 1] // blk

        def tail_group(gi, _):
            for wait in (False, True):
                for j in range(ZERO_GROUP):
                    b = first_unused + gi * ZERO_GROUP + j

                    @pl.when(b < n_blocks)
                    def _():
                        zero_block(b, wait)
            return 0

        lax.fori_loop(0, (n_blocks - first_unused + ZERO_GROUP - 1) // ZERO_GROUP, tail_group, 0)

    def scatter(wait):
        def body(t, _):
            for kk in range(TOP_K):
                cp = _row_copy(y_ref, xb_hbm, sem.at[0], t, dest_ref[(i * tm + t) * TOP_K + kk])
                cp.wait() if wait else cp.start()
            return 0
        lax.fori_loop(0, tm, body, 0, unroll=DMA_UNROLL // TOP_K)

    scatter(False)
    scatter(True)


def _dispatch(y, dest, region_end, n_rows):
    n, d = y.shape
    tm = DISPATCH_TILE
    grid_spec = pltpu.PrefetchScalarGridSpec(
        num_scalar_prefetch=2,
        grid=(n // tm,),
        in_specs=[pl.BlockSpec((tm, d), lambda i, dest, hi: (i, 0))],
        out_specs=pl.BlockSpec(memory_space=pl.ANY),
        scratch_shapes=[pltpu.VMEM((MOE_BLOCK, d), F32), pltpu.SemaphoreType.DMA((2,))],
    )
    return pl.pallas_call(
        functools.partial(_dispatch_kernel, n_experts=region_end.shape[0]),
        grid_spec=grid_spec,
        out_shape=jax.ShapeDtypeStruct((n_rows, d), F32),
        compiler_params=_cparams(1),
        name="moe_dispatch",
    )(dest, region_end, y)


def _expert_kernel(be_ref, nv_ref, x_ref, g_ref, wgu_ref, bgu_ref, wd_ref, bd_ref, o_ref, wd_bf, *, d_expert):
    i = pl.program_id(0)

    @pl.when((i == 0) | (be_ref[i] != be_ref[jnp.maximum(i - 1, 0)]))
    def _():
        wd_bf[...] = wd_ref[0, 0].astype(BF16)

    @pl.when(i < nv_ref[0])
    def _():
        rows = o_ref.shape[0] // EXPERT_ROW_GROUPS
        groups = [slice(r * rows, (r + 1) * rows) for r in range(EXPERT_ROW_GROUPS)]
        hgus = [jnp.dot(_rms(x_ref[grp, :], g_ref[...]).astype(BF16), wgu_ref[0], preferred_element_type=F32)
                + bgu_ref[0] for grp in groups]
        for grp, hgu in zip(groups, hgus):
            gate = jnp.minimum(hgu[:, :d_expert], SWIGLU_LIMIT)
            up = jnp.clip(hgu[:, d_expert:], -SWIGLU_LIMIT, SWIGLU_LIMIT)
            act = (up + 1.0) * gate * (1.0 / (1.0 + jnp.exp(-(gate * SWIGLU_ALPHA))))
            o_ref[grp, :] = jnp.dot(act.astype(BF16), wd_bf[...], preferred_element_type=F32) + bd_ref[0]

    @pl.when(i >= nv_ref[0])
    def _():
        o_ref[...] = jnp.zeros_like(o_ref)


def _experts(xb, g, block_expert, n_valid, wgu, bgu, wd_all, layer, bd):
    r, d = xb.shape
    de = wd_all.shape[2]
    blk = MOE_BLOCK
    grid_spec = pltpu.PrefetchScalarGridSpec(
        num_scalar_prefetch=2,
        grid=(r // blk,),
        in_specs=[
            pl.BlockSpec((blk, d), lambda i, be, nv: (jnp.minimum(i, nv[0] - 1), 0)),
            pl.BlockSpec((1, d), lambda i, be, nv: (0, 0)),
            pl.BlockSpec((1, d, 2 * de), lambda i, be, nv: (be[i], 0, 0)),
            pl.BlockSpec((1, 1, 2 * de), lambda i, be, nv: (be[i], 0, 0)),
            pl.BlockSpec((1, 1, de, d), lambda i, be, nv: (layer, be[i], 0, 0)),
            pl.BlockSpec((1, 1, d), lambda i, be, nv: (be[i], 0, 0)),
        ],
        out_specs=pl.BlockSpec((blk, d), lambda i, be, nv: (i, 0)),
        scratch_shapes=[pltpu.VMEM((de, d), BF16)],
    )
    return pl.pallas_call(
        functools.partial(_expert_kernel, d_expert=de),
        grid_spec=grid_spec,
        out_shape=jax.ShapeDtypeStruct((r, d), F32),
        compiler_params=_cparams(1),
        name="moe_experts",
    )(block_expert, n_valid, xb, g.reshape(1, d), wgu, bgu, wd_all, bd)


def _combine_kernel(dest_ref, res_ref, gate_ref, yb_hbm, o_ref, buf, sem):
    i = pl.program_id(0)
    nsteps = pl.num_programs(0)
    tm = o_ref.shape[0]
    slot = i % 2

    def gather(tile, sl, wait):
        def body(t, _):
            for kk in range(TOP_K):
                cp = _row_copy(yb_hbm, buf.at[sl, kk], sem.at[sl], dest_ref[(tile * tm + t) * TOP_K + kk], t)
                cp.wait() if wait else cp.start()
            return 0
        lax.fori_loop(0, tm, body, 0, unroll=DMA_UNROLL // TOP_K)

    @pl.when(i == 0)
    def _():
        gather(0, 0, False)

    @pl.when(i + 1 < nsteps)
    def _():
        gather(i + 1, 1 - slot, False)

    gather(i, slot, True)
    acc = res_ref[...]
    gates = gate_ref[...]
    for kk in range(TOP_K):
        acc = acc + gates[:, kk:kk + 1] * buf[slot, kk]
    o_ref[...] = acc


def _combine(res, gate_pad, dest, yb):
    n, d = res.shape
    tm = COMBINE_TILE
    grid_spec = pltpu.PrefetchScalarGridSpec(
        num_scalar_prefetch=1,
        grid=(n // tm,),
        in_specs=[
            pl.BlockSpec((tm, d), lambda i, dest: (i, 0)),
            pl.BlockSpec((tm, LANES), lambda i, dest: (i, 0)),
            pl.BlockSpec(memory_space=pl.ANY),
        ],
        out_specs=pl.BlockSpec((tm, d), lambda i, dest: (i, 0)),
        scratch_shapes=[pltpu.VMEM((2, TOP_K, tm, d), F32), pltpu.SemaphoreType.DMA((2,))],
    )
    return pl.pallas_call(
        _combine_kernel,
        grid_spec=grid_spec,
        out_shape=jax.ShapeDtypeStruct((n, d), F32),
        compiler_params=_cparams(1),
        name="moe_combine",
    )(dest, res, gate_pad, yb)


def _moe(y, g, wr, br, wgu_all, bgu, wd_all, bd, layer):
    n, d = y.shape
    e = wr.shape[1]
    de = wd_all.shape[2]
    idx_pad, gate_pad = _router(y, g, wr, br)

    a = n * TOP_K
    blk = MOE_BLOCK
    flat_e = idx_pad[:, :TOP_K].reshape(a)
    onehot = (flat_e[:, None] == jnp.arange(e, dtype=jnp.int32)[None, :]).astype(jnp.int32)
    csum = jnp.cumsum(onehot, axis=0)
    counts = csum[-1]
    rank = jnp.take_along_axis(csum, flat_e[:, None], axis=1)[:, 0] - 1
    padded = ((counts + blk - 1) // blk) * blk
    ends = jnp.cumsum(padded)
    dest = ((ends - padded)[flat_e] + rank).astype(jnp.int32)
    n_blocks = -(-a // blk) + e
    r = n_blocks * blk
    block_start = jnp.arange(n_blocks, dtype=jnp.int32) * blk
    block_expert = jnp.minimum(jnp.sum((ends[None, :] <= block_start[:, None]).astype(jnp.int32), axis=1), e - 1)
    n_valid = (ends[-1:] // blk).astype(jnp.int32)

    xb = _dispatch(y, dest, ends.astype(jnp.int32), r)
    wgu_b = _split_gate_up(wgu_all, layer)
    bgu_b = jnp.concatenate([bgu[:, 0::2], bgu[:, 1::2]], axis=-1).reshape(e, 1, 2 * de)
    yb = _experts(xb, g, block_expert, n_valid, wgu_b, bgu_b, wd_all, layer, bd.reshape(e, 1, d))
    return _combine(y, gate_pad, dest, yb)


def _rope_lanes(x, c, sa, sb, half):
    n = x.shape[1]
    return x * c + pltpu.roll(x, n - half, 1) * sa + pltpu.roll(x, half, 1) * sb


def _mla_proj_kernel(dqkv_ref, gq_ref, gkv_ref, wuq_ref, wuk_ref, wuv_ref, ones_ref, c_ref, sa_ref, sb_ref,
                     q_ref, k_ref, v_ref, ckv_ref, kpe_ref, *, q_rank, kv_rank, heads, rope_half, scale):
    x = dqkv_ref[...]
    c = c_ref[...]
    sa = sa_ref[...]
    sb = sb_ref[...]
    cq = _rms(x[:, :q_rank], gq_ref[...]).astype(BF16)
    ckv = _rms(x[:, q_rank:q_rank + kv_rank], gkv_ref[...])
    ckv_ref[...] = ckv
    kpe = _rope_lanes(x[:, q_rank + kv_rank:], c, sa, sb, rope_half)
    kpe_ref[...] = kpe
    ckv_b = ckv.astype(BF16)
    q = jnp.dot(cq, wuq_ref[...], preferred_element_type=F32)
    k = jnp.dot(ckv_b, wuk_ref[...], preferred_element_type=F32)
    for h in range(heads):
        cols = slice(h * LANES, (h + 1) * LANES)
        q_ref[:, cols] = (_rope_lanes(q[:, cols], c, sa, sb, rope_half) * scale).astype(BF16)
        k_ref[:, cols] = (k[:, cols] + kpe).astype(BF16)
    v_ref[...] = (jnp.dot(ckv_b, wuv_ref[...], preferred_element_type=F32) + ones_ref[...]).astype(BF16)


def _mla_proj(dqkv, gq, gkv, wuq_p, wuk_p, wuv_p, ones_p, c, sa, sb, *, n_prompt, seq, q_rank, kv_rank, heads,
              rope_half, scale):
    n, w = dqkv.shape
    tm = ROW_TILE
    pt, st = n_prompt // tm, seq // tm
    hv = wuv_p.shape[1]
    full = lambda shape: pl.BlockSpec(shape, lambda i: (0,) * len(shape))
    rows = lambda width: pl.BlockSpec((tm, width), lambda i: (i, 0))
    table = pl.BlockSpec((tm, LANES), lambda i: (_table_tile(i, pt, st), 0))
    return pl.pallas_call(
        functools.partial(_mla_proj_kernel, q_rank=q_rank, kv_rank=kv_rank, heads=heads, rope_half=rope_half,
                          scale=scale),
        grid=(n // tm,),
        in_specs=[rows(w), full((1, q_rank)), full((1, kv_rank)), full(wuq_p.shape), full(wuk_p.shape),
                  full(wuv_p.shape), full((1, hv)), table, table, table],
        out_specs=[rows(heads * LANES), rows(heads * LANES), rows(hv), rows(kv_rank), rows(LANES)],
        out_shape=[
            jax.ShapeDtypeStruct((n, heads * LANES), BF16),
            jax.ShapeDtypeStruct((n, heads * LANES), BF16),
            jax.ShapeDtypeStruct((n, hv), BF16),
            jax.ShapeDtypeStruct((n, kv_rank), F32),
            jax.ShapeDtypeStruct((n, LANES), F32),
        ],
        compiler_params=_cparams(1),
        name="mla_proj",
    )(dqkv, gq.reshape(1, -1), gkv.reshape(1, -1), wuq_p, wuk_p, wuv_p, ones_p, c, sa, sb)


def _flash_kernel(q_ref, k_ref, v_ref, o_ref, *scratch, tile, heads, v_head):
    m_refs, acc_refs = scratch[:heads], scratch[heads:]
    qi = pl.program_id(2)
    for h in range(heads):
        m_refs[h][...] = jnp.full((tile, LANES), -jnp.inf, F32)
        acc_refs[h][...] = jnp.zeros((tile, LANES), F32)
    nchunk = tile // LANES

    def step(ki, masked):
        start = pl.multiple_of(ki * tile, tile)
        scores = []
        for h in range(heads):
            cols = slice(h * LANES, (h + 1) * LANES)
            s = lax.dot_general(q_ref[:, cols], k_ref[pl.ds(start, tile), cols], _NT, preferred_element_type=F32)
            if masked:
                row = lax.broadcasted_iota(jnp.int32, s.shape, 0)
                col = lax.broadcasted_iota(jnp.int32, s.shape, 1)
                s = jnp.where(col <= row, s, jnp.finfo(F32).min)
            scores.append(s)
        for h in range(heads):
            cols = slice(h * LANES, (h + 1) * LANES)
            chunks = [scores[h][:, c * LANES:(c + 1) * LANES] for c in range(nchunk)]
            part = chunks[0]
            for ch in chunks[1:]:
                part = jnp.maximum(part, ch)
            m_old = m_refs[h][...]
            m_new = jnp.maximum(m_old, jnp.max(part, axis=-1, keepdims=True))
            m_refs[h][...] = m_new
            p = jnp.concatenate([jnp.exp2(ch - m_new) for ch in chunks], axis=1).astype(BF16)
            acc_refs[h][...] = (jnp.exp2(m_old - m_new) * acc_refs[h][...]
                                + jnp.dot(p, v_ref[pl.ds(start, tile), cols], preferred_element_type=F32))

    def body(kp, carry):
        step(2 * kp, False)
        step(2 * kp + 1, False)
        return carry

    lax.fori_loop(0, qi // 2, body, 0)

    @pl.when(qi % 2 == 1)
    def _():
        step(qi - 1, False)

    step(qi, True)
    lane = lax.broadcasted_iota(jnp.int32, (tile, LANES), 1)
    for j in range(heads // 2):
        even = acc_refs[2 * j][...]
        odd = acc_refs[2 * j + 1][...]
        num = jnp.where(lane < v_head, even, odd)
        den = jnp.where(lane < v_head, pltpu.roll(even, v_head, 1), pltpu.roll(odd, v_head, 1))
        o_ref[:, j * LANES:(j + 1) * LANES] = (num / den).astype(o_ref.dtype)


def _flash(q_cat, k_cat, v_ext, *, batch, seq, heads, v_head):
    tile = min(ATTN_TILE, seq)
    nq = seq // tile
    hg = ATTN_HEADS
    return pl.pallas_call(
        functools.partial(_flash_kernel, tile=tile, heads=hg, v_head=v_head),
        grid=(batch, heads // hg, nq),
        in_specs=[
            pl.BlockSpec((tile, hg * LANES), lambda b, g, qi: (b * nq + qi, g)),
            pl.BlockSpec((seq, hg * LANES), lambda b, g, qi: (b, g)),
            pl.BlockSpec((seq, hg * LANES), lambda b, g, qi: (b, g)),
        ],
        out_specs=pl.BlockSpec((tile, hg * v_head), lambda b, g, qi: (b * nq + qi, g)),
        out_shape=jax.ShapeDtypeStruct((batch * seq, heads * v_head), BF16),
        scratch_shapes=[pltpu.VMEM((tile, LANES), F32)] * (2 * hg),
        compiler_params=_cparams(3),
        name="mla_flash",
    )(q_cat, k_cat, v_ext)


def _qlat_kernel(q_ref, w_ref, o_ref, *, heads, kv_rank):
    for h in range(heads):
        o_ref[:, h * kv_rank:(h + 1) * kv_rank] = jnp.dot(
            q_ref[:, h * LANES:(h + 1) * LANES], w_ref[h], preferred_element_type=F32).astype(o_ref.dtype)


def _qlat(q_cat, wukt_p, *, row0, nrows, heads, kv_rank):
    tm = min(ROW_TILE, nrows)
    r0 = row0 // tm
    return pl.pallas_call(
        functools.partial(_qlat_kernel, heads=heads, kv_rank=kv_rank),
        grid=(nrows // tm,),
        in_specs=[
            pl.BlockSpec((tm, heads * LANES), lambda i: (r0 + i, 0)),
            pl.BlockSpec(wukt_p.shape, lambda i: (0, 0, 0)),
        ],
        out_specs=pl.BlockSpec((tm, heads * kv_rank), lambda i: (i, 0)),
        out_shape=jax.ShapeDtypeStruct((nrows, heads * kv_rank), BF16),
        compiler_params=_cparams(1),
        name="mla_qlat",
    )(q_cat, wukt_p)


def _olat_kernel(x_ref, w_ref, o_ref, *, pairs, kv_rank):
    for p in range(pairs):
        o_ref[:, p * LANES:(p + 1) * LANES] = jnp.dot(
            x_ref[:, p * 2 * kv_rank:(p + 1) * 2 * kv_rank], w_ref[p], preferred_element_type=F32).astype(o_ref.dtype)


def _olat(o_lat, wuv_pair, *, heads, kv_rank):
    n = o_lat.shape[0]
    tm = min(ROW_TILE, n)
    return pl.pallas_call(
        functools.partial(_olat_kernel, pairs=heads // 2, kv_rank=kv_rank),
        grid=(n // tm,),
        in_specs=[
            pl.BlockSpec((tm, heads * kv_rank), lambda i: (i, 0)),
            pl.BlockSpec(wuv_pair.shape, lambda i: (0, 0, 0)),
        ],
        out_specs=pl.BlockSpec((tm, (heads // 2) * LANES), lambda i: (i, 0)),
        out_shape=jax.ShapeDtypeStruct((n, (heads // 2) * LANES), BF16),
        compiler_params=_cparams(1),
        name="mla_olat",
    )(o_lat, wuv_pair)


def _paged_copies(pt_ref, cache_ckv, cache_kpe, ckv_buf, kpe_buf, sem, b, slot, p, page):
    pg = pt_ref[b, p]
    return (
        pltpu.make_async_copy(cache_ckv.at[0, pg], ckv_buf.at[slot, pl.ds(p * page, page)], sem.at[0, slot]),
        pltpu.make_async_copy(cache_kpe.at[0, pg], kpe_buf.at[slot, :, pl.ds(p * page, page)], sem.at[1, slot]),
    )


def _paged_attn_kernel(pt_ref, qlat_ref, qpe_ref, cnew_ref, pnew_ref, cache_ckv, cache_kpe, o_ref,
                       ckv_buf, kpe_buf, sem, *, n_pages, page, heads):
    b = pl.program_id(0)
    nb = pl.num_programs(0)
    slot = b % 2

    def start_all(bb, sl):
        def body(p, _):
            for cp in _paged_copies(pt_ref, cache_ckv, cache_kpe, ckv_buf, kpe_buf, sem, bb, sl, p, page):
                cp.start()
            return 0
        lax.fori_loop(0, n_pages, body, 0)

    @pl.when(b == 0)
    def _():
        start_all(0, 0)

    @pl.when(b + 1 < nb)
    def _():
        start_all(b + 1, 1 - slot)

    def wait_body(p, _):
        for cp in _paged_copies(pt_ref, cache_ckv, cache_kpe, ckv_buf, kpe_buf, sem, b, slot, p, page):
            cp.wait()
        return 0
    lax.fori_loop(0, n_pages, wait_body, 0)

    qlat = qlat_ref[0]
    qpe = qpe_ref[0]
    ck = ckv_buf[slot].astype(BF16)
    kp = kpe_buf[slot].astype(BF16)
    s_past = (lax.dot_general(qlat, ck, _NT, preferred_element_type=F32)
              + jnp.dot(qpe, kp, preferred_element_type=F32))
    cn = cnew_ref[0].astype(BF16)
    pn = pnew_ref[0].astype(BF16)
    s_new = (lax.dot_general(qlat, cn, _NT, preferred_element_type=F32)
             + lax.dot_general(qpe, pn, _NT, preferred_element_type=F32))
    t_of_row = lax.broadcasted_iota(jnp.int32, s_new.shape, 0) // heads
    j = lax.broadcasted_iota(jnp.int32, s_new.shape, 1)
    s_new = jnp.where(j <= t_of_row, s_new, jnp.finfo(F32).min)
    m = jnp.maximum(jnp.max(s_past, axis=-1, keepdims=True), jnp.max(s_new, axis=-1, keepdims=True))
    p_past = jnp.exp2(s_past - m)
    p_new = jnp.exp2(s_new - m)
    denom = jnp.sum(p_past, axis=-1, keepdims=True) + jnp.sum(p_new, axis=-1, keepdims=True)
    o = (jnp.dot(p_past.astype(BF16), ck, preferred_element_type=F32)
         + jnp.dot(p_new.astype(BF16), cn, preferred_element_type=F32))
    o_ref[0] = (o / denom).astype(o_ref.dtype)


def _paged_attn(page_table, qlat, qpe, ckv_new, kpe_new, cache_ckv, cache_kpe, *, heads):
    nbatch, n_pages = page_table.shape
    page, kv_rank = cache_ckv.shape[2], cache_ckv.shape[3]
    rope = cache_kpe.shape[2]
    rows = qlat.shape[1]
    tpad = ckv_new.shape[1]
    grid_spec = pltpu.PrefetchScalarGridSpec(
        num_scalar_prefetch=1,
        grid=(nbatch,),
        in_specs=[
            pl.BlockSpec((1, rows, kv_rank), lambda b, pt: (b, 0, 0)),
            pl.BlockSpec((1, rows, rope), lambda b, pt: (b, 0, 0)),
            pl.BlockSpec((1, tpad, kv_rank), lambda b, pt: (b, 0, 0)),
            pl.BlockSpec((1, tpad, rope), lambda b, pt: (b, 0, 0)),
            pl.BlockSpec(memory_space=pl.ANY),
            pl.BlockSpec(memory_space=pl.ANY),
        ],
        out_specs=pl.BlockSpec((1, rows, kv_rank), lambda b, pt: (b, 0, 0)),
        scratch_shapes=[
            pltpu.VMEM((2, n_pages * page, kv_rank), F32),
            pltpu.VMEM((2, rope, n_pages * page), F32),
            pltpu.SemaphoreType.DMA((2, 2)),
        ],
    )
    return pl.pallas_call(
        functools.partial(_paged_attn_kernel, n_pages=n_pages, page=page, heads=heads),
        grid_spec=grid_spec,
        out_shape=jax.ShapeDtypeStruct((nbatch, rows, kv_rank), BF16),
        compiler_params=_cparams(1),
        name="mla_paged_attn",
    )(page_table, qlat, qpe, ckv_new, kpe_new, cache_ckv, cache_kpe)


def _rmsnorm_kernel(x_ref, g_ref, op_ref, os_ref, *, prompt_tiles):
    i = pl.program_id(0)
    y = _rms(x_ref[...], g_ref[...])

    @pl.when(i < prompt_tiles)
    def _():
        op_ref[...] = y

    @pl.when(i >= prompt_tiles)
    def _():
        os_ref[...] = y


def _rmsnorm(x, g, n_prompt):
    n, d = x.shape
    tm = ROW_TILE
    pt = n_prompt // tm
    return pl.pallas_call(
        functools.partial(_rmsnorm_kernel, prompt_tiles=pt),
        grid=(n // tm,),
        in_specs=[pl.BlockSpec((tm, d), lambda i: (i, 0)), pl.BlockSpec((1, d), lambda i: (0, 0))],
        out_specs=[
            pl.BlockSpec((tm, d), lambda i: (jnp.minimum(i, pt - 1), 0)),
            pl.BlockSpec((tm, d), lambda i: (jnp.maximum(i - pt, 0), 0)),
        ],
        out_shape=[jax.ShapeDtypeStruct((n_prompt, d), F32), jax.ShapeDtypeStruct((n - n_prompt, d), F32)],
        compiler_params=_cparams(1),
        name="final_norm",
    )(x, g.reshape(1, d))


def _rope_angles(pos, half):
    inv = ROPE_BASE ** (-jnp.arange(half, dtype=F32) / half)
    return pos.astype(F32)[:, None] * inv[None, :]


def _table_positions(seq, dec_seq, past_len, tm):
    return jnp.concatenate([jnp.arange(seq), jnp.tile(past_len + jnp.arange(dec_seq), tm // dec_seq)])


def kernel(x_prompt, x_sample, state_ret, cache_ckv, cache_kpe, page_table, norm_mix, norm_ffn, norm_final, ret_wq, ret_wk, ret_wv, ret_wg, ret_gn, ret_wo, mla_wdq, mla_gq, mla_wuq, mla_wdkv, mla_gkv, mla_wuk, mla_wuv, mla_wo, moe_wr, moe_br, moe_wgu, moe_bgu, moe_wd, moe_bd):
    batch, seq, d = x_prompt.shape
    dec_batch, dec_seq, _ = x_sample.shape
    ret_heads, dk, dv = state_ret.shape[2], state_ret.shape[3], state_ret.shape[4]
    page = cache_ckv.shape[2]
    kv_rank = cache_ckv.shape[3]
    rope_dim = cache_kpe.shape[3]
    past_len = page_table.shape[1] * page
    mla_heads, nope = mla_wuk.shape[2], mla_wuk.shape[3]
    v_head = mla_wuv.shape[3]
    q_rank = mla_wdq.shape[2]
    n_prompt = batch * seq
    n_sample = dec_batch * dec_seq
    assert nope + rope_dim <= LANES and 2 * v_head == LANES and dk == 2 * LANES

    y = jnp.concatenate([x_prompt.reshape(n_prompt, d), x_sample.reshape(n_sample, d)], axis=0)
    n_all = n_prompt + n_sample
    assert n_sample % _proj_tile(n_all) == 0 and seq % _proj_tile(n_all) == 0 and ROW_TILE % dec_seq == 0

    ang = _rope_angles(_table_positions(seq, dec_seq, past_len, _proj_tile(n_all)), dk // 2)
    w_all = jnp.concatenate([ret_wq[0], ret_wk[0] * (dk ** -0.5), ret_wv[0], ret_wg[0]], axis=1).astype(BF16)
    qkvg = _norm_matmul(y, norm_mix[0], w_all, jnp.cos(ang), jnp.sin(ang), n_prompt=n_prompt, seq=seq,
                        tn=ret_heads * dk, n_rope_tiles=2, head_dim=dk, out_dtype=BF16, name="ret_proj")
    lg = jnp.log(1.0 - 2.0 ** (-5.0 - jnp.arange(ret_heads, dtype=F32)))
    gated_p, state_p = _ret_prompt(qkvg, ret_gn[0], lg, batch=batch, seq=seq, heads=ret_heads, dk=dk, dv=dv)
    gated_s, state_s = _ret_sample(qkvg, ret_gn[0], lg, state_ret[0], row0=n_prompt, t_len=dec_seq,
                                   heads=ret_heads, dk=dk, dv=dv)
    y = _matmul_res(gated_p, gated_s, ret_wo[0].astype(BF16), y, name="ret_out")
    y = _moe(y, norm_ffn[0], moe_wr[0], moe_br[0], moe_wgu, moe_bgu[0], moe_wd, moe_bd[0], 0)

    half = rope_dim // 2
    ang = _rope_angles(_table_positions(seq, dec_seq, past_len, ROW_TILE), half)
    cos2 = jnp.concatenate([jnp.cos(ang), jnp.cos(ang)], axis=1)
    sin = jnp.sin(ang)
    ones = jnp.ones((ang.shape[0], LANES), F32)
    zeros = jnp.zeros((ang.shape[0], LANES), F32)
    c_tab = lax.dynamic_update_slice(ones, cos2, (0, nope))
    sa_tab = lax.dynamic_update_slice(zeros, -sin, (0, nope))
    sb_tab = lax.dynamic_update_slice(zeros, sin, (0, nope + half))

    wdkv = mla_wdkv[0]
    kpe_cols = jnp.zeros((d, LANES), F32).at[:, nope:nope + rope_dim].set(wdkv[:, kv_rank:])
    w_down = jnp.concatenate([mla_wdq[0], wdkv[:, :kv_rank], kpe_cols], axis=1).astype(BF16)
    no_rope = jnp.zeros((seq + _proj_tile(n_all), LANES), F32)
    dqkv = _norm_matmul(y, norm_mix[1], w_down, no_rope, no_rope, n_prompt=n_prompt, seq=seq, tn=w_down.shape[1],
                        n_rope_tiles=0, head_dim=0, out_dtype=F32, name="mla_down")

    qk_dim = nope + rope_dim
    wuq_p = jnp.zeros((q_rank, mla_heads, LANES), F32).at[:, :, :qk_dim].set(
        mla_wuq[0].reshape(q_rank, mla_heads, qk_dim)).reshape(q_rank, mla_heads * LANES).astype(BF16)
    wuk_p = jnp.zeros((kv_rank, mla_heads, LANES), F32).at[:, :, :nope].set(mla_wuk[0]).reshape(
        kv_rank, mla_heads * LANES).astype(BF16)
    wuv_pairs = mla_wuv[0].reshape(kv_rank, mla_heads // 2, 2, v_head)
    wuv_p = jnp.zeros((kv_rank, mla_heads // 2, 2, 2, v_head), F32)
    wuv_p = wuv_p.at[:, :, 0, 0].set(wuv_pairs[:, :, 0]).at[:, :, 1, 1].set(wuv_pairs[:, :, 1])
    wuv_p = wuv_p.reshape(kv_rank, mla_heads * LANES).astype(BF16)
    ones_p = jnp.zeros((mla_heads // 2, 2, 2, v_head), F32).at[:, 0, 1].set(1.0).at[:, 1, 0].set(1.0)
    ones_p = ones_p.reshape(1, mla_heads * LANES)
    scale = (float(qk_dim) ** -0.5) * math.log2(math.e)
    q_cat, k_cat, v_ext, ckv_all, kpe_all = _mla_proj(
        dqkv, mla_gq[0], mla_gkv[0], wuq_p, wuk_p, wuv_p, ones_p, c_tab, sa_tab, sb_tab,
        n_prompt=n_prompt, seq=seq, q_rank=q_rank, kv_rank=kv_rank, heads=mla_heads, rope_half=half, scale=scale)
    kpe_rows = kpe_all[:, nope:nope + rope_dim]

    o_p = _flash(q_cat, k_cat, v_ext, batch=batch, seq=seq, heads=mla_heads, v_head=v_head)

    wukt_p = jnp.zeros((mla_heads, LANES, kv_rank), F32).at[:, :nope, :].set(
        jnp.transpose(mla_wuk[0], (1, 2, 0))).astype(BF16)
    q_lat = _qlat(q_cat, wukt_p, row0=n_prompt, nrows=n_sample, heads=mla_heads, kv_rank=kv_rank)
    rows = dec_seq * mla_heads
    q_pe = q_cat[n_prompt:].reshape(n_sample, mla_heads, LANES)[:, :, nope:qk_dim].reshape(dec_batch, rows, rope_dim)
    tpad = BF16_ROWS
    ckv_new = jnp.zeros((dec_batch, tpad, kv_rank), F32).at[:, :dec_seq].set(
        ckv_all[n_prompt:].reshape(dec_batch, dec_seq, kv_rank))
    kpe_new = jnp.zeros((dec_batch, tpad, rope_dim), F32).at[:, :dec_seq].set(
        kpe_rows[n_prompt:].reshape(dec_batch, dec_seq, rope_dim))
    o_lat = _paged_attn(page_table, q_lat.reshape(dec_batch, rows, kv_rank), q_pe, ckv_new, kpe_new,
                        cache_ckv, jnp.swapaxes(cache_kpe, 2, 3), heads=mla_heads)
    wuv_h = jnp.transpose(mla_wuv[0], (1, 0, 2)).reshape(mla_heads // 2, 2, kv_rank, v_head)
    wuv_pair = jnp.zeros((mla_heads // 2, 2, kv_rank, 2, v_head), F32)
    wuv_pair = wuv_pair.at[:, 0, :, 0, :].set(wuv_h[:, 0]).at[:, 1, :, 1, :].set(wuv_h[:, 1])
    wuv_pair = wuv_pair.reshape(mla_heads // 2, 2 * kv_rank, LANES).astype(BF16)
    o_s = _olat(o_lat.reshape(n_sample, mla_heads * kv_rank), wuv_pair, heads=mla_heads, kv_rank=kv_rank)

    y = _matmul_res(o_p, o_s, mla_wo[0].astype(BF16), y, name="mla_out")
    y = _moe(y, norm_ffn[1], moe_wr[1], moe_br[1], moe_wgu, moe_bgu[1], moe_wd, moe_bd[1], 1)

    y_p, y_s = _rmsnorm(y, norm_final, n_prompt)
    return (
        y_p.reshape(batch, seq, d),
        y_s.reshape(dec_batch, dec_seq, d),
        state_p[None],
        state_s[None],
        ckv_all[:n_prompt].reshape(1, batch, seq, kv_rank),
        kpe_rows[:n_prompt].reshape(1, batch, seq, rope_dim),
        ckv_all[n_prompt:].reshape(1, dec_batch, dec_seq, kv_rank),
        kpe_rows[n_prompt:].reshape(1, dec_batch, dec_seq, rope_dim),
    )
```
